```python
import math
import jax, jax.numpy as jnp
from jax import lax
import numpy as np

D_MODEL = 1024
BATCH = 8
SEQ = 4096
DEPTH = 4

GRID_W = 64
CTX_LEN = 256

BRANCH_W = D_MODEL // 2
N_BRANCHES = 3
GLA_HEADS = 4
GLA_DV = BRANCH_W // GLA_HEADS
GLA_DK = GLA_DV // 2
GLA_RANK = 16
GLA_TAU = 16.0
GLA_CHUNK = 64
DIFF_HEADS = 4
DIFF_DV = BRANCH_W // DIFF_HEADS
DIFF_DH = DIFF_DV // 2
DIFF_QBLOCK = 128
NA_HEADS = 8
NA_DH = BRANCH_W // NA_HEADS
NA_WIN_H = 8
NA_WIN_W = 16
NA_QCOLS = 16
NA_KCOLS = NA_QCOLS + NA_WIN_W
N_EXPERTS = 16
N_GROUPS = 4
EXPERTS_PER_GROUP = N_EXPERTS // N_GROUPS
TOP_K = 2
D_EXPERT = D_MODEL // 2
ROPE_BASE = 10000.0
LN_EPS = 1e-5
RMS_EPS = 1e-6
NEG_BIG = -1e30
DEEPNORM_ALPHA = (2 * DEPTH) ** 0.25
DEEPNORM_BETA = (8 * DEPTH) ** -0.25
IN_SPLITS = (
    GLA_HEADS * GLA_DK, GLA_HEADS * GLA_DK, GLA_HEADS * GLA_DV, GLA_HEADS * GLA_DV, 2 * GLA_RANK,
    DIFF_HEADS * 2 * DIFF_DH, DIFF_HEADS * 2 * DIFF_DH, DIFF_HEADS * DIFF_DV,
    NA_HEADS * NA_DH, NA_HEADS * NA_DH, NA_HEADS * NA_DH,
    N_BRANCHES * D_MODEL,
)
D_IN = sum(IN_SPLITS)

kernel_name = "hybrid_gla_diffattn_natten_groupmoe_dit"


def _layernorm(x, g, b):
    xf = x.astype(jnp.float32)
    mu = jnp.mean(xf, -1, keepdims=True)
    var = jnp.mean(jnp.square(xf - mu), -1, keepdims=True)
    return ((xf - mu) * lax.rsqrt(var + LN_EPS) * g + b).astype(x.dtype)


def _rms(x, g):
    xf = x.astype(jnp.float32)
    return (xf * lax.rsqrt(jnp.mean(xf * xf, -1, keepdims=True) + RMS_EPS) * g).astype(x.dtype)


def _heads(t, h):
    b, l, _ = t.shape
    return t.reshape(b, l, h, -1).transpose(0, 2, 1, 3)


def _merge_heads(o):
    b, h, l, d = o.shape
    return o.transpose(0, 2, 1, 3).reshape(b, l, h * d)


def _axial_rope(n_tok, dim):
    t = jnp.arange(n_tok)
    row = (t // GRID_W).astype(jnp.float32)
    col = (t % GRID_W).astype(jnp.float32)
    n_freq = dim // 4
    inv = ROPE_BASE ** (-jnp.arange(n_freq, dtype=jnp.float32) / n_freq)
    ang = jnp.concatenate([row[:, None] * inv, col[:, None] * inv], -1)
    return jnp.cos(ang), jnp.sin(ang)


def _rope(x, cos, sin):
    x1, x2 = jnp.split(x, 2, axis=-1)
    c = cos.astype(x.dtype)
    s = sin.astype(x.dtype)
    return jnp.concatenate([x1 * c - x2 * s, x1 * s + x2 * c], -1)


def _gla_chunk_scan(q, k, v, logd, s0):
    b, h, l, dk = q.shape
    dv = v.shape[-1]
    n = l // GLA_CHUNK

    def chunks(t):
        return t.reshape(b, h, n, GLA_CHUNK, t.shape[-1]).transpose(2, 0, 1, 3, 4)

    qc, kc, vc, gc = chunks(q), chunks(k), chunks(v), chunks(logd)
    cum = jnp.cumsum(gc, axis=-2)
    last = cum[..., -1:, :]
    q_in = qc * jnp.exp(cum)
    k_in = kc * jnp.exp(-cum)
    k_st = kc * jnp.exp(last - cum)
    causal_in_chunk = jnp.tril(jnp.ones((GLA_CHUNK, GLA_CHUNK), dtype=bool))
    att = jnp.where(causal_in_chunk, jnp.einsum('nbhid,nbhjd->nbhij', q_in, k_in), 0.0)
    o_intra = jnp.einsum('nbhij,nbhjv->nbhiv', att, vc)
    kv = jnp.einsum('nbhjd,nbhjv->nbhdv', k_st, vc)
    decay = jnp.exp(last[..., 0, :])

    def step(s, inp):
        dec, kv_n = inp
        return dec[..., None] * s + kv_n, s

    s_final, s_start = lax.scan(step, s0, (decay, kv))
    o_inter = jnp.einsum('nbhid,nbhdv->nbhiv', q_in, s_start)
    o = (o_intra + o_inter).transpose(1, 2, 0, 3, 4).reshape(b, h, l, dv)
    return o, s_final


def _gla_mixer(parts_l, parts_c, w_decay, b_decay, norm_g, need_ctx):
    f32 = jnp.float32

    def prep(parts):
        q, k, v, _, r = parts
        b, l, _ = q.shape
        qh = _heads(q.astype(f32), GLA_HEADS) * (GLA_DK ** -0.5)
        kh = _heads(k.astype(f32), GLA_HEADS)
        vh = _heads(v.astype(f32), GLA_HEADS)
        r2 = r.astype(f32).reshape(b, l, 2, GLA_RANK)
        z = jnp.einsum('bldr,drk->dblk', r2, w_decay.astype(f32)) + b_decay.astype(f32)[:, None, None, :]
        logd = jax.nn.log_sigmoid(z) / GLA_TAU
        return qh, kh, vh, _heads(logd[0], GLA_HEADS), _heads(logd[1], GLA_HEADS)

    qL, kL, vL, dLf, dLb = prep(parts_l)
    qC, kC, vC, dCf, dCb = prep(parts_c)
    b = qL.shape[0]
    s_zero = jnp.zeros((b, GLA_HEADS, GLA_DK, GLA_DV), f32)
    flip = lambda t: t[:, :, ::-1]
    oCf, sCf = _gla_chunk_scan(qC, kC, vC, dCf, s_zero)
    oLf, _ = _gla_chunk_scan(qL, kL, vL, dLf, sCf)
    oCb, sCb = _gla_chunk_scan(flip(qC), flip(kC), flip(vC), flip(dCb), s_zero)
    oLb, _ = _gla_chunk_scan(flip(qL), flip(kL), flip(vL), flip(dLb), sCb)

    def finish(o, g):
        o = _merge_heads(_rms(o, norm_g.astype(f32)))
        return (o * jax.nn.silu(g.astype(f32))).astype(g.dtype)

    out_l = finish(oLf + flip(oLb), parts_l[3])
    out_c = finish(oCf + flip(oCb), parts_c[3]) if need_ctx else None
    return out_l, out_c


def _diff_mixer(qL, kL, vL, qC, kC, vC, lam_q, lam_k, norm_g, lam_init, need_ctx):
    b, l, _ = qL.shape
    scale = DIFF_DH ** -0.5

    def qk_heads(t):
        return t.reshape(b, t.shape[1], DIFF_HEADS, 2, DIFF_DH).transpose(0, 2, 3, 1, 4)

    cos, sin = _axial_rope(l, DIFF_DH)
    qLh = _rope(qk_heads(qL), cos, sin) * scale
    kLh = _rope(qk_heads(kL), cos, sin)
    qCh = qk_heads(qC) * scale
    kCh = qk_heads(kC)
    vLh = _heads(vL, DIFF_HEADS)
    vCh = _heads(vC, DIFF_HEADS)
    lq = lam_q.astype(jnp.float32)
    lk = lam_k.astype(jnp.float32)
    lam = jnp.exp(jnp.sum(lq[0] * lk[0])) - jnp.exp(jnp.sum(lq[1] * lk[1])) + lam_init

    def attend(q, k, v):
        s = jnp.einsum('bhcqd,bhckd->bhcqk', q, k).astype(jnp.float32)
        p = jax.nn.softmax(s, axis=-1)
        w = (p[:, :, 0] - lam * p[:, :, 1]).astype(v.dtype)
        return jnp.einsum('bhqk,bhkv->bhqv', w, v)

    k_all = jnp.concatenate([kCh, kLh], axis=3)
    v_all = jnp.concatenate([vCh, vLh], axis=2)
    nb = l // DIFF_QBLOCK
    q_blocks = qLh.reshape(b, DIFF_HEADS, 2, nb, DIFF_QBLOCK, DIFF_DH).transpose(3, 0, 1, 2, 4, 5)
    oL = lax.map(lambda qb: attend(qb, k_all, v_all), q_blocks)
    oL = oL.transpose(1, 2, 0, 3, 4).reshape(b, DIFF_HEADS, l, DIFF_DV)

    def finish(o):
        return _merge_heads(_rms(o, norm_g) * (1.0 - lam_init))

    out_c = finish(attend(qCh, kCh, vCh)) if need_ctx else None
    return finish(oL), out_c


def _na_mixer(qL, kL, vL, qC, kC, vC, rpb, need_ctx):
    b, l, _ = qL.shape
    rows = l // GRID_W
    wh = min(NA_WIN_H, rows)
    ncb = GRID_W // NA_QCOLS
    n_loc = wh * NA_KCOLS
    q = _heads(qL, NA_HEADS) * (NA_DH ** -0.5)
    k_grid = _heads(kL, NA_HEADS).reshape(b, NA_HEADS, rows, GRID_W, NA_DH)
    v_grid = _heads(vL, NA_HEADS).reshape(b, NA_HEADS, rows, GRID_W, NA_DH)
    kc = _heads(kC, NA_HEADS)
    vc = _heads(vC, NA_HEADS)
    qcol = np.arange(GRID_W).reshape(ncb, NA_QCOLS)
    kstart = np.clip(qcol[:, 0] - NA_WIN_W // 2, 0, GRID_W - NA_KCOLS)
    kcol = kstart[:, None] + np.arange(NA_KCOLS)
    cstart = np.clip(qcol - NA_WIN_W // 2, 0, GRID_W - NA_WIN_W)
    kc_b = kcol[:, None, :]
    col_in = (kc_b >= cstart[..., None]) & (kc_b < cstart[..., None] + NA_WIN_W)
    col_idx = np.clip(kc_b - qcol[:, :, None] + NA_WIN_W - 1, 0, 2 * NA_WIN_W - 2)
    loc_mask = np.broadcast_to(col_in[:, :, None, :], (ncb, NA_QCOLS, wh, NA_KCOLS)).reshape(
        ncb, NA_QCOLS, n_loc)
    q_rows = q.reshape(b, NA_HEADS, rows, GRID_W, NA_DH).transpose(2, 0, 1, 3, 4)

    def row_block(inp):
        r, q_r = inp
        rs = jnp.clip(r - NA_WIN_H // 2, 0, rows - wh)
        k_rows = lax.dynamic_slice_in_dim(k_grid, rs, wh, axis=2)
        v_rows = lax.dynamic_slice_in_dim(v_grid, rs, wh, axis=2)
        k_blk = k_rows[:, :, :, kcol].transpose(0, 1, 3, 2, 4, 5).reshape(b, NA_HEADS, ncb, n_loc, NA_DH)
        v_blk = v_rows[:, :, :, kcol].transpose(0, 1, 3, 2, 4, 5).reshape(b, NA_HEADS, ncb, n_loc, NA_DH)
        q_blk = q_r.reshape(b, NA_HEADS, ncb, NA_QCOLS, NA_DH)
        row_idx = rs + jnp.arange(wh) - r + (NA_WIN_H - 1)
        bias = rpb[:, row_idx][:, :, col_idx]
        bias = bias.transpose(0, 2, 3, 1, 4).reshape(NA_HEADS, ncb, NA_QCOLS, n_loc).astype(jnp.float32)
        s_loc = jnp.einsum('bhnqd,bhnkd->bhnqk', q_blk, k_blk).astype(jnp.float32) + bias
        s_loc = jnp.where(loc_mask, s_loc, NEG_BIG)
        s_ctx = jnp.einsum('bhnqd,bhkd->bhnqk', q_blk, kc).astype(jnp.float32)
        p = jax.nn.softmax(jnp.concatenate([s_loc, s_ctx], -1), axis=-1).astype(v_blk.dtype)
        o = (jnp.einsum('bhnqk,bhnkd->bhnqd', p[..., :n_loc], v_blk)
             + jnp.einsum('bhnqk,bhkd->bhnqd', p[..., n_loc:], vc))
        return o.reshape(b, NA_HEADS, GRID_W, NA_DH)

    o = lax.map(row_block, (jnp.arange(rows, dtype=jnp.int32), q_rows))
    out_l = _merge_heads(o.transpose(1, 2, 0, 3, 4).reshape(b, NA_HEADS, l, NA_DH))
    out_c = None
    if need_ctx:
        qch = _heads(qC, NA_HEADS) * (NA_DH ** -0.5)
        p = jax.nn.softmax(jnp.einsum('bhqd,bhkd->bhqk', qch, kc).astype(jnp.float32), axis=-1)
        out_c = _merge_heads(jnp.einsum('bhqk,bhkd->bhqd', p.astype(vc.dtype), vc))
    return out_l, out_c


def _merge(branches, gate_logits, w_branch, w_o):
    b, l = gate_logits.shape[:2]
    proj = jnp.einsum('blne,ned->blnd', branches, w_branch)
    gates = jax.nn.sigmoid(gate_logits.reshape(b, l, N_BRANCHES, -1))
    return jnp.sum(gates * proj, axis=2) @ w_o


def _mixer_sublayer(hL, hC, w_in, w_decay, b_decay, gla_g, lam_q, lam_k, diff_g, lam_init, rpb,
                    w_branch, w_o, need_ctx):
    splits = [int(i) for i in np.cumsum(IN_SPLITS)[:-1]]
    pL = jnp.split(hL @ w_in, splits, axis=-1)
    pC = jnp.split(hC @ w_in, splits, axis=-1)
    gla_l, gla_c = _gla_mixer(pL[0:5], pC[0:5], w_decay, b_decay, gla_g, need_ctx)
    diff_l, diff_c = _diff_mixer(pL[5], pL[6], pL[7], pC[5], pC[6], pC[7], lam_q, lam_k, diff_g,
                                 lam_init, need_ctx)
    na_l, na_c = _na_mixer(pL[8], pL[9], pL[10], pC[8], pC[9], pC[10], rpb, need_ctx)
    y_l = _merge(jnp.stack([gla_l, diff_l, na_l], axis=2), pL[11], w_branch, w_o)
    y_c = _merge(jnp.stack([gla_c, diff_c, na_c], axis=2), pC[11], w_branch, w_o) if need_ctx else None
    return y_l, y_c


def _moe(h, w_router, b_router, w_gate, w_up, w_down):
    b, t, _ = h.shape
    aff = jax.nn.sigmoid(jnp.einsum('btd,de->bte', h, w_router).astype(jnp.float32))
    sel = aff + b_router.astype(jnp.float32)
    group_score = jnp.sum(lax.top_k(sel.reshape(b, t, N_GROUPS, EXPERTS_PER_GROUP), TOP_K)[0], -1)
    best_group = jnp.argmax(group_score, axis=-1)
    in_group = (jnp.arange(N_EXPERTS) // EXPERTS_PER_GROUP) == best_group[..., None]
    _, top_idx = lax.top_k(jnp.where(in_group, sel, -jnp.inf), TOP_K)
    top_aff = jnp.take_along_axis(aff, top_idx, axis=-1)
    top_w = top_aff / jnp.sum(top_aff, -1, keepdims=True)
    combine = jnp.sum(jax.nn.one_hot(top_idx, N_EXPERTS, dtype=jnp.float32) * top_w[..., None], -2)
    combine = combine.astype(h.dtype)
    out = jnp.zeros_like(h)
    for e in range(N_EXPERTS):
        a = jax.nn.silu(h @ w_gate[e]) * (h @ w_up[e])
        out = out + combine[..., e:e + 1] * (a @ w_down[e])
    return out


def setup_inputs(seed: int = 0) -> dict:
    key = jax.random.key(seed)
    ks = jax.random.split(key, 24)
    f32 = jnp.float32
    d = D_MODEL

    def nrm(k, shape, s):
        return jax.random.normal(k, shape, f32) * s

    return {
        "x": nrm(ks[0], (BATCH, SEQ, d), 1.0),
        "c": nrm(ks[1], (BATCH, d), 1.0),
        "ctx": nrm(ks[2], (BATCH, CTX_LEN, d), 1.0),
        "c_ctx": nrm(ks[3], (d,), 1.0),
        "w_ada": nrm(ks[4], (DEPTH, d, 6 * d), 0.5 * d ** -0.5),
        "b_ada": nrm(ks[5], (DEPTH, 6 * d), 0.02),
        "w_in": nrm(ks[6], (DEPTH, d, D_IN), d ** -0.5),
        "gla_w_decay": nrm(ks[7], (DEPTH, 2, GLA_RANK, GLA_HEADS * GLA_DK), GLA_RANK ** -0.5),
        "gla_b_decay": nrm(ks[8], (DEPTH, 2, GLA_HEADS * GLA_DK), 0.1),
        "gla_norm_g": 1.0 + nrm(ks[9], (DEPTH, GLA_DV), 0.02),
        "diff_lam_q": nrm(ks[10], (DEPTH, 2, DIFF_DH), 0.1),
        "diff_lam_k": nrm(ks[11], (DEPTH, 2, DIFF_DH), 0.1),
        "diff_norm_g": 1.0 + nrm(ks[12], (DEPTH, DIFF_DV), 0.02),
        "na_rpb": nrm(ks[13], (DEPTH, NA_HEADS, 2 * NA_WIN_H - 1, 2 * NA_WIN_W - 1), 0.1),
        "w_branch": nrm(ks[14], (DEPTH, N_BRANCHES, BRANCH_W, d), BRANCH_W ** -0.5),
        "w_o": nrm(ks[15], (DEPTH, d, d), DEEPNORM_BETA * d ** -0.5),
        "ln_g": 1.0 + nrm(ks[16], (DEPTH, 2, d), 0.02),
        "ln_b": nrm(ks[17], (DEPTH, 2, d), 0.02),
        "w_router": nrm(ks[18], (d, N_EXPERTS), d ** -0.5),
        "b_router": nrm(ks[19], (N_EXPERTS,), 0.01),
        "w_exp_gate": nrm(ks[20], (DEPTH, N_EXPERTS, d, D_EXPERT), d ** -0.5),
        "w_exp_up": nrm(ks[21], (DEPTH, N_EXPERTS, d, D_EXPERT), d ** -0.5),
        "w_exp_down": nrm(ks[22], (DEPTH, N_EXPERTS, D_EXPERT, d), DEEPNORM_BETA * D_EXPERT ** -0.5),
    }


def reference(x, c, ctx, c_ctx, w_ada, b_ada, w_in, gla_w_decay, gla_b_decay, gla_norm_g,
              diff_lam_q, diff_lam_k, diff_norm_g, na_rpb, w_branch, w_o, ln_g, ln_b,
              w_router, b_router, w_exp_gate, w_exp_up, w_exp_down):
    b, l, d = x.shape
    lc = ctx.shape[1]
    xL, xC = x, ctx
    for layer in range(DEPTH):
        need_ctx = layer < DEPTH - 1
        lam_init = 0.8 - 0.6 * math.exp(-0.3 * layer)
        mL = (jax.nn.silu(c) @ w_ada[layer] + b_ada[layer]).reshape(b, 6, 1, d)
        mC = (jax.nn.silu(c_ctx) @ w_ada[layer] + b_ada[layer]).reshape(6, d)
        hL = xL * (1.0 + mL[:, 1]) + mL[:, 0]
        hC = xC * (1.0 + mC[1]) + mC[0]
        yL, yC = _mixer_sublayer(hL, hC, w_in[layer], gla_w_decay[layer], gla_b_decay[layer],
                                 gla_norm_g[layer], diff_lam_q[layer], diff_lam_k[layer],
                                 diff_norm_g[layer], lam_init, na_rpb[layer], w_branch[layer],
                                 w_o[layer], need_ctx)
        xL = _layernorm(DEEPNORM_ALPHA * xL + mL[:, 2] * yL, ln_g[layer, 0], ln_b[layer, 0])
        hL = xL * (1.0 + mL[:, 4]) + mL[:, 3]
        if need_ctx:
            xC = _layernorm(DEEPNORM_ALPHA * xC + mC[2] * yC, ln_g[layer, 0], ln_b[layer, 0])
            hC = xC * (1.0 + mC[4]) + mC[3]
            y = _moe(jnp.concatenate([hC, hL], axis=1), w_router, b_router,
                     w_exp_gate[layer], w_exp_up[layer], w_exp_down[layer])
            yC, yL = y[:, :lc], y[:, lc:]
            xC = _layernorm(DEEPNORM_ALPHA * xC + mC[5] * yC, ln_g[layer, 1], ln_b[layer, 1])
        else:
            yL = _moe(hL, w_router, b_router, w_exp_gate[layer], w_exp_up[layer], w_exp_down[layer])
        xL = _layernorm(DEEPNORM_ALPHA * xL + mL[:, 5] * yL, ln_g[layer, 1], ln_b[layer, 1])
    return xL
```

```python
import functools
import math

import jax
import jax.numpy as jnp
import numpy as np
from jax import lax
from jax.experimental import pallas as pl
from jax.experimental.pallas import tpu as pltpu

F32 = jnp.float32
BF16 = jnp.bfloat16
HIGHEST = lax.Precision.HIGHEST

D_MODEL = 1024
DEPTH = 4
GRID_W = 64
CTX_LEN = 256
BRANCH_W = D_MODEL // 2
GLA_HEADS = 4
GLA_DV = 128
GLA_DK = 64
GLA_RANK = 16
GLA_TAU = 16.0
GLA_CHUNK = 64
DIFF_HEADS = 4
DIFF_DV = 128
DIFF_DH = 64
NA_HEADS = 8
NA_DH = 64
NA_WIN_H = 8
NA_WIN_W = 16
N_EXPERTS = 16
N_GROUPS = 4
EXPERTS_PER_GROUP = 4
D_EXPERT = D_MODEL // 2
ROPE_BASE = 10000.0
LN_EPS = 1e-5
RMS_EPS = 1e-6
NEG_BIG = -1e30
DEEPNORM_ALPHA = (2 * DEPTH) ** 0.25
LOG2E = 1.4426950408889634

LANES = 128
TOK_TILE = 256
NA_TILE = GRID_W
MOE_TILE = 256
N_PAIRS = 6
N_SEG = N_GROUPS * N_PAIRS
SEG_ROWS = 32
PAIR_LO = (0, 0, 0, 1, 1, 2)
PAIR_HI = (1, 2, 3, 2, 3, 3)
VMEM_LIMIT = 56 * 1024 * 1024

W_GQ, W_GK, W_GV, W_GG, W_GR = 0, 256, 512, 1024, 1536
W_DQ, W_DK, W_DV = 1664, 2176, 2688
W_NQ, W_NK, W_NV = 3200, 3712, 4224
W_SG, W_END = 4736, 7808


def _nt(a, b):
    return lax.dot_general(a, b, (((1,), (1,)), ((), ())), preferred_element_type=F32)


def _tn(a, b):
    return lax.dot_general(a, b, (((0,), (0,)), ((), ())), preferred_element_type=F32)


def _mm(a, b):
    return jnp.dot(a, b, preferred_element_type=F32)


def _sigmoid(x):
    return 1.0 / (1.0 + jnp.exp(-x))


def _params(*sem):
    return pltpu.CompilerParams(dimension_semantics=sem, vmem_limit_bytes=VMEM_LIMIT)


def _ada_kernel(c_ref, w_ref, b_ref, o_ref):
    cs = c_ref[...]
    s = cs * _sigmoid(cs)
    o_ref[0] = jnp.dot(s, w_ref[0], precision=HIGHEST, preferred_element_type=F32) + b_ref[0]


def _ada(cs, w_ada, b_ada):
    depth, d, n = w_ada.shape
    bn = 1536
    return pl.pallas_call(
        _ada_kernel,
        grid=(depth, n // bn),
        in_specs=[
            pl.BlockSpec((cs.shape[0], d), lambda l, j: (0, 0)),
            pl.BlockSpec((1, d, bn), lambda l, j: (l, 0, j)),
            pl.BlockSpec((1, 1, bn), lambda l, j: (l, 0, j)),
        ],
        out_specs=pl.BlockSpec((1, cs.shape[0], bn), lambda l, j: (l, 0, j)),
        out_shape=jax.ShapeDtypeStruct((depth, cs.shape[0], n), F32),
        compiler_params=_params("arbitrary", "arbitrary"),
        name="ada",
    )(cs, w_ada, b_ada.reshape(depth, 1, n))


def _proj_kernel(x_ref, mod_ref, w_ref, cos_ref, sin_ref,
                 gq, gk, gr, gv, gsg, dq, dk, dv, nq, nk, nv, sg):
    x = x_ref[0]
    mod = mod_ref[0]
    h = (x * (1.0 + mod[1:2]) + mod[0:1]).astype(BF16)

    def mm(lo, hi):
        return _mm(h, w_ref[:, lo:hi])

    gq[0] = mm(W_GQ, W_GK) * (GLA_DK ** -0.5)
    gk[0] = mm(W_GK, W_GV)
    gv[0] = mm(W_GV, W_GG).astype(BF16)
    g = mm(W_GG, W_GR)
    gsg[0] = (g * _sigmoid(g)).astype(BF16)
    gr[0] = mm(W_GR, W_DQ)

    cos = cos_ref[...]
    sin = sin_ref[...]

    def rope(y, scale):
        parts = []
        for i in range(DIFF_HEADS):
            p = y[:, i * LANES:(i + 1) * LANES]
            parts.append(((p * cos + pltpu.roll(p, LANES // 2, 1) * sin) * scale).astype(BF16))
        return jnp.concatenate(parts, axis=1)

    dq[0] = rope(mm(W_DQ, W_DK), (DIFF_DH ** -0.5) * LOG2E)
    dk[0] = rope(mm(W_DK, W_DV), 1.0)
    v = mm(W_DV, W_NQ).astype(BF16)
    lane = lax.broadcasted_iota(jnp.int32, (v.shape[0], LANES), 1)
    ones_col = jnp.where(lane == 0, 1.0, 0.0).astype(BF16)
    for i in range(DIFF_HEADS):
        dv[0, :, 2 * i * LANES:(2 * i + 1) * LANES] = v[:, i * LANES:(i + 1) * LANES]
        dv[0, :, (2 * i + 1) * LANES:(2 * i + 2) * LANES] = ones_col
    nq[0] = (mm(W_NQ, W_NK) * ((NA_DH ** -0.5) * LOG2E)).astype(BF16)
    nk[0] = mm(W_NK, W_NV).astype(BF16)
    nv[0] = mm(W_NV, W_SG).astype(BF16)
    for i in range(3):
        lo = W_SG + i * D_MODEL
        sg[0, :, i * D_MODEL:(i + 1) * D_MODEL] = _sigmoid(mm(lo, lo + D_MODEL)).astype(BF16)


def _proj(xs, mod2, w, cos_t, sin_t):
    b, ntok, d = xs.shape
    nt = ntok // TOK_TILE
    widths = (256, 256, 128, 512, 512, 512, 512, 1024, 512, 512, 512, 3072)
    dtypes = (F32, F32, F32, BF16, BF16, BF16, BF16, BF16, BF16, BF16, BF16, BF16)
    tok = lambda bi, ti: (bi, ti, 0)
    return pl.pallas_call(
        _proj_kernel,
        grid=(b, nt),
        in_specs=[
            pl.BlockSpec((1, TOK_TILE, d), tok),
            pl.BlockSpec((1, 8, d), lambda bi, ti: (2 * bi + jnp.minimum(ti, 1), 0, 0)),
            pl.BlockSpec((d, W_END), lambda bi, ti: (0, 0), pipeline_mode=pl.Buffered(1)),
            pl.BlockSpec((TOK_TILE, LANES), lambda bi, ti: (ti, 0)),
            pl.BlockSpec((TOK_TILE, LANES), lambda bi, ti: (ti, 0)),
        ],
        out_specs=[pl.BlockSpec((1, TOK_TILE, wd), tok) for wd in widths],
        out_shape=[jax.ShapeDtypeStruct((b, ntok, wd), dt) for wd, dt in zip(widths, dtypes)],
        compiler_params=_params("arbitrary", "arbitrary"),
        name="proj",
    )(xs, mod2, w, cos_t, sin_t)


def _log_sigmoid(z):
    return -(jnp.maximum(-z, 0.0) + jnp.log(1.0 + jnp.exp(-jnp.abs(z))))


def _gla_kernel(qf, kf, vf, rf, qb, kb, vb, rb, wd_ref, bd_ref, of_ref, ob_ref, st_ref):
    n = pl.program_id(1)

    @pl.when(n == 0)
    def _():
        st_ref[...] = jnp.zeros_like(st_ref)

    c = GLA_CHUNK
    row = lax.broadcasted_iota(jnp.int32, (c, c), 0)
    col = lax.broadcasted_iota(jnp.int32, (c, c), 1)
    lane = lax.broadcasted_iota(jnp.int32, (1, LANES), 1)
    head_mask = (lane < GLA_DK, lane >= GLA_DK)

    for d, (q_ref, k_ref, v_ref, r_ref, o_ref) in enumerate(
            ((qf, kf, vf, rf, of_ref), (qb, kb, vb, rb, ob_ref))):
        causal = (row >= col) if d == 0 else (row <= col)
        tri = jnp.where(causal, 1.0, 0.0).astype(F32)
        z = _mm(r_ref[0].astype(BF16), wd_ref[d]) + bd_ref[d]
        logd = _log_sigmoid(z) * (1.0 / GLA_TAU)
        cum = jnp.dot(tri, logd, precision=HIGHEST, preferred_element_type=F32)
        last = cum[c - 1:c] if d == 0 else cum[0:1]
        k = k_ref[0]
        q_in = q_ref[0] * jnp.exp(cum)
        k_in = (k * jnp.exp(-cum)).astype(BF16)
        k_st = k * jnp.exp(last - cum)
        dec = jnp.exp(last)
        v = v_ref[0]
        for p in range(GLA_HEADS // 2):
            sl = slice(p * LANES, (p + 1) * LANES)
            q2 = q_in[:, sl]
            lhs = jnp.concatenate([jnp.where(head_mask[0], q2, 0.0),
                                   jnp.where(head_mask[1], q2, 0.0)], axis=0).astype(BF16)
            att = _nt(lhs, k_in[:, sl])
            q2b = q2.astype(BF16)
            for hh in range(2):
                h = 2 * p + hh
                a = jnp.where(causal, att[hh * c:(hh + 1) * c], 0.0).astype(BF16)
                vh = v[:, h * GLA_DV:(h + 1) * GLA_DV]
                st = st_ref[d, h]
                o_ref[0, :, h * GLA_DV:(h + 1) * GLA_DV] = _mm(a, vh) + _nt(q2b, st.astype(BF16))
                kh = jnp.where(head_mask[hh], k_st[:, sl], 0.0).astype(BF16)
                st_ref[d, h] = st * dec[:, sl] + _tn(vh, kh)


def _gla(gq, gk, gv, gr, wd, bd):
    b, ntok, _ = gq.shape
    nch = ntok // GLA_CHUNK
    nctx = CTX_LEN // GLA_CHUNK

    def fwd(bi, n):
        return (bi, n, 0)

    def bwd(bi, n):
        return (bi, jnp.where(n < nctx, nctx - 1 - n, nch + nctx - 1 - n), 0)

    def specs(im):
        return [pl.BlockSpec((1, GLA_CHUNK, 256), im), pl.BlockSpec((1, GLA_CHUNK, 256), im),
                pl.BlockSpec((1, GLA_CHUNK, 512), im), pl.BlockSpec((1, GLA_CHUNK, 128), im)]

    return pl.pallas_call(
        _gla_kernel,
        grid=(b, nch),
        in_specs=specs(fwd) + specs(bwd) + [
            pl.BlockSpec((2, LANES, 256), lambda bi, n: (0, 0, 0)),
            pl.BlockSpec((2, 1, 256), lambda bi, n: (0, 0, 0)),
        ],
        out_specs=[pl.BlockSpec((1, GLA_CHUNK, 512), fwd), pl.BlockSpec((1, GLA_CHUNK, 512), bwd)],
        out_shape=[jax.ShapeDtypeStruct((b, ntok, 512), F32)] * 2,
        scratch_shapes=[pltpu.VMEM((2, GLA_HEADS, GLA_DV, LANES), F32)],
        compiler_params=_params("arbitrary", "arbitrary"),
        name="gla",
    )(gq, gk, gv, gr, gq, gk, gv, gr, wd, bd)


DIFF_KC = 256


def _diff_kernel(lam_init, q_ref, k_ref, v_ref, lq_ref, lk_ref, g_ref, o_ref, s_scr, acc_scr):
    t = pl.program_id(2)
    tq = q_ref.shape[1]
    nkc = k_ref.shape[1] // DIFF_KC
    lane = lax.broadcasted_iota(jnp.int32, (1, LANES), 1)
    comp0 = (lane // (DIFF_DH // 2)) % 2 == 0
    q = q_ref[0]
    zero = jnp.zeros_like(q)
    lhs = jnp.concatenate([jnp.where(comp0, q, zero), jnp.where(comp0, zero, q)], axis=0)
    nch = jnp.where(t == 0, 1, nkc)

    def scores(ci, m):
        off = pl.multiple_of(ci * DIFF_KC, DIFF_KC)
        s = _nt(lhs, k_ref[0, pl.ds(off, DIFF_KC), :])
        s_scr[ci] = s
        return jnp.maximum(m, jnp.maximum(s[:, :LANES], s[:, LANES:]))

    m = lax.fori_loop(0, nch, scores, jnp.full((2 * tq, LANES), -jnp.inf, F32))
    mrow = jnp.max(m, axis=1, keepdims=True)
    acc_scr[...] = jnp.zeros_like(acc_scr)

    def values(ci, carry):
        off = pl.multiple_of(ci * DIFF_KC, DIFF_KC)
        p = jnp.exp2(s_scr[ci] - mrow).astype(BF16)
        acc_scr[...] += _mm(p, v_ref[0, pl.ds(off, DIFF_KC), :])
        return carry

    lax.fori_loop(0, nch, values, 0)
    acc = acc_scr[...]
    o1 = acc[:tq, :DIFF_DV] / acc[:tq, DIFF_DV:DIFF_DV + 1]
    o2 = acc[tq:, :DIFF_DV] / acc[tq:, DIFF_DV:DIFF_DV + 1]
    lql = lq_ref[...] * lk_ref[...]
    lam = (jnp.exp(jnp.sum(lql[0:1], axis=1, keepdims=True))
           - jnp.exp(jnp.sum(lql[1:2], axis=1, keepdims=True)) + lam_init)
    o = o1 - lam * o2
    o = o * lax.rsqrt(jnp.mean(o * o, axis=1, keepdims=True) + RMS_EPS) * g_ref[...] * (1.0 - lam_init)
    o_ref[0] = o.astype(BF16)


def _diff(dq, dk, dv, lam_q, lam_k, norm_g, lam_init):
    b, ntok, _ = dq.shape
    nt = ntok // TOK_TILE
    return pl.pallas_call(
        functools.partial(_diff_kernel, lam_init),
        grid=(b, DIFF_HEADS, nt),
        in_specs=[
            pl.BlockSpec((1, TOK_TILE, LANES), lambda bi, h, t: (bi, t, h)),
            pl.BlockSpec((1, ntok, LANES), lambda bi, h, t: (bi, 0, h)),
            pl.BlockSpec((1, ntok, 2 * LANES), lambda bi, h, t: (bi, 0, h)),
            pl.BlockSpec((2, DIFF_DH), lambda bi, h, t: (0, 0)),
            pl.BlockSpec((2, DIFF_DH), lambda bi, h, t: (0, 0)),
            pl.BlockSpec((1, DIFF_DV), lambda bi, h, t: (0, 0)),
        ],
        out_specs=pl.BlockSpec((1, TOK_TILE, LANES), lambda bi, h, t: (bi, t, h)),
        out_shape=jax.ShapeDtypeStruct((b, ntok, DIFF_HEADS * DIFF_DV), BF16),
        scratch_shapes=[pltpu.VMEM((ntok // DIFF_KC, 2 * TOK_TILE, DIFF_KC), F32),
                        pltpu.VMEM((2 * TOK_TILE, 2 * LANES), F32)],
        compiler_params=_params("arbitrary", "arbitrary", "arbitrary"),
        name="diff",
    )(dq, dk, dv, lam_q, lam_k, norm_g.reshape(1, DIFF_DV))


def _na_kernel(rows, q_ref, k_ref, v_ref, b_ref, o_ref):
    t = pl.program_id(1)
    nctx = CTX_LEN // NA_TILE
    r = t - nctx
    rs = jnp.clip(r - NA_WIN_H // 2, 0, rows - NA_WIN_H)
    start = pl.multiple_of(CTX_LEN + rs * GRID_W, GRID_W)
    nloc = NA_WIN_H * GRID_W
    kwin = k_ref[0, pl.ds(start, nloc), :]
    vwin = v_ref[0, pl.ds(start, nloc), :]
    kctx = k_ref[0, 0:CTX_LEN, :]
    vctx = v_ref[0, 0:CTX_LEN, :]
    q = q_ref[0]
    lane = lax.broadcasted_iota(jnp.int32, (1, LANES), 1)
    first = lane < NA_DH
    for p in range(NA_HEADS // 2):
        sl = slice(p * LANES, (p + 1) * LANES)
        q2 = q[:, sl]
        zero = jnp.zeros_like(q2)
        lhs = jnp.concatenate([jnp.where(first, q2, zero), jnp.where(first, zero, q2)], axis=0)
        s_loc = _nt(lhs, kwin[:, sl]) + b_ref[p]
        s_ctx = _nt(lhs, kctx[:, sl])
        m = jnp.maximum(jnp.max(s_loc, axis=1, keepdims=True), jnp.max(s_ctx, axis=1, keepdims=True))
        p_loc = jnp.exp2(s_loc - m)
        p_ctx = jnp.exp2(s_ctx - m)
        l = jnp.sum(p_loc, axis=1, keepdims=True) + jnp.sum(p_ctx, axis=1, keepdims=True)
        o = (_mm(p_loc.astype(BF16), vwin[:, sl]) + _mm(p_ctx.astype(BF16), vctx[:, sl])) / l
        o_ref[0, :, sl] = jnp.where(first, o[:NA_TILE], o[NA_TILE:]).astype(BF16)


def _na(nq, nk, nv, bias):
    b, ntok, w = nq.shape
    nt = ntok // NA_TILE
    nctx = CTX_LEN // NA_TILE
    rows = (ntok - CTX_LEN) // GRID_W

    def cfg(bi, t):
        r = t - nctx
        return (jnp.where(t < nctx, NA_WIN_H, r - jnp.clip(r - NA_WIN_H // 2, 0, rows - NA_WIN_H)), 0, 0)

    return pl.pallas_call(
        functools.partial(_na_kernel, rows),
        grid=(b, nt),
        in_specs=[
            pl.BlockSpec((1, NA_TILE, w), lambda bi, t: (bi, t, 0)),
            pl.BlockSpec((1, ntok, w), lambda bi, t: (bi, 0, 0)),
            pl.BlockSpec((1, ntok, w), lambda bi, t: (bi, 0, 0)),
            pl.BlockSpec((NA_HEADS // 2, 2 * NA_TILE, NA_WIN_H * GRID_W), cfg),
        ],
        out_specs=pl.BlockSpec((1, NA_TILE, w), lambda bi, t: (bi, t, 0)),
        out_shape=jax.ShapeDtypeStruct((b, ntok, w), BF16),
        compiler_params=_params("arbitrary", "arbitrary"),
        name="na",
    )(nq, nk, nv, bias)


def _na_bias(rpb):
    qc = np.arange(GRID_W)[:, None]
    kc = np.arange(GRID_W)[None, :]
    col_idx = np.clip(kc - qc + NA_WIN_W - 1, 0, 2 * NA_WIN_W - 2)
    cstart = np.clip(qc - NA_WIN_W // 2, 0, GRID_W - NA_WIN_W)
    win = (kc >= cstart) & (kc < cstart + NA_WIN_W)
    row_idx = np.arange(NA_WIN_H)[None, :] - np.arange(NA_WIN_H)[:, None] + (NA_WIN_H - 1)
    tbl = rpb.astype(F32)[:, row_idx][..., col_idx]
    tbl = jnp.where(win[None, None, None], tbl * LOG2E, NEG_BIG)
    tbl = tbl.transpose(1, 0, 3, 2, 4)
    tbl = tbl.reshape(NA_WIN_H, NA_HEADS // 2, 2 * GRID_W, NA_WIN_H * GRID_W)
    masked = jnp.full((1,) + tbl.shape[1:], NEG_BIG, F32)
    return jnp.concatenate([tbl, masked], axis=0).reshape(-1, 2 * GRID_W, NA_WIN_H * GRID_W)


def _layernorm(x, g, b):
    mu = jnp.mean(x, axis=1, keepdims=True)
    xc = x - mu
    var = jnp.mean(xc * xc, axis=1, keepdims=True)
    return xc * lax.rsqrt(var + LN_EPS) * g + b


def _merge_kernel(x_ref, of_ref, ob_ref, gsg_ref, do_ref, no_ref, sg_ref, mod_ref, gg_ref,
                  wb_ref, wo_ref, lng_ref, lnb_ref, wr_ref, br_ref,
                  x1_ref, h2_ref, route_ref, cnt_ref, run_ref):
    first_step = jnp.logical_and(pl.program_id(0) == 0, pl.program_id(1) == 0)

    @pl.when(first_step)
    def _():
        run_ref[...] = jnp.zeros_like(run_ref)

    tm = x_ref.shape[1]
    mod = mod_ref[0]
    o = of_ref[0] + ob_ref[0]
    gg = gg_ref[...]
    parts = []
    for h in range(GLA_HEADS):
        oh = o[:, h * GLA_DV:(h + 1) * GLA_DV]
        parts.append(oh * lax.rsqrt(jnp.mean(oh * oh, axis=1, keepdims=True) + RMS_EPS) * gg)
    gla = (jnp.concatenate(parts, axis=1) * gsg_ref[0].astype(F32)).astype(BF16)
    branches = (gla, do_ref[0], no_ref[0])
    y = jnp.zeros((tm, D_MODEL), F32)
    for i in range(3):
        y = y + sg_ref[0, :, i * D_MODEL:(i + 1) * D_MODEL].astype(F32) * _mm(branches[i], wb_ref[i])
    y = _mm(y.astype(BF16), wo_ref[...])
    x1 = _layernorm(DEEPNORM_ALPHA * x_ref[0] + mod[2:3] * y, lng_ref[...], lnb_ref[...])
    x1_ref[0] = x1
    h2 = x1 * (1.0 + mod[4:5]) + mod[3:4]
    h2_ref[0] = h2

    logits = lax.dot_general(wr_ref[...], h2, (((1,), (1,)), ((), ())),
                             precision=HIGHEST, preferred_element_type=F32)
    aff = _sigmoid(logits)
    sel = aff + br_ref[...]
    srow = [sel[e:e + 1] for e in range(N_EXPERTS)]
    arow = [aff[e:e + 1] for e in range(N_EXPERTS)]
    gscore = []
    for g in range(N_GROUPS):
        a0, a1, a2, a3 = srow[4 * g:4 * g + 4]
        hi01, lo01 = jnp.maximum(a0, a1), jnp.minimum(a0, a1)
        hi23, lo23 = jnp.maximum(a2, a3), jnp.minimum(a2, a3)
        top1 = jnp.maximum(hi01, hi23)
        top2 = jnp.maximum(jnp.minimum(hi01, hi23), jnp.maximum(lo01, lo23))
        gscore.append(top1 + top2)
    best = jnp.zeros_like(gscore[0])
    bestv = gscore[0]
    for g in range(1, N_GROUPS):
        better = gscore[g] > bestv
        best = jnp.where(better, float(g), best)
        bestv = jnp.where(better, gscore[g], bestv)

    def pick(rows_, i):
        out = rows_[i]
        for g in range(1, N_GROUPS):
            out = jnp.where(best == float(g), rows_[4 * g + i], out)
        return out

    s4 = [pick(srow, i) for i in range(4)]
    f4 = [pick(arow, i) for i in range(4)]
    chosen = []
    for i in range(4):
        rank = jnp.zeros_like(best)
        for j in range(4):
            if j == i:
                continue
            ahead = (s4[j] > s4[i]) | ((s4[j] == s4[i]) & (j < i))
            rank = rank + jnp.where(ahead, 1.0, 0.0)
        chosen.append(rank < 2.0)
    c0, c1, c2, c3 = chosen
    pidx = jnp.where(c0, jnp.where(c1, 0.0, jnp.where(c2, 1.0, 2.0)),
                     jnp.where(c1, jnp.where(c2, 3.0, 4.0), 5.0))
    a_lo = jnp.where(c0, f4[0], jnp.where(c1, f4[1], f4[2]))
    a_hi = jnp.where(c3, f4[3], jnp.where(c2, f4[2], f4[1]))
    den = a_lo + a_hi
    seg = best * float(N_PAIRS) + pidx

    srows = lax.broadcasted_iota(jnp.int32, (SEG_ROWS, tm), 0).astype(F32)
    onehot = jnp.where(srows == seg, 1.0, 0.0)
    ii = lax.broadcasted_iota(jnp.int32, (tm, tm), 0)
    jj = lax.broadcasted_iota(jnp.int32, (tm, tm), 1)
    before = jnp.where(ii < jj, 1.0, 0.0).astype(BF16)
    prefix = _mm(onehot.astype(BF16), before) + run_ref[...][:, 0:1]
    rank = jnp.sum(onehot * prefix, axis=0, keepdims=True)
    run = run_ref[...] + jnp.sum(onehot, axis=1, keepdims=True)
    run_ref[...] = run
    cnt_ref[...] = run
    zrow = jnp.zeros_like(seg)
    route_ref[0, 0] = jnp.concatenate([seg, rank, a_lo / den, a_hi / den, zrow, zrow, zrow, zrow], axis=0)


def _merge(xs, o_f, o_b, gsg, d_o, n_o, sg, mod2, gla_g, wb, wo, ln_g, ln_b, wr_t, br):
    b, ntok, d = xs.shape
    nt = ntok // TOK_TILE
    tok = lambda bi, ti: (bi, ti, 0)
    const2 = lambda bi, ti: (0, 0)

    def tokspec(wd):
        return pl.BlockSpec((1, TOK_TILE, wd), tok)

    return pl.pallas_call(
        _merge_kernel,
        grid=(b, nt),
        in_specs=[
            tokspec(d), tokspec(512), tokspec(512), tokspec(512), tokspec(512), tokspec(512),
            tokspec(3 * d),
            pl.BlockSpec((1, 8, d), lambda bi, ti: (2 * bi + jnp.minimum(ti, 1), 0, 0)),
            pl.BlockSpec((1, GLA_DV), const2),
            pl.BlockSpec((3, BRANCH_W, d), lambda bi, ti: (0, 0, 0)),
            pl.BlockSpec((d, d), const2),
            pl.BlockSpec((1, d), const2),
            pl.BlockSpec((1, d), const2),
            pl.BlockSpec((N_EXPERTS, d), const2),
            pl.BlockSpec((N_EXPERTS, 1), const2),
        ],
        out_specs=[
            tokspec(d), tokspec(d),
            pl.BlockSpec((1, 1, 8, TOK_TILE), lambda bi, ti: (bi, ti, 0, 0)),
            pl.BlockSpec((SEG_ROWS, LANES), const2),
        ],
        out_shape=[
            jax.ShapeDtypeStruct((b, ntok, d), F32),
            jax.ShapeDtypeStruct((b, ntok, d), F32),
            jax.ShapeDtypeStruct((b, nt, 8, TOK_TILE), F32),
            jax.ShapeDtypeStruct((SEG_ROWS, LANES), F32),
        ],
        scratch_shapes=[pltpu.VMEM((SEG_ROWS, LANES), F32)],
        compiler_params=_params("arbitrary", "arbitrary"),
        name="merge",
    )(xs, o_f, o_b, gsg, d_o, n_o, sg, mod2, gla_g.reshape(1, GLA_DV), wb, wo,
      ln_g.reshape(1, d), ln_b.reshape(1, d), wr_t, br.reshape(N_EXPERTS, 1))


def _moe_kernel(src, meta, h_hbm, ws_ref, wg1, wu1, wd1, wg2, wu2, wd2, y_hbm,
                xbuf, ybuf, sem_in, sem_out):
    i = pl.program_id(0)
    ntile = pl.num_programs(0)
    n_used = meta[2 * ntile]
    slot = i % 2
    tm = MOE_TILE

    def n_real(tile):
        return meta[2 * ntile + 1 + tile]

    def gather(tile, sl):
        def body(j, carry):
            tok = src[tile * tm + j]
            pltpu.make_async_copy(h_hbm.at[pl.ds(tok, 1)], xbuf.at[sl, pl.ds(j, 1)], sem_in.at[sl]).start()
            return carry
        lax.fori_loop(0, tm, body, 0, unroll=8)

    def wait_in(sl):
        pltpu.make_async_copy(h_hbm.at[pl.ds(0, tm)], xbuf.at[sl], sem_in.at[sl]).wait()

    def row_out(tile, sl, j):
        tok = src[tile * tm + j]
        return pltpu.make_async_copy(ybuf.at[sl, pl.ds(j, 1)], y_hbm.at[pl.ds(tok, 1)], sem_out.at[sl])

    def scatter(tile, sl):
        nr = n_real(tile)

        def body(j, carry):
            row_out(tile, sl, j).start()
            return carry

        @pl.when(nr == tm)
        def _():
            lax.fori_loop(0, tm, body, 0, unroll=8)

        @pl.when(nr < tm)
        def _():
            lax.fori_loop(0, nr, body, 0)

    def wait_out(tile, sl):
        nr = n_real(tile)

        @pl.when(nr == tm)
        def _():
            pltpu.make_async_copy(ybuf.at[sl], y_hbm.at[pl.ds(0, tm)], sem_out.at[sl]).wait()

        @pl.when(nr < tm)
        def _():
            def body(j, carry):
                row_out(tile, sl, j).wait()
                return carry
            lax.fori_loop(0, nr, body, 0)

    @pl.when(i == 0)
    def _():
        gather(0, 0)

    @pl.when(i + 1 < n_used)
    def _():
        gather(i + 1, 1 - slot)

    @pl.when(i < n_used)
    def _():
        wait_in(slot)

        @pl.when(i >= 2)
        def _():
            wait_out(i - 2, slot)

        x = xbuf[slot].astype(BF16)

        def ffn(wg, wu, wd):
            a = _mm(x, wg[0])
            u = _mm(x, wu[0])
            return _mm((a * _sigmoid(a) * u).astype(BF16), wd[0])

        w = ws_ref[...]
        ybuf[slot] = w[:, 0:1] * ffn(wg1, wu1, wd1) + w[:, 1:2] * ffn(wg2, wu2, wd2)
        scatter(i, slot)

    @pl.when(i == ntile - 1)
    def _():
        wait_out(n_used - 1, (n_used - 1) % 2)

        @pl.when(n_used >= 2)
        def _():
            wait_out(n_used - 2, n_used % 2)


def _moe(h2, src, meta, wsort, wg, wu, wd):
    t, d = h2.shape
    tpad = src.shape[0]
    ntile = tpad // MOE_TILE
    lo = lambda i, s, m: (m[2 * i], 0, 0)
    hi = lambda i, s, m: (m[2 * i + 1], 0, 0)
    grid_spec = pltpu.PrefetchScalarGridSpec(
        num_scalar_prefetch=2,
        grid=(ntile,),
        in_specs=[
            pl.BlockSpec(memory_space=pl.ANY),
            pl.BlockSpec((MOE_TILE, 2), lambda i, s, m: (i, 0)),
            pl.BlockSpec((1, d, D_EXPERT), lo), pl.BlockSpec((1, d, D_EXPERT), lo),
            pl.BlockSpec((1, D_EXPERT, d), lo),
            pl.BlockSpec((1, d, D_EXPERT), hi), pl.BlockSpec((1, d, D_EXPERT), hi),
            pl.BlockSpec((1, D_EXPERT, d), hi),
        ],
        out_specs=pl.BlockSpec(memory_space=pl.ANY),
        scratch_shapes=[pltpu.VMEM((2, MOE_TILE, d), F32), pltpu.VMEM((2, MOE_TILE, d), F32),
                        pltpu.SemaphoreType.DMA((2,)), pltpu.SemaphoreType.DMA((2,))],
    )
    return pl.pallas_call(
        _moe_kernel,
        grid_spec=grid_spec,
        out_shape=jax.ShapeDtypeStruct((t, d), F32),
        compiler_params=_params("arbitrary"),
        name="moe",
    )(src, meta, h2, wsort, wg, wu, wd, wg, wu, wd)


def _dispatch_plan(route, counts, t):
    ntile = (t + N_SEG * (MOE_TILE - 1)) // MOE_TILE
    tpad = ntile * MOE_TILE
    seg = route[:, :, 0, :].reshape(t).astype(jnp.int32)
    rank = route[:, :, 1, :].reshape(t).astype(jnp.int32)
    wtok = jnp.stack([route[:, :, 2, :].reshape(t), route[:, :, 3, :].reshape(t)], axis=1)
    cnt = counts[:N_SEG, 0].astype(jnp.int32)
    seg_tiles = (cnt + MOE_TILE - 1) // MOE_TILE
    tile_end = jnp.cumsum(seg_tiles)
    start = (tile_end - seg_tiles) * MOE_TILE
    dest = start[seg] + rank
    tok_ids = jnp.arange(t, dtype=jnp.int32)
    src = jnp.zeros((tpad,), jnp.int32).at[dest].set(tok_ids)
    real = jnp.zeros((tpad,), jnp.bool_).at[dest].set(True)
    wsort = jnp.where(real[:, None], wtok[src], 0.0)
    n_used = tile_end[-1]
    tiles = jnp.arange(ntile, dtype=jnp.int32)
    tile_seg = jnp.searchsorted(tile_end, jnp.minimum(tiles, n_used - 1), side="right").astype(jnp.int32)
    tile_seg = jnp.minimum(tile_seg, N_SEG - 1)
    base = (tile_seg // N_PAIRS) * EXPERTS_PER_GROUP
    pair = tile_seg % N_PAIRS
    e_lo = base + jnp.asarray(PAIR_LO, jnp.int32)[pair]
    e_hi = base + jnp.asarray(PAIR_HI, jnp.int32)[pair]
    n_real = jnp.sum(real.reshape(ntile, MOE_TILE).astype(jnp.int32), axis=1)
    meta = jnp.concatenate([jnp.stack([e_lo, e_hi], axis=1).reshape(-1), n_used[None], n_real]).astype(jnp.int32)
    return src, meta, wsort


def _final_kernel(x_ref, y_ref, mod_ref, g_ref, b_ref, o_ref):
    mod = mod_ref[0]
    o_ref[0] = _layernorm(DEEPNORM_ALPHA * x_ref[0] + mod[5:6] * y_ref[...], g_ref[...], b_ref[...])


def _final(x1, y, mod2, ln_g, ln_b):
    b, ntok, d = x1.shape
    nt = ntok // TOK_TILE
    return pl.pallas_call(
        _final_kernel,
        grid=(b, nt),
        in_specs=[
            pl.BlockSpec((1, TOK_TILE, d), lambda bi, ti: (bi, ti, 0)),
            pl.BlockSpec((TOK_TILE, d), lambda bi, ti: (bi * nt + ti, 0)),
            pl.BlockSpec((1, 8, d), lambda bi, ti: (2 * bi + jnp.minimum(ti, 1), 0, 0)),
            pl.BlockSpec((1, d), lambda bi, ti: (0, 0)),
            pl.BlockSpec((1, d), lambda bi, ti: (0, 0)),
        ],
        out_specs=pl.BlockSpec((1, TOK_TILE, d), lambda bi, ti: (bi, ti, 0)),
        out_shape=jax.ShapeDtypeStruct((b, ntok, d), F32),
        compiler_params=_params("arbitrary", "arbitrary"),
        name="final_ln",
    )(x1, y, mod2, ln_g.reshape(1, d), ln_b.reshape(1, d))


def _diff_perm():
    j = np.arange(LANES)
    within = ((j // 32) % 2) * DIFF_DH + (j // 64) * (DIFF_DH // 2) + (j % 32)
    return np.concatenate([h * LANES + within for h in range(DIFF_HEADS)])


def _pack_w_in(w_in):
    splits = np.cumsum([256, 256, 512, 512, 32, 512, 512, 512, 512, 512, 512])
    gq, gk, gv, gg, gr, dq, dk, dv, nq, nk, nv, sg = jnp.split(w_in, splits, axis=-1)
    perm = _diff_perm()
    gr = jnp.pad(gr, ((0, 0), (0, 0), (0, LANES - 2 * GLA_RANK)))
    packed = jnp.concatenate([gq, gk, gv, gg, gr, dq[..., perm], dk[..., perm], dv, nq, nk, nv, sg], axis=-1)
    return packed.astype(BF16)


def _rope_tables(n_lat):
    t = jnp.arange(n_lat)
    row = (t // GRID_W).astype(F32)
    col = (t % GRID_W).astype(F32)
    n_freq = DIFF_DH // 4
    inv = ROPE_BASE ** (-jnp.arange(n_freq, dtype=F32) / n_freq)
    ang = jnp.concatenate([row[:, None] * inv, col[:, None] * inv], -1)
    cos, sin = jnp.cos(ang), jnp.sin(ang)
    cos_t = jnp.concatenate([cos] * 4, axis=1)
    sin_t = jnp.concatenate([-sin, -sin, sin, sin], axis=1)
    cos_t = jnp.concatenate([jnp.ones((CTX_LEN, LANES), F32), cos_t], axis=0)
    sin_t = jnp.concatenate([jnp.zeros((CTX_LEN, LANES), F32), sin_t], axis=0)
    return cos_t, sin_t


def _pack_decay(w_decay, b_decay):
    depth = w_decay.shape[0]
    wd = jnp.zeros((depth, 2, LANES, GLA_HEADS * GLA_DK), F32)
    for d in range(2):
        wd = wd.at[:, d, d * GLA_RANK:(d + 1) * GLA_RANK].set(w_decay[:, d])
    return wd.astype(BF16), b_decay.reshape(depth, 2, 1, GLA_HEADS * GLA_DK)


def kernel(x, c, ctx, c_ctx, w_ada, b_ada, w_in, gla_w_decay, gla_b_decay, gla_norm_g, diff_lam_q,
           diff_lam_k, diff_norm_g, na_rpb, w_branch, w_o, ln_g, ln_b, w_router, b_router,
           w_exp_gate, w_exp_up, w_exp_down):
    b, l, d = x.shape
    lc = ctx.shape[1]
    ntok = lc + l
    t = b * ntok

    w_in_p = _pack_w_in(w_in)
    wd_p, bd_p = _pack_decay(gla_w_decay, gla_b_decay)
    cos_t, sin_t = _rope_tables(l)
    wb = w_branch.astype(BF16)
    wo = w_o.astype(BF16)
    wg = w_exp_gate.astype(BF16)
    wu = w_exp_up.astype(BF16)
    wdn = w_exp_down.astype(BF16)
    wr_t = w_router.T

    cs = jnp.concatenate([c, c_ctx[None], jnp.zeros((16 - b - 1, d), F32)], axis=0)
    mods = _ada(cs, w_ada, b_ada).reshape(DEPTH, 16, 6, d)

    xs = jnp.concatenate([ctx, x], axis=1)
    for layer in range(DEPTH):
        lam_init = 0.8 - 0.6 * math.exp(-0.3 * layer)
        m_lat = mods[layer, :b]
        m_ctx = jnp.broadcast_to(mods[layer, b][None], (b, 6, d))
        mod2 = jnp.stack([m_ctx, m_lat], axis=1).reshape(2 * b, 6, d)
        mod2 = jnp.pad(mod2, ((0, 0), (0, 2), (0, 0)))

        gq, gk, gr, gv, gsg, dq, dk, dv, nq, nk, nv, sg = _proj(xs, mod2, w_in_p[layer], cos_t, sin_t)
        o_f, o_b = _gla(gq, gk, gv, gr, wd_p[layer], bd_p[layer])
        d_o = _diff(dq, dk, dv, diff_lam_q[layer], diff_lam_k[layer], diff_norm_g[layer], lam_init)
        n_o = _na(nq, nk, nv, _na_bias(na_rpb[layer]))
        x1, h2, route, counts = _merge(xs, o_f, o_b, gsg, d_o, n_o, sg, mod2, gla_norm_g[layer],
                                       wb[layer], wo[layer], ln_g[layer, 0], ln_b[layer, 0],
                                       wr_t, b_router)
        src, meta, wsort = _dispatch_plan(route, counts, t)
        y = _moe(h2.reshape(t, d), src, meta, wsort, wg[layer], wu[layer], wdn[layer])
        xs = _final(x1, y, mod2, ln_g[layer, 1], ln_b[layer, 1])
    return xs[:, lc:]
```

```python
import functools
import math

import jax
import jax.numpy as jnp
import numpy as np
from jax import lax
from jax.experimental import pallas as pl
from jax.experimental.pallas import tpu as pltpu

F32 = jnp.float32
BF16 = jnp.bfloat16
HIGHEST = lax.Precision.HIGHEST

D_MODEL = 1024
DEPTH = 4
GRID_W = 64
CTX_LEN = 256
BRANCH_W = D_MODEL // 2
GLA_HEADS = 4
GLA_DV = 128
GLA_DK = 64
GLA_RANK = 16
GLA_TAU = 16.0
GLA_CHUNK = 64
DIFF_HEADS = 4
DIFF_DV = 128
DIFF_DH = 64
DIFF_VROWS = DIFF_DV + 16
NA_HEADS = 8
NA_DH = 64
NA_WIN_H = 8
NA_WIN_W = 16
N_EXPERTS = 16
N_GROUPS = 4
EXPERTS_PER_GROUP = 4
D_EXPERT = D_MODEL // 2
ROPE_BASE = 10000.0
LN_EPS = 1e-5
RMS_EPS = 1e-6
NEG_BIG = -1e30
DEEPNORM_ALPHA = (2 * DEPTH) ** 0.25
LOG2E = 1.4426950408889634

LANES = 128
TOK_TILE = 256
NA_TILE = GRID_W
MOE_TILE = 256
N_PAIRS = 6
N_SEG = N_GROUPS * N_PAIRS
SEG_ROWS = 32
PAIR_LO = (0, 0, 0, 1, 1, 2)
PAIR_HI = (1, 2, 3, 2, 3, 3)
VMEM_LIMIT = 56 * 1024 * 1024

W_GQ, W_GK, W_GV, W_GG, W_GR = 0, 256, 512, 1024, 1536
W_DQ, W_DK, W_DV = 1664, 2176, 2688
W_NQ, W_NK, W_NV = 3200, 3712, 4224
W_SG, W_END = 4736, 7808


def _nt(a, b):
    return lax.dot_general(a, b, (((1,), (1,)), ((), ())), preferred_element_type=F32)


def _tn(a, b):
    return lax.dot_general(a, b, (((0,), (0,)), ((), ())), preferred_element_type=F32)


def _mm(a, b):
    return jnp.dot(a, b, preferred_element_type=F32)


def _sigmoid(x):
    return 1.0 / (1.0 + jnp.exp(-x))


def _params(*sem):
    return pltpu.CompilerParams(dimension_semantics=sem, vmem_limit_bytes=VMEM_LIMIT)


def _ada_kernel(c_ref, w_ref, b_ref, o_ref):
    cs = c_ref[...]
    s = cs * _sigmoid(cs)
    o_ref[0] = jnp.dot(s, w_ref[0], precision=HIGHEST, preferred_element_type=F32) + b_ref[0]


def _ada(cs, w_ada, b_ada):
    depth, d, n = w_ada.shape
    bn = 1536
    return pl.pallas_call(
        _ada_kernel,
        grid=(depth, n // bn),
        in_specs=[
            pl.BlockSpec((cs.shape[0], d), lambda l, j: (0, 0)),
            pl.BlockSpec((1, d, bn), lambda l, j: (l, 0, j)),
            pl.BlockSpec((1, 1, bn), lambda l, j: (l, 0, j)),
        ],
        out_specs=pl.BlockSpec((1, cs.shape[0], bn), lambda l, j: (l, 0, j)),
        out_shape=jax.ShapeDtypeStruct((depth, cs.shape[0], n), F32),
        compiler_params=_params("arbitrary", "arbitrary"),
        name="ada",
    )(cs, w_ada, b_ada.reshape(depth, 1, n))


def _proj_kernel(x_ref, mod_ref, w_ref, cos_ref, sin_ref,
                 gq, gk, gr, gv, gsg, dq, dk, dv, nq, nk, nv, sg):
    x = x_ref[0]
    mod = mod_ref[0]
    h = (x * (1.0 + mod[1:2]) + mod[0:1]).astype(BF16)

    def mm(lo, hi):
        return _mm(h, w_ref[:, lo:hi])

    gq[0] = mm(W_GQ, W_GK) * (GLA_DK ** -0.5)
    gk[0] = mm(W_GK, W_GV)
    gv[0] = mm(W_GV, W_GG).astype(BF16)
    g = mm(W_GG, W_GR)
    gsg[0] = (g * _sigmoid(g)).astype(BF16)
    gr[0] = mm(W_GR, W_DQ)

    cos = cos_ref[...]
    sin = sin_ref[...]

    def rope(y, scale):
        parts = []
        for i in range(DIFF_HEADS):
            p = y[:, i * LANES:(i + 1) * LANES]
            parts.append(((p * cos + pltpu.roll(p, LANES // 2, 1) * sin) * scale).astype(BF16))
        return jnp.concatenate(parts, axis=1)

    dq[0] = rope(mm(W_DQ, W_DK), (DIFF_DH ** -0.5) * LOG2E)
    dk[0] = rope(mm(W_DK, W_DV), 1.0)
    v = mm(W_DV, W_NQ)
    sub = lax.broadcasted_iota(jnp.int32, (DIFF_VROWS - DIFF_DV, v.shape[0]), 0)
    ones_rows = jnp.where(sub == 0, 1.0, 0.0).astype(BF16)
    for i in range(DIFF_HEADS):
        dv[0, i * DIFF_VROWS:i * DIFF_VROWS + DIFF_DV, :] = v[:, i * LANES:(i + 1) * LANES].T.astype(BF16)
        dv[0, i * DIFF_VROWS + DIFF_DV:(i + 1) * DIFF_VROWS, :] = ones_rows
    nq[0] = (mm(W_NQ, W_NK) * ((NA_DH ** -0.5) * LOG2E)).astype(BF16)
    nk[0] = mm(W_NK, W_NV).astype(BF16)
    nv[0] = mm(W_NV, W_SG).astype(BF16)
    for i in range(3):
        lo = W_SG + i * D_MODEL
        sg[0, :, i * D_MODEL:(i + 1) * D_MODEL] = _sigmoid(mm(lo, lo + D_MODEL)).astype(BF16)


def _proj(xs, mod2, w, cos_t, sin_t):
    b, ntok, d = xs.shape
    nt = ntok // TOK_TILE
    widths = (256, 256, 128, 512, 512, 512, 512, None, 512, 512, 512, 3072)
    dtypes = (F32, F32, F32, BF16, BF16, BF16, BF16, BF16, BF16, BF16, BF16, BF16)
    tok = lambda bi, ti: (bi, ti, 0)
    vrows = DIFF_HEADS * DIFF_VROWS

    def out_spec(wd):
        if wd is None:
            return pl.BlockSpec((1, vrows, TOK_TILE), lambda bi, ti: (bi, 0, ti))
        return pl.BlockSpec((1, TOK_TILE, wd), tok)

    def out_shape(wd, dt):
        return jax.ShapeDtypeStruct((b, vrows, ntok) if wd is None else (b, ntok, wd), dt)

    return pl.pallas_call(
        _proj_kernel,
        grid=(b, nt),
        in_specs=[
            pl.BlockSpec((1, TOK_TILE, d), tok),
            pl.BlockSpec((1, 8, d), lambda bi, ti: (2 * bi + jnp.minimum(ti, 1), 0, 0)),
            pl.BlockSpec((d, W_END), lambda bi, ti: (0, 0), pipeline_mode=pl.Buffered(1)),
            pl.BlockSpec((TOK_TILE, LANES), lambda bi, ti: (ti, 0)),
            pl.BlockSpec((TOK_TILE, LANES), lambda bi, ti: (ti, 0)),
        ],
        out_specs=[out_spec(wd) for wd in widths],
        out_shape=[out_shape(wd, dt) for wd, dt in zip(widths, dtypes)],
        compiler_params=_params("arbitrary", "arbitrary"),
        name="proj",
    )(xs, mod2, w, cos_t, sin_t)


def _log_sigmoid(z):
    return -(jnp.maximum(-z, 0.0) + jnp.log(1.0 + jnp.exp(-jnp.abs(z))))


def _gla_kernel(qf, kf, vf, rf, qb, kb, vb, rb, wd_ref, bd_ref, of_ref, ob_ref, st_ref):
    n = pl.program_id(1)

    @pl.when(n == 0)
    def _():
        st_ref[...] = jnp.zeros_like(st_ref)

    c = GLA_CHUNK
    row = lax.broadcasted_iota(jnp.int32, (c, c), 0)
    col = lax.broadcasted_iota(jnp.int32, (c, c), 1)
    lane = lax.broadcasted_iota(jnp.int32, (1, LANES), 1)
    head_mask = (lane < GLA_DK, lane >= GLA_DK)

    for d, (q_ref, k_ref, v_ref, r_ref, o_ref) in enumerate(
            ((qf, kf, vf, rf, of_ref), (qb, kb, vb, rb, ob_ref))):
        causal = (row >= col) if d == 0 else (row <= col)
        tri = jnp.where(causal, 1.0, 0.0).astype(F32)
        z = _mm(r_ref[0].astype(BF16), wd_ref[d]) + bd_ref[d]
        logd = _log_sigmoid(z) * (1.0 / GLA_TAU)
        cum = jnp.dot(tri, logd, precision=HIGHEST, preferred_element_type=F32)
        last = cum[c - 1:c] if d == 0 else cum[0:1]
        k = k_ref[0]
        q_in = q_ref[0] * jnp.exp(cum)
        k_in = (k * jnp.exp(-cum)).astype(BF16)
        k_st = k * jnp.exp(last - cum)
        dec = jnp.exp(last)
        v = v_ref[0]
        for p in range(GLA_HEADS // 2):
            sl = slice(p * LANES, (p + 1) * LANES)
            q2 = q_in[:, sl]
            lhs = jnp.concatenate([jnp.where(head_mask[0], q2, 0.0),
                                   jnp.where(head_mask[1], q2, 0.0)], axis=0).astype(BF16)
            att = _nt(lhs, k_in[:, sl])
            q2b = q2.astype(BF16)
            for hh in range(2):
                h = 2 * p + hh
                a = jnp.where(causal, att[hh * c:(hh + 1) * c], 0.0).astype(BF16)
                vh = v[:, h * GLA_DV:(h + 1) * GLA_DV]
                st = st_ref[d, h]
                o_ref[0, :, h * GLA_DV:(h + 1) * GLA_DV] = _mm(a, vh) + _nt(q2b, st.astype(BF16))
                kh = jnp.where(head_mask[hh], k_st[:, sl], 0.0).astype(BF16)
                st_ref[d, h] = st * dec[:, sl] + _tn(vh, kh)


def _gla(gq, gk, gv, gr, wd, bd):
    b, ntok, _ = gq.shape
    nch = ntok // GLA_CHUNK
    nctx = CTX_LEN // GLA_CHUNK

    def fwd(bi, n):
        return (bi, n, 0)

    def bwd(bi, n):
        return (bi, jnp.where(n < nctx, nctx - 1 - n, nch + nctx - 1 - n), 0)

    def specs(im):
        return [pl.BlockSpec((1, GLA_CHUNK, 256), im), pl.BlockSpec((1, GLA_CHUNK, 256), im),
                pl.BlockSpec((1, GLA_CHUNK, 512), im), pl.BlockSpec((1, GLA_CHUNK, 128), im)]

    return pl.pallas_call(
        _gla_kernel,
        grid=(b, nch),
        in_specs=specs(fwd) + specs(bwd) + [
            pl.BlockSpec((2, LANES, 256), lambda bi, n: (0, 0, 0)),
            pl.BlockSpec((2, 1, 256), lambda bi, n: (0, 0, 0)),
        ],
        out_specs=[pl.BlockSpec((1, GLA_CHUNK, 512), fwd), pl.BlockSpec((1, GLA_CHUNK, 512), bwd)],
        out_shape=[jax.ShapeDtypeStruct((b, ntok, 512), F32)] * 2,
        scratch_shapes=[pltpu.VMEM((2, GLA_HEADS, GLA_DV, LANES), F32)],
        compiler_params=_params("arbitrary", "arbitrary"),
        name="gla",
    )(gq, gk, gv, gr, gq, gk, gv, gr, wd, bd)


DIFF_KC = 256


def _diff_kernel(lam_init, q_ref, k_ref, v_ref, lq_ref, lk_ref, g_ref, o_ref, s_scr):
    t = pl.program_id(2)
    tq = q_ref.shape[1]
    nkc = k_ref.shape[1] // DIFF_KC
    lane = lax.broadcasted_iota(jnp.int32, (1, LANES), 1)
    comp0 = (lane // (DIFF_DH // 2)) % 2 == 0
    q = q_ref[0]
    zero = jnp.zeros_like(q)
    qq = jnp.concatenate([jnp.where(comp0, q, zero), jnp.where(comp0, zero, q)], axis=0)

    def attend(nch):
        m8 = None
        for ci in range(nch):
            s = _nt(k_ref[0, ci * DIFF_KC:(ci + 1) * DIFF_KC, :], qq)
            s_scr[ci] = s
            mc = jnp.max(s.reshape(DIFF_KC // 8, 8, 2 * tq), axis=0)
            m8 = mc if m8 is None else jnp.maximum(m8, mc)
        mrow = jnp.max(m8, axis=0, keepdims=True)
        acc = None
        for ci in range(nch):
            p = jnp.exp2(s_scr[ci] - mrow).astype(BF16)
            part = _mm(v_ref[0, :, ci * DIFF_KC:(ci + 1) * DIFF_KC], p)
            acc = part if acc is None else acc + part
        o = acc[:DIFF_DV] / acc[DIFF_DV:DIFF_DV + 1]
        lql = lq_ref[...] * lk_ref[...]
        lam = (jnp.exp(jnp.sum(lql[0:1], axis=1, keepdims=True))
               - jnp.exp(jnp.sum(lql[1:2], axis=1, keepdims=True)) + lam_init)
        o = o[:, :tq] - lam * o[:, tq:]
        o = o * lax.rsqrt(jnp.mean(o * o, axis=0, keepdims=True) + RMS_EPS) * g_ref[...] * (1.0 - lam_init)
        o_ref[0] = o.T.astype(BF16)

    @pl.when(t == 0)
    def _():
        attend(1)

    @pl.when(t > 0)
    def _():
        attend(nkc)


def _diff(dq, dk, dv, lam_q, lam_k, norm_g, lam_init):
    b, ntok, _ = dq.shape
    nt = ntok // TOK_TILE
    return pl.pallas_call(
        functools.partial(_diff_kernel, lam_init),
        grid=(b, DIFF_HEADS, nt),
        in_specs=[
            pl.BlockSpec((1, TOK_TILE, LANES), lambda bi, h, t: (bi, t, h)),
            pl.BlockSpec((1, ntok, LANES), lambda bi, h, t: (bi, 0, h)),
            pl.BlockSpec((1, DIFF_VROWS, ntok), lambda bi, h, t: (bi, h, 0)),
            pl.BlockSpec((2, DIFF_DH), lambda bi, h, t: (0, 0)),
            pl.BlockSpec((2, DIFF_DH), lambda bi, h, t: (0, 0)),
            pl.BlockSpec((DIFF_DV, 1), lambda bi, h, t: (0, 0)),
        ],
        out_specs=pl.BlockSpec((1, TOK_TILE, LANES), lambda bi, h, t: (bi, t, h)),
        out_shape=jax.ShapeDtypeStruct((b, ntok, DIFF_HEADS * DIFF_DV), BF16),
        scratch_shapes=[pltpu.VMEM((ntok // DIFF_KC, DIFF_KC, 2 * TOK_TILE), F32)],
        compiler_params=_params("arbitrary", "arbitrary", "arbitrary"),
        name="diff",
    )(dq, dk, dv, lam_q, lam_k, norm_g.reshape(DIFF_DV, 1))


def _na_kernel(rows, q_ref, k_ref, v_ref, b_ref, o_ref):
    t = pl.program_id(1)
    nctx = CTX_LEN // NA_TILE
    r = t - nctx
    rs = jnp.clip(r - NA_WIN_H // 2, 0, rows - NA_WIN_H)
    start = pl.multiple_of(CTX_LEN + rs * GRID_W, GRID_W)
    nloc = NA_WIN_H * GRID_W
    kwin = k_ref[0, pl.ds(start, nloc), :]
    vwin = v_ref[0, pl.ds(start, nloc), :]
    kctx = k_ref[0, 0:CTX_LEN, :]
    vctx = v_ref[0, 0:CTX_LEN, :]
    q = q_ref[0]
    lane = lax.broadcasted_iota(jnp.int32, (1, LANES), 1)
    first = lane < NA_DH
    for p in range(NA_HEADS // 2):
        sl = slice(p * LANES, (p + 1) * LANES)
        q2 = q[:, sl]
        zero = jnp.zeros_like(q2)
        lhs = jnp.concatenate([jnp.where(first, q2, zero), jnp.where(first, zero, q2)], axis=0)
        s_loc = _nt(lhs, kwin[:, sl]) + b_ref[p]
        s_ctx = _nt(lhs, kctx[:, sl])
        m = jnp.maximum(jnp.max(s_loc, axis=1, keepdims=True), jnp.max(s_ctx, axis=1, keepdims=True))
        p_loc = jnp.exp2(s_loc - m)
        p_ctx = jnp.exp2(s_ctx - m)
        l = jnp.sum(p_loc, axis=1, keepdims=True) + jnp.sum(p_ctx, axis=1, keepdims=True)
        o = (_mm(p_loc.astype(BF16), vwin[:, sl]) + _mm(p_ctx.astype(BF16), vctx[:, sl])) / l
        o_ref[0, :, sl] = jnp.where(first, o[:NA_TILE], o[NA_TILE:]).astype(BF16)


def _na(nq, nk, nv, bias):
    b, ntok, w = nq.shape
    nt = ntok // NA_TILE
    nctx = CTX_LEN // NA_TILE
    rows = (ntok - CTX_LEN) // GRID_W

    def cfg(bi, t):
        r = t - nctx
        return (jnp.where(t < nctx, NA_WIN_H, r - jnp.clip(r - NA_WIN_H // 2, 0, rows - NA_WIN_H)), 0, 0)

    return pl.pallas_call(
        functools.partial(_na_kernel, rows),
        grid=(b, nt),
        in_specs=[
            pl.BlockSpec((1, NA_TILE, w), lambda bi, t: (bi, t, 0)),
            pl.BlockSpec((1, ntok, w), lambda bi, t: (bi, 0, 0)),
            pl.BlockSpec((1, ntok, w), lambda bi, t: (bi, 0, 0)),
            pl.BlockSpec((NA_HEADS // 2, 2 * NA_TILE, NA_WIN_H * GRID_W), cfg),
        ],
        out_specs=pl.BlockSpec((1, NA_TILE, w), lambda bi, t: (bi, t, 0)),
        out_shape=jax.ShapeDtypeStruct((b, ntok, w), BF16),
        compiler_params=_params("arbitrary", "arbitrary"),
        name="na",
    )(nq, nk, nv, bias)


def _na_bias(rpb):
    qc = np.arange(GRID_W)[:, None]
    kc = np.arange(GRID_W)[None, :]
    col_idx = np.clip(kc - qc + NA_WIN_W - 1, 0, 2 * NA_WIN_W - 2)
    cstart = np.clip(qc - NA_WIN_W // 2, 0, GRID_W - NA_WIN_W)
    win = (kc >= cstart) & (kc < cstart + NA_WIN_W)
    row_idx = np.arange(NA_WIN_H)[None, :] - np.arange(NA_WIN_H)[:, None] + (NA_WIN_H - 1)
    tbl = rpb.astype(F32)[:, row_idx][..., col_idx]
    tbl = jnp.where(win[None, None, None], tbl * LOG2E, NEG_BIG)
    tbl = tbl.transpose(1, 0, 3, 2, 4)
    tbl = tbl.reshape(NA_WIN_H, NA_HEADS // 2, 2 * GRID_W, NA_WIN_H * GRID_W)
    masked = jnp.full((1,) + tbl.shape[1:], NEG_BIG, F32)
    return jnp.concatenate([tbl, masked], axis=0).reshape(-1, 2 * GRID_W, NA_WIN_H * GRID_W)


def _layernorm(x, g, b):
    mu = jnp.mean(x, axis=1, keepdims=True)
    xc = x - mu
    var = jnp.mean(xc * xc, axis=1, keepdims=True)
    return xc * lax.rsqrt(var + LN_EPS) * g + b


def _merge_kernel(x_ref, of_ref, ob_ref, gsg_ref, do_ref, no_ref, sg_ref, mod_ref, gg_ref,
                  wb_ref, wo_ref, lng_ref, lnb_ref, wr_ref, br_ref,
                  x1_ref, h2_ref, route_ref, cnt_ref, run_ref):
    first_step = jnp.logical_and(pl.program_id(0) == 0, pl.program_id(1) == 0)

    @pl.when(first_step)
    def _():
        run_ref[...] = jnp.zeros_like(run_ref)

    tm = x_ref.shape[1]
    mod = mod_ref[0]
    o = of_ref[0] + ob_ref[0]
    gg = gg_ref[...]
    parts = []
    for h in range(GLA_HEADS):
        oh = o[:, h * GLA_DV:(h + 1) * GLA_DV]
        parts.append(oh * lax.rsqrt(jnp.mean(oh * oh, axis=1, keepdims=True) + RMS_EPS) * gg)
    gla = (jnp.concatenate(parts, axis=1) * gsg_ref[0].astype(F32)).astype(BF16)
    branches = (gla, do_ref[0], no_ref[0])
    y = jnp.zeros((tm, D_MODEL), F32)
    for i in range(3):
        y = y + sg_ref[0, :, i * D_MODEL:(i + 1) * D_MODEL].astype(F32) * _mm(branches[i], wb_ref[i])
    y = _mm(y.astype(BF16), wo_ref[...])
    x1 = _layernorm(DEEPNORM_ALPHA * x_ref[0] + mod[2:3] * y, lng_ref[...], lnb_ref[...])
    x1_ref[0] = x1
    h2 = x1 * (1.0 + mod[4:5]) + mod[3:4]
    h2_ref[0, :, :D_MODEL] = h2

    logits = lax.dot_general(wr_ref[...], h2, (((1,), (1,)), ((), ())),
                             precision=HIGHEST, preferred_element_type=F32)
    aff = _sigmoid(logits)
    sel = aff + br_ref[...]
    srow = [sel[e:e + 1] for e in range(N_EXPERTS)]
    arow = [aff[e:e + 1] for e in range(N_EXPERTS)]
    gscore = []
    for g in range(N_GROUPS):
        a0, a1, a2, a3 = srow[4 * g:4 * g + 4]
        hi01, lo01 = jnp.maximum(a0, a1), jnp.minimum(a0, a1)
        hi23, lo23 = jnp.maximum(a2, a3), jnp.minimum(a2, a3)
        top1 = jnp.maximum(hi01, hi23)
        top2 = jnp.maximum(jnp.minimum(hi01, hi23), jnp.maximum(lo01, lo23))
        gscore.append(top1 + top2)
    best = jnp.zeros_like(gscore[0])
    bestv = gscore[0]
    for g in range(1, N_GROUPS):
        better = gscore[g] > bestv
        best = jnp.where(better, float(g), best)
        bestv = jnp.where(better, gscore[g], bestv)

    def pick(rows_, i):
        out = rows_[i]
        for g in range(1, N_GROUPS):
            out = jnp.where(best == float(g), rows_[4 * g + i], out)
        return out

    s4 = [pick(srow, i) for i in range(4)]
    f4 = [pick(arow, i) for i in range(4)]
    chosen = []
    for i in range(4):
        rank = jnp.zeros_like(best)
        for j in range(4):
            if j == i:
                continue
            ahead = (s4[j] > s4[i]) | ((s4[j] == s4[i]) & (j < i))
            rank = rank + jnp.where(ahead, 1.0, 0.0)
        chosen.append(rank < 2.0)
    c0, c1, c2, c3 = chosen
    pidx = jnp.where(c0, jnp.where(c1, 0.0, jnp.where(c2, 1.0, 2.0)),
                     jnp.where(c1, jnp.where(c2, 3.0, 4.0), 5.0))
    a_lo = jnp.where(c0, f4[0], jnp.where(c1, f4[1], f4[2]))
    a_hi = jnp.where(c3, f4[3], jnp.where(c2, f4[2], f4[1]))
    den = a_lo + a_hi
    seg = best * float(N_PAIRS) + pidx

    srows = lax.broadcasted_iota(jnp.int32, (SEG_ROWS, tm), 0).astype(F32)
    onehot = jnp.where(srows == seg, 1.0, 0.0)
    ii = lax.broadcasted_iota(jnp.int32, (tm, tm), 0)
    jj = lax.broadcasted_iota(jnp.int32, (tm, tm), 1)
    before = jnp.where(ii < jj, 1.0, 0.0).astype(BF16)
    prefix = _mm(onehot.astype(BF16), before) + run_ref[...][:, 0:1]
    rank = jnp.sum(onehot * prefix, axis=0, keepdims=True)
    run = run_ref[...] + jnp.sum(onehot, axis=1, keepdims=True)
    run_ref[...] = run
    cnt_ref[...] = run
    zrow = jnp.zeros_like(seg)
    route = jnp.concatenate([seg, rank, a_lo / den, a_hi / den, zrow, zrow, zrow, zrow], axis=0)
    route_ref[0, 0] = route
    wide = jnp.concatenate([route, jnp.zeros((LANES - 8, tm), F32)], axis=0)
    h2_ref[0, :, D_MODEL:] = wide.T


def _merge(xs, o_f, o_b, gsg, d_o, n_o, sg, mod2, gla_g, wb, wo, ln_g, ln_b, wr_t, br):
    b, ntok, d = xs.shape
    nt = ntok // TOK_TILE
    tok = lambda bi, ti: (bi, ti, 0)
    const2 = lambda bi, ti: (0, 0)

    def tokspec(wd):
        return pl.BlockSpec((1, TOK_TILE, wd), tok)

    return pl.pallas_call(
        _merge_kernel,
        grid=(b, nt),
        in_specs=[
            tokspec(d), tokspec(512), tokspec(512), tokspec(512), tokspec(512), tokspec(512),
            tokspec(3 * d),
            pl.BlockSpec((1, 8, d), lambda bi, ti: (2 * bi + jnp.minimum(ti, 1), 0, 0)),
            pl.BlockSpec((1, GLA_DV), const2),
            pl.BlockSpec((3, BRANCH_W, d), lambda bi, ti: (0, 0, 0)),
            pl.BlockSpec((d, d), const2),
            pl.BlockSpec((1, d), const2),
            pl.BlockSpec((1, d), const2),
            pl.BlockSpec((N_EXPERTS, d), const2),
            pl.BlockSpec((N_EXPERTS, 1), const2),
        ],
        out_specs=[
            tokspec(d), tokspec(d + LANES),
            pl.BlockSpec((1, 1, 8, TOK_TILE), lambda bi, ti: (bi, ti, 0, 0)),
            pl.BlockSpec((SEG_ROWS, LANES), const2),
        ],
        out_shape=[
            jax.ShapeDtypeStruct((b, ntok, d), F32),
            jax.ShapeDtypeStruct((b, ntok, d + LANES), F32),
            jax.ShapeDtypeStruct((b, nt, 8, TOK_TILE), F32),
            jax.ShapeDtypeStruct((SEG_ROWS, LANES), F32),
        ],
        scratch_shapes=[pltpu.VMEM((SEG_ROWS, LANES), F32)],
        compiler_params=_params("arbitrary", "arbitrary"),
        name="merge",
    )(xs, o_f, o_b, gsg, d_o, n_o, sg, mod2, gla_g.reshape(1, GLA_DV), wb, wo,
      ln_g.reshape(1, d), ln_b.reshape(1, d), wr_t, br.reshape(N_EXPERTS, 1))


def _plan_kernel(dest_ref, src_ref):
    def clear(j, carry):
        src_ref[j] = 0
        return carry
    lax.fori_loop(0, src_ref.shape[0], clear, 0, unroll=8)

    def place(j, carry):
        src_ref[dest_ref[j]] = j
        return carry
    lax.fori_loop(0, dest_ref.shape[0], place, 0, unroll=8)


def _plan(dest, tpad):
    grid_spec = pltpu.PrefetchScalarGridSpec(
        num_scalar_prefetch=1, grid=(1,), in_specs=[],
        out_specs=pl.BlockSpec(memory_space=pltpu.SMEM))
    return pl.pallas_call(
        _plan_kernel,
        grid_spec=grid_spec,
        out_shape=jax.ShapeDtypeStruct((tpad,), jnp.int32),
        compiler_params=_params("arbitrary"),
        name="plan",
    )(dest)


def _moe_kernel(src, meta, h_hbm, wg1, wu1, wd1, wg2, wu2, wd2, y_hbm,
                xbuf, ybuf, sem_in, sem_out):
    i = pl.program_id(0)
    ntile = pl.num_programs(0)
    n_used = meta[2 * ntile]
    slot = i % 2
    tm = MOE_TILE

    def n_real(tile):
        return meta[2 * ntile + 1 + tile]

    def gather(tile, sl):
        def body(j, carry):
            tok = src[tile * tm + j]
            pltpu.make_async_copy(h_hbm.at[pl.ds(tok, 1)], xbuf.at[sl, pl.ds(j, 1)], sem_in.at[sl]).start()
            return carry
        lax.fori_loop(0, tm, body, 0, unroll=8)

    def wait_in(sl):
        pltpu.make_async_copy(h_hbm.at[pl.ds(0, tm)], xbuf.at[sl], sem_in.at[sl]).wait()

    def row_out(tile, sl, j):
        tok = src[tile * tm + j]
        return pltpu.make_async_copy(ybuf.at[sl, pl.ds(j, 1)], y_hbm.at[pl.ds(tok, 1)], sem_out.at[sl])

    def scatter(tile, sl):
        nr = n_real(tile)

        def body(j, carry):
            row_out(tile, sl, j).start()
            return carry

        @pl.when(nr == tm)
        def _():
            lax.fori_loop(0, tm, body, 0, unroll=8)

        @pl.when(nr < tm)
        def _():
            lax.fori_loop(0, nr, body, 0)

    def wait_out(tile, sl):
        nr = n_real(tile)

        @pl.when(nr == tm)
        def _():
            pltpu.make_async_copy(ybuf.at[sl], y_hbm.at[pl.ds(0, tm)], sem_out.at[sl]).wait()

        @pl.when(nr < tm)
        def _():
            def body(j, carry):
                row_out(tile, sl, j).wait()
                return carry
            lax.fori_loop(0, nr, body, 0)

    @pl.when(i == 0)
    def _():
        gather(0, 0)

    @pl.when(i + 1 < n_used)
    def _():
        gather(i + 1, 1 - slot)

    @pl.when(i < n_used)
    def _():
        wait_in(slot)

        @pl.when(i >= 2)
        def _():
            wait_out(i - 2, slot)

        x = xbuf[slot, :, :D_MODEL].astype(BF16)

        def ffn(wg, wu, wd):
            a = _mm(x, wg[0])
            u = _mm(x, wu[0])
            return _mm((a * _sigmoid(a) * u).astype(BF16), wd[0])

        w = xbuf[slot, :, D_MODEL:]
        ybuf[slot] = w[:, 2:3] * ffn(wg1, wu1, wd1) + w[:, 3:4] * ffn(wg2, wu2, wd2)
        scatter(i, slot)

    @pl.when(i == ntile - 1)
    def _():
        wait_out(n_used - 1, (n_used - 1) % 2)

        @pl.when(n_used >= 2)
        def _():
            wait_out(n_used - 2, n_used % 2)


def _moe(h2, src, meta, wg, wu, wd):
    t, dw = h2.shape
    d = dw - LANES
    tpad = src.shape[0]
    ntile = tpad // MOE_TILE
    lo = lambda i, s, m: (m[2 * i], 0, 0)
    hi = lambda i, s, m: (m[2 * i + 1], 0, 0)
    grid_spec = pltpu.PrefetchScalarGridSpec(
        num_scalar_prefetch=2,
        grid=(ntile,),
        in_specs=[
            pl.BlockSpec(memory_space=pl.ANY),
            pl.BlockSpec((1, d, D_EXPERT), lo), pl.BlockSpec((1, d, D_EXPERT), lo),
            pl.BlockSpec((1, D_EXPERT, d), lo),
            pl.BlockSpec((1, d, D_EXPERT), hi), pl.BlockSpec((1, d, D_EXPERT), hi),
            pl.BlockSpec((1, D_EXPERT, d), hi),
        ],
        out_specs=pl.BlockSpec(memory_space=pl.ANY),
        scratch_shapes=[pltpu.VMEM((2, MOE_TILE, dw), F32), pltpu.VMEM((2, MOE_TILE, d), F32),
                        pltpu.SemaphoreType.DMA((2,)), pltpu.SemaphoreType.DMA((2,))],
    )
    return pl.pallas_call(
        _moe_kernel,
        grid_spec=grid_spec,
        out_shape=jax.ShapeDtypeStruct((t, d), F32),
        compiler_params=_params("arbitrary"),
        name="moe",
    )(src, meta, h2, wg, wu, wd, wg, wu, wd)


def _lookup(idx, table):
    n = table.shape[0]
    hit = idx[:, None] == jnp.arange(n, dtype=jnp.int32)[None, :]
    return jnp.sum(jnp.where(hit, table[None, :], 0), axis=1)


def _dispatch_plan(route, counts, t):
    ntile = (t + N_SEG * (MOE_TILE - 1)) // MOE_TILE
    seg = route[:, :, 0, :].reshape(t).astype(jnp.int32)
    rank = route[:, :, 1, :].reshape(t).astype(jnp.int32)
    cnt = counts[:N_SEG, 0].astype(jnp.int32)
    seg_tiles = (cnt + MOE_TILE - 1) // MOE_TILE
    upto = jnp.arange(N_SEG)[None, :] <= jnp.arange(N_SEG)[:, None]
    tile_end = jnp.sum(jnp.where(upto, seg_tiles[None, :], 0), axis=1)
    first_tile = tile_end - seg_tiles
    dest = _lookup(seg, first_tile * MOE_TILE) + rank
    n_used = tile_end[-1]
    tiles = jnp.arange(ntile, dtype=jnp.int32)
    tile_seg = jnp.sum((tile_end[None, :] <= jnp.minimum(tiles, n_used - 1)[:, None]).astype(jnp.int32), axis=1)
    tile_seg = jnp.minimum(tile_seg, N_SEG - 1)
    base = (tile_seg // N_PAIRS) * EXPERTS_PER_GROUP
    pair = tile_seg % N_PAIRS
    e_lo = base + _lookup(pair, jnp.asarray(PAIR_LO, jnp.int32))
    e_hi = base + _lookup(pair, jnp.asarray(PAIR_HI, jnp.int32))
    left = _lookup(tile_seg, cnt) - (tiles - _lookup(tile_seg, first_tile)) * MOE_TILE
    n_real = jnp.where(tiles < n_used, jnp.clip(left, 0, MOE_TILE), 0)
    meta = jnp.concatenate([jnp.stack([e_lo, e_hi], axis=1).reshape(-1), n_used[None], n_real]).astype(jnp.int32)
    return _plan(dest, ntile * MOE_TILE), meta


def _final_kernel(x_ref, y_ref, mod_ref, g_ref, b_ref, o_ref):
    mod = mod_ref[0]
    o_ref[0] = _layernorm(DEEPNORM_ALPHA * x_ref[0] + mod[5:6] * y_ref[...], g_ref[...], b_ref[...])


def _final(x1, y, mod2, ln_g, ln_b):
    b, ntok, d = x1.shape
    nt = ntok // TOK_TILE
    return pl.pallas_call(
        _final_kernel,
        grid=(b, nt),
        in_specs=[
            pl.BlockSpec((1, TOK_TILE, d), lambda bi, ti: (bi, ti, 0)),
            pl.BlockSpec((TOK_TILE, d), lambda bi, ti: (bi * nt + ti, 0)),
            pl.BlockSpec((1, 8, d), lambda bi, ti: (2 * bi + jnp.minimum(ti, 1), 0, 0)),
            pl.BlockSpec((1, d), lambda bi, ti: (0, 0)),
            pl.BlockSpec((1, d), lambda bi, ti: (0, 0)),
        ],
        out_specs=pl.BlockSpec((1, TOK_TILE, d), lambda bi, ti: (bi, ti, 0)),
        out_shape=jax.ShapeDtypeStruct((b, ntok, d), F32),
        compiler_params=_params("arbitrary", "arbitrary"),
        name="final_ln",
    )(x1, y, mod2, ln_g.reshape(1, d), ln_b.reshape(1, d))


def _diff_perm():
    j = np.arange(LANES)
    within = ((j // 32) % 2) * DIFF_DH + (j // 64) * (DIFF_DH // 2) + (j % 32)
    return np.concatenate([h * LANES + within for h in range(DIFF_HEADS)])


def _pack_w_in(w_in):
    splits = np.cumsum([256, 256, 512, 512, 32, 512, 512, 512, 512, 512, 512])
    gq, gk, gv, gg, gr, dq, dk, dv, nq, nk, nv, sg = jnp.split(w_in, splits, axis=-1)
    perm = _diff_perm()
    gr = jnp.pad(gr, ((0, 0), (0, 0), (0, LANES - 2 * GLA_RANK)))
    packed = jnp.concatenate([gq, gk, gv, gg, gr, dq[..., perm], dk[..., perm], dv, nq, nk, nv, sg], axis=-1)
    return packed.astype(BF16)


def _rope_tables(n_lat):
    t = jnp.arange(n_lat)
    row = (t // GRID_W).astype(F32)
    col = (t % GRID_W).astype(F32)
    n_freq = DIFF_DH // 4
    inv = ROPE_BASE ** (-jnp.arange(n_freq, dtype=F32) / n_freq)
    ang = jnp.concatenate([row[:, None] * inv, col[:, None] * inv], -1)
    cos, sin = jnp.cos(ang), jnp.sin(ang)
    cos_t = jnp.concatenate([cos] * 4, axis=1)
    sin_t = jnp.concatenate([-sin, -sin, sin, sin], axis=1)
    cos_t = jnp.concatenate([jnp.ones((CTX_LEN, LANES), F32), cos_t], axis=0)
    sin_t = jnp.concatenate([jnp.zeros((CTX_LEN, LANES), F32), sin_t], axis=0)
    return cos_t, sin_t


def _pack_decay(w_decay, b_decay):
    depth = w_decay.shape[0]
    wd = jnp.zeros((depth, 2, LANES, GLA_HEADS * GLA_DK), F32)
    for d in range(2):
        wd = wd.at[:, d, d * GLA_RANK:(d + 1) * GLA_RANK].set(w_decay[:, d])
    return wd.astype(BF16), b_decay.reshape(depth, 2, 1, GLA_HEADS * GLA_DK)


def kernel(x, c, ctx, c_ctx, w_ada, b_ada, w_in, gla_w_decay, gla_b_decay, gla_norm_g, diff_lam_q,
           diff_lam_k, diff_norm_g, na_rpb, w_branch, w_o, ln_g, ln_b, w_router, b_router,
           w_exp_gate, w_exp_up, w_exp_down):
    b, l, d = x.shape
    lc = ctx.shape[1]
    ntok = lc + l
    t = b * ntok

    w_in_p = _pack_w_in(w_in)
    wd_p, bd_p = _pack_decay(gla_w_decay, gla_b_decay)
    cos_t, sin_t = _rope_tables(l)
    wb = w_branch.astype(BF16)
    wo = w_o.astype(BF16)
    wg = w_exp_gate.astype(BF16)
    wu = w_exp_up.astype(BF16)
    wdn = w_exp_down.astype(BF16)
    wr_t = w_router.T

    cs = jnp.concatenate([c, c_ctx[None], jnp.zeros((16 - b - 1, d), F32)], axis=0)
    mods = _ada(cs, w_ada, b_ada).reshape(DEPTH, 16, 6, d)

    xs = jnp.concatenate([ctx, x], axis=1)
    for layer in range(DEPTH):
        lam_init = 0.8 - 0.6 * math.exp(-0.3 * layer)
        m_lat = mods[layer, :b]
        m_ctx = jnp.broadcast_to(mods[layer, b][None], (b, 6, d))
        mod2 = jnp.stack([m_ctx, m_lat], axis=1).reshape(2 * b, 6, d)
        mod2 = jnp.pad(mod2, ((0, 0), (0, 2), (0, 0)))

        gq, gk, gr, gv, gsg, dq, dk, dv, nq, nk, nv, sg = _proj(xs, mod2, w_in_p[layer], cos_t, sin_t)
        o_f, o_b = _gla(gq, gk, gv, gr, wd_p[layer], bd_p[layer])
        d_o = _diff(dq, dk, dv, diff_lam_q[layer], diff_lam_k[layer], diff_norm_g[layer], lam_init)
        n_o = _na(nq, nk, nv, _na_bias(na_rpb[layer]))
        x1, h2, route, counts = _merge(xs, o_f, o_b, gsg, d_o, n_o, sg, mod2, gla_norm_g[layer],
                                       wb[layer], wo[layer], ln_g[layer, 0], ln_b[layer, 0],
                                       wr_t, b_router)
        src, meta = _dispatch_plan(route, counts, t)
        y = _moe(h2.reshape(t, d + LANES), src, meta, wg[layer], wu[layer], wdn[layer])
        xs = _final(x1, y, mod2, ln_g[layer, 1], ln_b[layer, 1])
    return xs[:, lc:]
```

```python
import functools
import itertools
import math

import jax
import jax.numpy as jnp
import numpy as np
from jax import lax
from jax.experimental import pallas as pl
from jax.experimental.pallas import tpu as pltpu

F32 = jnp.float32
BF16 = jnp.bfloat16
HIGHEST = lax.Precision.HIGHEST

D_MODEL = 1024
DEPTH = 4
GRID_W = 64
CTX_LEN = 256
BRANCH_W = D_MODEL // 2
GLA_HEADS = 4
GLA_DV = 128
GLA_DK = 64
GLA_RANK = 16
GLA_TAU = 16.0
GLA_CHUNK = 64
GLA_BATCH = 2
DIFF_HEADS = 4
DIFF_DV = 128
DIFF_DH = 64
DIFF_VROWS = DIFF_DV + 16
NA_HEADS = 8
NA_DH = 64
NA_WIN_H = 8
NA_WIN_W = 16
N_EXPERTS = 16
N_GROUPS = 4
EXPERTS_PER_GROUP = 4
D_EXPERT = D_MODEL // 2
ROPE_BASE = 10000.0
LN_EPS = 1e-5
RMS_EPS = 1e-6
NEG_BIG = -1e30
DEEPNORM_ALPHA = (2 * DEPTH) ** 0.25
LOG2E = 1.4426950408889634

LANES = 128
TOK_TILE = 256
NA_TILE = GRID_W
NA_BATCH = 2
MERGE_BATCH = 2
MOE_TILE = 256
N_PAIRS = 6
N_SEG = N_GROUPS * N_PAIRS
SEG_ROWS = 32
PAIR_LO = (0, 0, 0, 1, 1, 2)
PAIR_HI = (1, 2, 3, 2, 3, 3)
VMEM_LIMIT = 56 * 1024 * 1024

W_GQ, W_GK, W_GV, W_GG, W_GR = 0, 256, 512, 1024, 1536
W_DQ, W_DK, W_DV = 1664, 2176, 2688
W_NQ, W_NK, W_NV = 3200, 3712, 4224
W_SG, W_END = 4736, 7808


def _nt(a, b):
    return lax.dot_general(a, b, (((1,), (1,)), ((), ())), preferred_element_type=F32)


def _tn(a, b):
    return lax.dot_general(a, b, (((0,), (0,)), ((), ())), preferred_element_type=F32)


def _mm(a, b):
    return jnp.dot(a, b, preferred_element_type=F32)


def _sigmoid(x):
    return 1.0 / (1.0 + jnp.exp(-x))


def _params(*sem):
    return pltpu.CompilerParams(dimension_semantics=sem, vmem_limit_bytes=VMEM_LIMIT)


def _ada_kernel(c_ref, w_ref, b_ref, o_ref):
    cs = c_ref[...]
    s = cs * _sigmoid(cs)
    o_ref[0] = jnp.dot(s, w_ref[0], precision=HIGHEST, preferred_element_type=F32) + b_ref[0]


def _ada(cs, w_ada, b_ada):
    depth, d, n = w_ada.shape
    bn = 1536
    return pl.pallas_call(
        _ada_kernel,
        grid=(depth, n // bn),
        in_specs=[
            pl.BlockSpec((cs.shape[0], d), lambda l, j: (0, 0)),
            pl.BlockSpec((1, d, bn), lambda l, j: (l, 0, j)),
            pl.BlockSpec((1, 1, bn), lambda l, j: (l, 0, j)),
        ],
        out_specs=pl.BlockSpec((1, cs.shape[0], bn), lambda l, j: (l, 0, j)),
        out_shape=jax.ShapeDtypeStruct((depth, cs.shape[0], n), F32),
        compiler_params=_params("arbitrary", "arbitrary"),
        name="ada",
    )(cs, w_ada, b_ada.reshape(depth, 1, n))


def _proj_kernel(x_ref, mod_ref, w_ref, cos_ref, sin_ref,
                 gq, gk, gr, gv, gsg, dq, dk, dv, nq, nk, nv, sg):
    x = x_ref[0]
    mod = mod_ref[0]
    h = (x * (1.0 + mod[1:2]) + mod[0:1]).astype(BF16)

    def mm(lo, hi):
        return _mm(h, w_ref[0, :, lo:hi])

    gq[0] = mm(W_GQ, W_GK) * (GLA_DK ** -0.5)
    gk[0] = mm(W_GK, W_GV)
    gv[0] = mm(W_GV, W_GG).astype(BF16)
    g = mm(W_GG, W_GR)
    gsg[0] = (g * _sigmoid(g)).astype(BF16)
    gr[0] = mm(W_GR, W_DQ)

    cos = cos_ref[...]
    sin = sin_ref[...]

    def rope(y, scale):
        parts = []
        for i in range(DIFF_HEADS):
            p = y[:, i * LANES:(i + 1) * LANES]
            parts.append(((p * cos + pltpu.roll(p, LANES // 2, 1) * sin) * scale).astype(BF16))
        return jnp.concatenate(parts, axis=1)

    dq[0] = rope(mm(W_DQ, W_DK), (DIFF_DH ** -0.5) * LOG2E)
    dk[0] = rope(mm(W_DK, W_DV), 1.0)
    v = mm(W_DV, W_NQ)
    sub = lax.broadcasted_iota(jnp.int32, (DIFF_VROWS - DIFF_DV, v.shape[0]), 0)
    ones_rows = jnp.where(sub == 0, 1.0, 0.0).astype(BF16)
    for i in range(DIFF_HEADS):
        dv[0, i * DIFF_VROWS:i * DIFF_VROWS + DIFF_DV, :] = v[:, i * LANES:(i + 1) * LANES].T.astype(BF16)
        dv[0, i * DIFF_VROWS + DIFF_DV:(i + 1) * DIFF_VROWS, :] = ones_rows
    nq[0] = (mm(W_NQ, W_NK) * ((NA_DH ** -0.5) * LOG2E)).astype(BF16)
    nk[0] = mm(W_NK, W_NV).astype(BF16)
    nv[0] = mm(W_NV, W_SG).astype(BF16)
    for i in range(3):
        lo = W_SG + i * D_MODEL
        sg[0, :, i * D_MODEL:(i + 1) * D_MODEL] = _sigmoid(mm(lo, lo + D_MODEL)).astype(BF16)


def _proj(xs, mod2, w, cos_t, sin_t, layer):
    b, ntok, d = xs.shape
    nt = ntok // TOK_TILE
    widths = (256, 256, 128, 512, 512, 512, 512, None, 512, 512, 512, 3072)
    dtypes = (F32, F32, F32, BF16, BF16, BF16, BF16, BF16, BF16, BF16, BF16, BF16)
    tok = lambda bi, ti: (bi, ti, 0)
    vrows = DIFF_HEADS * DIFF_VROWS

    def out_spec(wd):
        if wd is None:
            return pl.BlockSpec((1, vrows, TOK_TILE), lambda bi, ti: (bi, 0, ti))
        return pl.BlockSpec((1, TOK_TILE, wd), tok)

    def out_shape(wd, dt):
        return jax.ShapeDtypeStruct((b, vrows, ntok) if wd is None else (b, ntok, wd), dt)

    return pl.pallas_call(
        _proj_kernel,
        grid=(b, nt),
        in_specs=[
            pl.BlockSpec((1, TOK_TILE, d), tok),
            pl.BlockSpec((1, 8, d), lambda bi, ti: (2 * bi + jnp.minimum(ti, 1), 0, 0)),
            pl.BlockSpec((1, d, W_END), lambda bi, ti: (layer, 0, 0), pipeline_mode=pl.Buffered(1)),
            pl.BlockSpec((TOK_TILE, LANES), lambda bi, ti: (ti, 0)),
            pl.BlockSpec((TOK_TILE, LANES), lambda bi, ti: (ti, 0)),
        ],
        out_specs=[out_spec(wd) for wd in widths],
        out_shape=[out_shape(wd, dt) for wd, dt in zip(widths, dtypes)],
        compiler_params=_params("arbitrary", "arbitrary"),
        name="proj",
    )(xs, mod2, w, cos_t, sin_t)


def _log_sigmoid(z):
    return -(jnp.maximum(-z, 0.0) + jnp.log(1.0 + jnp.exp(-jnp.abs(z))))


def _gla_kernel(qf, kf, vf, rf, qb, kb, vb, rb, wd_ref, bd_ref, of_ref, ob_ref, *st_refs):
    n = pl.program_id(1)

    @pl.when(n == 0)
    def _():
        for st_ref in st_refs:
            st_ref[...] = jnp.zeros_like(st_ref)

    c = GLA_CHUNK
    row = lax.broadcasted_iota(jnp.int32, (c, c), 0)
    col = lax.broadcasted_iota(jnp.int32, (c, c), 1)
    lane = lax.broadcasted_iota(jnp.int32, (1, LANES), 1)
    head_mask = (lane < GLA_DK, lane >= GLA_DK)

    dirs = ((qf, kf, vf, rf, of_ref), (qb, kb, vb, rb, ob_ref))
    chains = list(itertools.product(range(qf.shape[0]), range(2)))
    causal = [(row >= col), (row <= col)]
    tri = [jnp.where(cz, 1.0, 0.0).astype(F32) for cz in causal]
    pairs = range(GLA_HEADS // 2)
    psl = [slice(p * LANES, (p + 1) * LANES) for p in pairs]

    z = [_mm(dirs[d][3][bb].astype(BF16), wd_ref[0, d]) + bd_ref[0, d] for bb, d in chains]
    cum = [jnp.dot(tri[d], _log_sigmoid(zc) * (1.0 / GLA_TAU), precision=HIGHEST,
                   preferred_element_type=F32) for (bb, d), zc in zip(chains, z)]
    q_in, k_in, k_st, dec = [], [], [], []
    for (bb, d), cm in zip(chains, cum):
        last = cm[c - 1:c] if d == 0 else cm[0:1]
        k = dirs[d][1][bb]
        q_in.append(dirs[d][0][bb] * jnp.exp(cm))
        k_in.append((k * jnp.exp(-cm)).astype(BF16))
        k_st.append(k * jnp.exp(last - cm))
        dec.append(jnp.exp(last))
    att = []
    for ci in range(len(chains)):
        for p in pairs:
            q2 = q_in[ci][:, psl[p]]
            lhs = jnp.concatenate([jnp.where(head_mask[0], q2, 0.0),
                                   jnp.where(head_mask[1], q2, 0.0)], axis=0).astype(BF16)
            att.append(_nt(lhs, k_in[ci][:, psl[p]]))
    for ci, (bb, d) in enumerate(chains):
        v = dirs[d][2][bb]
        for p in pairs:
            q2b = q_in[ci][:, psl[p]].astype(BF16)
            for hh in range(2):
                h = 2 * p + hh
                a = jnp.where(causal[d], att[ci * len(pairs) + p][hh * c:(hh + 1) * c], 0.0).astype(BF16)
                st = st_refs[(bb * 2 + d) * GLA_HEADS + h][...]
                dirs[d][4][bb, :, h * GLA_DV:(h + 1) * GLA_DV] = (
                    _mm(a, v[:, h * GLA_DV:(h + 1) * GLA_DV]) + _nt(q2b, st.astype(BF16)))
    for ci, (bb, d) in enumerate(chains):
        v = dirs[d][2][bb]
        for p in pairs:
            for hh in range(2):
                h = 2 * p + hh
                st_ref = st_refs[(bb * 2 + d) * GLA_HEADS + h]
                kh = jnp.where(head_mask[hh], k_st[ci][:, psl[p]], 0.0).astype(BF16)
                st_ref[...] = st_ref[...] * dec[ci][:, psl[p]] + _tn(v[:, h * GLA_DV:(h + 1) * GLA_DV], kh)


def _gla(gq, gk, gv, gr, wd, bd, layer):
    b, ntok, _ = gq.shape
    nch = ntok // GLA_CHUNK
    nctx = CTX_LEN // GLA_CHUNK
    nb = GLA_BATCH if b % GLA_BATCH == 0 else 1

    def fwd(bi, n):
        return (bi, n, 0)

    def bwd(bi, n):
        return (bi, jnp.where(n < nctx, nctx - 1 - n, nch + nctx - 1 - n), 0)

    def specs(im):
        return [pl.BlockSpec((nb, GLA_CHUNK, 256), im), pl.BlockSpec((nb, GLA_CHUNK, 256), im),
                pl.BlockSpec((nb, GLA_CHUNK, 512), im), pl.BlockSpec((nb, GLA_CHUNK, 128), im)]

    return pl.pallas_call(
        _gla_kernel,
        grid=(b // nb, nch),
        in_specs=specs(fwd) + specs(bwd) + [
            pl.BlockSpec((1, 2, LANES, 256), lambda bi, n: (layer, 0, 0, 0)),
            pl.BlockSpec((1, 2, 1, 256), lambda bi, n: (layer, 0, 0, 0)),
        ],
        out_specs=[pl.BlockSpec((nb, GLA_CHUNK, 512), fwd), pl.BlockSpec((nb, GLA_CHUNK, 512), bwd)],
        out_shape=[jax.ShapeDtypeStruct((b, ntok, 512), F32)] * 2,
        scratch_shapes=[pltpu.VMEM((GLA_DV, LANES), F32)] * (nb * 2 * GLA_HEADS),
        compiler_params=_params("arbitrary", "arbitrary"),
        name="gla",
    )(gq, gk, gv, gr, gq, gk, gv, gr, wd, bd)


DIFF_KC = 256


def _diff_kernel(lam_init, q_ref, k_ref, v_ref, lq_ref, lk_ref, g_ref, o_ref, s_scr):
    t = pl.program_id(2)
    tq = q_ref.shape[1]
    nkc = k_ref.shape[1] // DIFF_KC
    lane = lax.broadcasted_iota(jnp.int32, (1, LANES), 1)
    comp0 = (lane // (DIFF_DH // 2)) % 2 == 0
    q = q_ref[0]
    zero = jnp.zeros_like(q)
    qq = jnp.concatenate([jnp.where(comp0, q, zero), jnp.where(comp0, zero, q)], axis=0)

    def attend(nch):
        m8 = None
        for ci in range(nch):
            s = _nt(k_ref[0, ci * DIFF_KC:(ci + 1) * DIFF_KC, :], qq)
            s_scr[ci] = s
            mc = jnp.max(s.reshape(DIFF_KC // 8, 8, 2 * tq), axis=0)
            m8 = mc if m8 is None else jnp.maximum(m8, mc)
        mrow = jnp.max(m8, axis=0, keepdims=True)
        acc = None
        for ci in range(nch):
            p = jnp.exp2(s_scr[ci] - mrow).astype(BF16)
            part = _mm(v_ref[0, :, ci * DIFF_KC:(ci + 1) * DIFF_KC], p)
            acc = part if acc is None else acc + part
        o = acc[:DIFF_DV] / acc[DIFF_DV:DIFF_DV + 1]
        lql = lq_ref[...] * lk_ref[...]
        lam = (jnp.exp(jnp.sum(lql[0:1], axis=1, keepdims=True))
               - jnp.exp(jnp.sum(lql[1:2], axis=1, keepdims=True)) + lam_init)
        o = o[:, :tq] - lam * o[:, tq:]
        o = o * lax.rsqrt(jnp.mean(o * o, axis=0, keepdims=True) + RMS_EPS) * g_ref[...] * (1.0 - lam_init)
        o_ref[0] = o.T.astype(BF16)

    @pl.when(t == 0)
    def _():
        attend(1)

    @pl.when(t > 0)
    def _():
        attend(nkc)


def _diff(dq, dk, dv, lam_q, lam_k, norm_g, lam_init):
    b, ntok, _ = dq.shape
    nt = ntok // TOK_TILE
    return pl.pallas_call(
        functools.partial(_diff_kernel, lam_init),
        grid=(b, DIFF_HEADS, nt),
        in_specs=[
            pl.BlockSpec((1, TOK_TILE, LANES), lambda bi, h, t: (bi, t, h)),
            pl.BlockSpec((1, ntok, LANES), lambda bi, h, t: (bi, 0, h)),
            pl.BlockSpec((1, DIFF_VROWS, ntok), lambda bi, h, t: (bi, h, 0)),
            pl.BlockSpec((2, DIFF_DH), lambda bi, h, t: (0, 0)),
            pl.BlockSpec((2, DIFF_DH), lambda bi, h, t: (0, 0)),
            pl.BlockSpec((DIFF_DV, 1), lambda bi, h, t: (0, 0)),
        ],
        out_specs=pl.BlockSpec((1, TOK_TILE, LANES), lambda bi, h, t: (bi, t, h)),
        out_shape=jax.ShapeDtypeStruct((b, ntok, DIFF_HEADS * DIFF_DV), BF16),
        scratch_shapes=[pltpu.VMEM((ntok // DIFF_KC, DIFF_KC, 2 * TOK_TILE), F32)],
        compiler_params=_params("arbitrary", "arbitrary", "arbitrary"),
        name="diff",
    )(dq, dk, dv, lam_q, lam_k, norm_g.reshape(DIFF_DV, 1))


def _na_kernel(rows, q_ref, k_ref, v_ref, b_ref, o_ref):
    t = pl.program_id(1)
    nctx = CTX_LEN // NA_TILE
    r = t - nctx
    rs = jnp.clip(r - NA_WIN_H // 2, 0, rows - NA_WIN_H)
    start = pl.multiple_of(CTX_LEN + rs * GRID_W, GRID_W)
    nloc = NA_WIN_H * GRID_W
    lane = lax.broadcasted_iota(jnp.int32, (1, LANES), 1)
    first = lane < NA_DH
    chains = list(itertools.product(range(q_ref.shape[0]), range(NA_HEADS // 2)))
    psl = [slice(p * LANES, (p + 1) * LANES) for p in range(NA_HEADS // 2)]
    scores = []
    for bb, p in chains:
        q2 = q_ref[bb, :, psl[p]]
        zero = jnp.zeros_like(q2)
        lhs = jnp.concatenate([jnp.where(first, q2, zero), jnp.where(first, zero, q2)], axis=0)
        s_loc = _nt(lhs, k_ref[bb, pl.ds(start, nloc), psl[p]]) + b_ref[p]
        s_ctx = _nt(lhs, k_ref[bb, 0:CTX_LEN, psl[p]])
        scores.append((s_loc, s_ctx))
    probs = []
    for s_loc, s_ctx in scores:
        m = jnp.maximum(jnp.max(s_loc, axis=1, keepdims=True), jnp.max(s_ctx, axis=1, keepdims=True))
        p_loc = jnp.exp2(s_loc - m)
        p_ctx = jnp.exp2(s_ctx - m)
        l = jnp.sum(p_loc, axis=1, keepdims=True) + jnp.sum(p_ctx, axis=1, keepdims=True)
        probs.append((p_loc.astype(BF16), p_ctx.astype(BF16), l))
    for (bb, p), (p_loc, p_ctx, l) in zip(chains, probs):
        o = (_mm(p_loc, v_ref[bb, pl.ds(start, nloc), psl[p]]) + _mm(p_ctx, v_ref[bb, 0:CTX_LEN, psl[p]])) / l
        o_ref[bb, :, psl[p]] = jnp.where(first, o[:NA_TILE], o[NA_TILE:]).astype(BF16)


def _na(nq, nk, nv, bias):
    b, ntok, w = nq.shape
    nt = ntok // NA_TILE
    nctx = CTX_LEN // NA_TILE
    rows = (ntok - CTX_LEN) // GRID_W
    nb = NA_BATCH if b % NA_BATCH == 0 else 1

    def cfg(bi, t):
        r = t - nctx
        return (jnp.where(t < nctx, NA_WIN_H, r - jnp.clip(r - NA_WIN_H // 2, 0, rows - NA_WIN_H)), 0, 0)

    kv_spec = pl.BlockSpec((nb, ntok, w), lambda bi, t: (bi, 0, 0), pipeline_mode=pl.Buffered(1))
    return pl.pallas_call(
        functools.partial(_na_kernel, rows),
        grid=(b // nb, nt),
        in_specs=[
            pl.BlockSpec((nb, NA_TILE, w), lambda bi, t: (bi, t, 0)),
            kv_spec, kv_spec,
            pl.BlockSpec((NA_HEADS // 2, 2 * NA_TILE, NA_WIN_H * GRID_W), cfg),
        ],
        out_specs=pl.BlockSpec((nb, NA_TILE, w), lambda bi, t: (bi, t, 0)),
        out_shape=jax.ShapeDtypeStruct((b, ntok, w), BF16),
        compiler_params=_params("arbitrary", "arbitrary"),
        name="na",
    )(nq, nk, nv, bias)


def _na_bias(rpb):
    n_col = 2 * NA_WIN_W
    qc = np.arange(GRID_W)[:, None]
    kc = np.arange(GRID_W)[None, :]
    cstart = np.clip(qc - NA_WIN_W // 2, 0, GRID_W - NA_WIN_W)
    win = (kc >= cstart) & (kc < cstart + NA_WIN_W)
    col_idx = np.where(win, np.clip(kc - qc + NA_WIN_W - 1, 0, n_col - 2), n_col - 1)
    row_idx = np.arange(NA_WIN_H)[None, :] - np.arange(NA_WIN_H)[:, None] + (NA_WIN_H - 1)
    head = np.arange(NA_HEADS).reshape(NA_HEADS // 2, 2)
    flat = ((head[None, :, :, None, None, None] * (2 * NA_WIN_H - 1)
             + row_idx[:, None, None, None, :, None]) * n_col
            + col_idx[None, None, None, :, None, :])
    masked_cfg = np.full((1,) + flat.shape[1:], n_col - 1)
    flat = np.concatenate([flat, masked_cfg], axis=0).reshape(-1, 2 * GRID_W, NA_WIN_H * GRID_W)
    padded = jnp.concatenate([rpb.astype(F32) * LOG2E,
                              jnp.full(rpb.shape[:2] + (1,), NEG_BIG, F32)], axis=-1)
    return jnp.take(padded.reshape(-1), jnp.asarray(flat, jnp.int32))


def _layernorm(x, g, b):
    mu = jnp.mean(x, axis=1, keepdims=True)
    xc = x - mu
    var = jnp.mean(xc * xc, axis=1, keepdims=True)
    return xc * lax.rsqrt(var + LN_EPS) * g + b


def _route(logits, bias):
    aff = _sigmoid(logits)
    sel = aff + bias
    srow = [sel[e:e + 1] for e in range(N_EXPERTS)]
    arow = [aff[e:e + 1] for e in range(N_EXPERTS)]
    gscore = []
    for g in range(N_GROUPS):
        a0, a1, a2, a3 = srow[4 * g:4 * g + 4]
        hi01, lo01 = jnp.maximum(a0, a1), jnp.minimum(a0, a1)
        hi23, lo23 = jnp.maximum(a2, a3), jnp.minimum(a2, a3)
        top1 = jnp.maximum(hi01, hi23)
        top2 = jnp.maximum(jnp.minimum(hi01, hi23), jnp.maximum(lo01, lo23))
        gscore.append(top1 + top2)
    best = jnp.zeros_like(gscore[0])
    bestv = gscore[0]
    for g in range(1, N_GROUPS):
        better = gscore[g] > bestv
        best = jnp.where(better, float(g), best)
        bestv = jnp.where(better, gscore[g], bestv)

    def pick(rows_, i):
        out = rows_[i]
        for g in range(1, N_GROUPS):
            out = jnp.where(best == float(g), rows_[4 * g + i], out)
        return out

    s4 = [pick(srow, i) for i in range(4)]
    f4 = [pick(arow, i) for i in range(4)]
    chosen = []
    for i in range(4):
        rank = jnp.zeros_like(best)
        for j in range(4):
            if j == i:
                continue
            ahead = (s4[j] > s4[i]) | ((s4[j] == s4[i]) & (j < i))
            rank = rank + jnp.where(ahead, 1.0, 0.0)
        chosen.append(rank < 2.0)
    c0, c1, c2, c3 = chosen
    pidx = jnp.where(c0, jnp.where(c1, 0.0, jnp.where(c2, 1.0, 2.0)),
                     jnp.where(c1, jnp.where(c2, 3.0, 4.0), 5.0))
    a_lo = jnp.where(c0, f4[0], jnp.where(c1, f4[1], f4[2]))
    a_hi = jnp.where(c3, f4[3], jnp.where(c2, f4[2], f4[1]))
    den = a_lo + a_hi
    return best * float(N_PAIRS) + pidx, a_lo / den, a_hi / den


def _merge_kernel(x_ref, of_ref, ob_ref, gsg_ref, do_ref, no_ref, sg_ref, mod_ref, gg_ref,
                  wb_ref, wo_ref, lng_ref, lnb_ref, wr_ref, br_ref,
                  x1_ref, h2_ref, route_ref, cnt_ref, run_ref):
    first_step = jnp.logical_and(pl.program_id(0) == 0, pl.program_id(1) == 0)

    @pl.when(first_step)
    def _():
        run_ref[...] = jnp.zeros_like(run_ref)

    nb, tm = x_ref.shape[0], x_ref.shape[1]
    tiles = range(nb)
    mods = [mod_ref[bb, 0] for bb in tiles]
    gg = gg_ref[...]

    branches = []
    for bb in tiles:
        o = of_ref[bb] + ob_ref[bb]
        parts = []
        for h in range(GLA_HEADS):
            oh = o[:, h * GLA_DV:(h + 1) * GLA_DV]
            parts.append(oh * lax.rsqrt(jnp.mean(oh * oh, axis=1, keepdims=True) + RMS_EPS) * gg)
        gla = (jnp.concatenate(parts, axis=1) * gsg_ref[bb].astype(F32)).astype(BF16)
        branches.append((gla, do_ref[bb], no_ref[bb]))
    ys = [None] * nb
    for i in range(3):
        for bb in tiles:
            term = sg_ref[bb, :, i * D_MODEL:(i + 1) * D_MODEL].astype(F32) * _mm(branches[bb][i], wb_ref[0, i])
            ys[bb] = term if i == 0 else ys[bb] + term
    ys = [_mm(y.astype(BF16), wo_ref[0]) for y in ys]
    h2s = []
    for bb in tiles:
        x1 = _layernorm(DEEPNORM_ALPHA * x_ref[bb] + mods[bb][2:3] * ys[bb], lng_ref[...], lnb_ref[...])
        x1_ref[bb] = x1
        h2 = x1 * (1.0 + mods[bb][4:5]) + mods[bb][3:4]
        h2_ref[bb, :, :D_MODEL] = h2
        h2s.append(h2)
    logits = [lax.dot_general(wr_ref[...], h2, (((1,), (1,)), ((), ())),
                              precision=HIGHEST, preferred_element_type=F32) for h2 in h2s]
    routes = [_route(lg, br_ref[...]) for lg in logits]

    srows = lax.broadcasted_iota(jnp.int32, (SEG_ROWS, tm), 0).astype(F32)
    ii = lax.broadcasted_iota(jnp.int32, (tm, tm), 0)
    jj = lax.broadcasted_iota(jnp.int32, (tm, tm), 1)
    before = jnp.where(ii < jj, 1.0, 0.0).astype(BF16)
    onehots = [jnp.where(srows == seg, 1.0, 0.0) for seg, _, _ in routes]
    prefixes = [_mm(oh.astype(BF16), before) for oh in onehots]
    run = run_ref[...]
    for bb in tiles:
        seg, w_lo, w_hi = routes[bb]
        rank = jnp.sum(onehots[bb] * (prefixes[bb] + run[:, 0:1]), axis=0, keepdims=True)
        run = run + jnp.sum(onehots[bb], axis=1, keepdims=True)
        zrow = jnp.zeros_like(seg)
        route = jnp.concatenate([seg, rank, w_lo, w_hi, zrow, zrow, zrow, zrow], axis=0)
        route_ref[bb, 0] = route
        wide = jnp.concatenate([route, jnp.zeros((LANES - 8, tm), F32)], axis=0)
        h2_ref[bb, :, D_MODEL:] = wide.T
    run_ref[...] = run
    cnt_ref[...] = run


def _merge(xs, o_f, o_b, gsg, d_o, n_o, sg, mod2, gla_g, wb, wo, ln_g, ln_b, wr_t, br, layer):
    b, ntok, d = xs.shape
    nt = ntok // TOK_TILE
    nb = MERGE_BATCH if b % MERGE_BATCH == 0 else 1
    tok = lambda bi, ti: (bi, ti, 0)
    const2 = lambda bi, ti: (0, 0)

    def tokspec(wd):
        return pl.BlockSpec((nb, TOK_TILE, wd), tok)

    return pl.pallas_call(
        _merge_kernel,
        grid=(b // nb, nt),
        in_specs=[
            tokspec(d), tokspec(512), tokspec(512), tokspec(512), tokspec(512), tokspec(512),
            tokspec(3 * d),
            pl.BlockSpec((nb, 1, 8, d), lambda bi, ti: (bi, jnp.minimum(ti, 1), 0, 0)),
            pl.BlockSpec((1, GLA_DV), const2),
            pl.BlockSpec((1, 3, BRANCH_W, d), lambda bi, ti: (layer, 0, 0, 0)),
            pl.BlockSpec((1, d, d), lambda bi, ti: (layer, 0, 0)),
            pl.BlockSpec((1, d), const2),
            pl.BlockSpec((1, d), const2),
            pl.BlockSpec((N_EXPERTS, d), const2),
            pl.BlockSpec((N_EXPERTS, 1), const2),
        ],
        out_specs=[
            tokspec(d), tokspec(d + LANES),
            pl.BlockSpec((nb, 1, 8, TOK_TILE), lambda bi, ti: (bi, ti, 0, 0)),
            pl.BlockSpec((SEG_ROWS, LANES), const2),
        ],
        out_shape=[
            jax.ShapeDtypeStruct((b, ntok, d), F32),
            jax.ShapeDtypeStruct((b, ntok, d + LANES), F32),
            jax.ShapeDtypeStruct((b, nt, 8, TOK_TILE), F32),
            jax.ShapeDtypeStruct((SEG_ROWS, LANES), F32),
        ],
        scratch_shapes=[pltpu.VMEM((SEG_ROWS, LANES), F32)],
        compiler_params=_params("arbitrary", "arbitrary"),
        name="merge",
    )(xs, o_f, o_b, gsg, d_o, n_o, sg, mod2.reshape(b, 2, 8, d), gla_g.reshape(1, GLA_DV), wb, wo,
      ln_g.reshape(1, d), ln_b.reshape(1, d), wr_t, br.reshape(N_EXPERTS, 1))


def _plan_kernel(dest_ref, src_ref):
    def clear(j, carry):
        src_ref[j] = 0
        return carry
    lax.fori_loop(0, src_ref.shape[0], clear, 0, unroll=8)

    def place(j, carry):
        src_ref[dest_ref[j]] = j
        return carry
    lax.fori_loop(0, dest_ref.shape[0], place, 0, unroll=8)


def _plan(dest, tpad):
    grid_spec = pltpu.PrefetchScalarGridSpec(
        num_scalar_prefetch=1, grid=(1,), in_specs=[],
        out_specs=pl.BlockSpec(memory_space=pltpu.SMEM))
    return pl.pallas_call(
        _plan_kernel,
        grid_spec=grid_spec,
        out_shape=jax.ShapeDtypeStruct((tpad,), jnp.int32),
        compiler_params=_params("arbitrary"),
        name="plan",
    )(dest)


def _moe_kernel(src, meta, h_hbm, wg1, wu1, wd1, wg2, wu2, wd2, y_hbm,
                xbuf, ybuf, sem_in, sem_out):
    i = pl.program_id(0)
    ntile = pl.num_programs(0)
    n_used = meta[2 * ntile]
    slot = i % 2
    tm = MOE_TILE

    def n_real(tile):
        return meta[2 * ntile + 1 + tile]

    def gather(tile, sl):
        def body(j, carry):
            tok = src[tile * tm + j]
            pltpu.make_async_copy(h_hbm.at[pl.ds(tok, 1)], xbuf.at[sl, pl.ds(j, 1)], sem_in.at[sl]).start()
            return carry
        lax.fori_loop(0, tm, body, 0, unroll=8)

    def wait_in(sl):
        pltpu.make_async_copy(h_hbm.at[pl.ds(0, tm)], xbuf.at[sl], sem_in.at[sl]).wait()

    def row_out(tile, sl, j):
        tok = src[tile * tm + j]
        return pltpu.make_async_copy(ybuf.at[sl, pl.ds(j, 1)], y_hbm.at[pl.ds(tok, 1)], sem_out.at[sl])

    def scatter(tile, sl):
        nr = n_real(tile)

        def body(j, carry):
            row_out(tile, sl, j).start()
            return carry

        @pl.when(nr == tm)
        def _():
            for j in range(tm):
                row_out(tile, sl, j).start()

        @pl.when(nr < tm)
        def _():
            lax.fori_loop(0, nr, body, 0)

    def wait_out(tile, sl):
        nr = n_real(tile)

        @pl.when(nr == tm)
        def _():
            pltpu.make_async_copy(ybuf.at[sl], y_hbm.at[pl.ds(0, tm)], sem_out.at[sl]).wait()

        @pl.when(nr < tm)
        def _():
            def body(j, carry):
                row_out(tile, sl, j).wait()
                return carry
            lax.fori_loop(0, nr, body, 0)

    @pl.when(i == 0)
    def _():
        gather(0, 0)

    @pl.when(i < n_used)
    def _():
        wait_in(slot)

        @pl.when(i >= 2)
        def _():
            wait_out(i - 2, slot)

        x = xbuf[slot, :, :D_MODEL].astype(BF16)
        w = xbuf[slot, :, D_MODEL:]

        for j in range(tm):
            tok = src[(i + 1) * tm + j]
            pltpu.make_async_copy(h_hbm.at[pl.ds(tok, 1)], xbuf.at[1 - slot, pl.ds(j, 1)],
                                  sem_in.at[1 - slot]).start()

        a1, u1 = _mm(x, wg1[0]), _mm(x, wu1[0])
        a2, u2 = _mm(x, wg2[0]), _mm(x, wu2[0])
        y1 = _mm((a1 * _sigmoid(a1) * u1).astype(BF16), wd1[0])
        y2 = _mm((a2 * _sigmoid(a2) * u2).astype(BF16), wd2[0])
        ybuf[slot] = w[:, 2:3] * y1 + w[:, 3:4] * y2
        scatter(i, slot)

    @pl.when(i == ntile - 1)
    def _():
        wait_in(n_used % 2)
        wait_out(n_used - 1, (n_used - 1) % 2)

        @pl.when(n_used >= 2)
        def _():
            wait_out(n_used - 2, n_used % 2)


def _moe(h2, src, meta, wg, wu, wd, layer):
    t, dw = h2.shape
    d = dw - LANES
    ntile = src.shape[0] // MOE_TILE - 1
    lo = lambda i, s, m: (layer * N_EXPERTS + m[2 * i], 0, 0)
    hi = lambda i, s, m: (layer * N_EXPERTS + m[2 * i + 1], 0, 0)
    grid_spec = pltpu.PrefetchScalarGridSpec(
        num_scalar_prefetch=2,
        grid=(ntile,),
        in_specs=[
            pl.BlockSpec(memory_space=pl.ANY),
            pl.BlockSpec((1, d, D_EXPERT), lo), pl.BlockSpec((1, d, D_EXPERT), lo),
            pl.BlockSpec((1, D_EXPERT, d), lo),
            pl.BlockSpec((1, d, D_EXPERT), hi), pl.BlockSpec((1, d, D_EXPERT), hi),
            pl.BlockSpec((1, D_EXPERT, d), hi),
        ],
        out_specs=pl.BlockSpec(memory_space=pl.ANY),
        scratch_shapes=[pltpu.VMEM((2, MOE_TILE, dw), F32), pltpu.VMEM((2, MOE_TILE, d), F32),
                        pltpu.SemaphoreType.DMA((2,)), pltpu.SemaphoreType.DMA((2,))],
    )
    return pl.pallas_call(
        _moe_kernel,
        grid_spec=grid_spec,
        out_shape=jax.ShapeDtypeStruct((t, d), F32),
        compiler_params=_params("arbitrary"),
        name="moe",
    )(src, meta, h2, wg, wu, wd, wg, wu, wd)


def _lookup(idx, table):
    n = table.shape[0]
    hit = idx[:, None] == jnp.arange(n, dtype=jnp.int32)[None, :]
    return jnp.sum(jnp.where(hit, table[None, :], 0), axis=1)


def _dispatch_plan(route, counts, t):
    ntile = (t + N_SEG * (MOE_TILE - 1)) // MOE_TILE
    seg = route[:, :, 0, :].reshape(t).astype(jnp.int32)
    rank = route[:, :, 1, :].reshape(t).astype(jnp.int32)
    cnt = counts[:N_SEG, 0].astype(jnp.int32)
    seg_tiles = (cnt + MOE_TILE - 1) // MOE_TILE
    upto = jnp.arange(N_SEG)[None, :] <= jnp.arange(N_SEG)[:, None]
    tile_end = jnp.sum(jnp.where(upto, seg_tiles[None, :], 0), axis=1)
    first_tile = tile_end - seg_tiles
    dest = _lookup(seg, first_tile * MOE_TILE) + rank
    n_used = tile_end[-1]
    tiles = jnp.arange(ntile, dtype=jnp.int32)
    tile_seg = jnp.sum((tile_end[None, :] <= jnp.minimum(tiles, n_used - 1)[:, None]).astype(jnp.int32), axis=1)
    tile_seg = jnp.minimum(tile_seg, N_SEG - 1)
    base = (tile_seg // N_PAIRS) * EXPERTS_PER_GROUP
    pair = tile_seg % N_PAIRS
    e_lo = base + _lookup(pair, jnp.asarray(PAIR_LO, jnp.int32))
    e_hi = base + _lookup(pair, jnp.asarray(PAIR_HI, jnp.int32))
    left = _lookup(tile_seg, cnt) - (tiles - _lookup(tile_seg, first_tile)) * MOE_TILE
    n_real = jnp.where(tiles < n_used, jnp.clip(left, 0, MOE_TILE), 0)
    meta = jnp.concatenate([jnp.stack([e_lo, e_hi], axis=1).reshape(-1), n_used[None], n_real]).astype(jnp.int32)
    return _plan(dest, (ntile + 1) * MOE_TILE), meta


def _final_kernel(x_ref, y_ref, mod_ref, g_ref, b_ref, o_ref):
    mod = mod_ref[0]
    o_ref[0] = _layernorm(DEEPNORM_ALPHA * x_ref[0] + mod[5:6] * y_ref[...], g_ref[...], b_ref[...])


def _final(x1, y, mod2, ln_g, ln_b, latents_only):
    b, ntok, d = x1.shape
    nt = ntok // TOK_TILE
    skip = CTX_LEN // TOK_TILE if latents_only else 0
    return pl.pallas_call(
        _final_kernel,
        grid=(b, nt - skip),
        in_specs=[
            pl.BlockSpec((1, TOK_TILE, d), lambda bi, ti: (bi, ti + skip, 0)),
            pl.BlockSpec((TOK_TILE, d), lambda bi, ti: (bi * nt + ti + skip, 0)),
            pl.BlockSpec((1, 8, d), lambda bi, ti: (2 * bi + jnp.minimum(ti + skip, 1), 0, 0)),
            pl.BlockSpec((1, d), lambda bi, ti: (0, 0)),
            pl.BlockSpec((1, d), lambda bi, ti: (0, 0)),
        ],
        out_specs=pl.BlockSpec((1, TOK_TILE, d), lambda bi, ti: (bi, ti, 0)),
        out_shape=jax.ShapeDtypeStruct((b, ntok - skip * TOK_TILE, d), F32),
        compiler_params=_params("arbitrary", "arbitrary"),
        name="final_ln",
    )(x1, y, mod2, ln_g.reshape(1, d), ln_b.reshape(1, d))


def _rotary_order(w):
    lead = w.shape[:-1]
    half = DIFF_DH // 2
    return w.reshape(lead + (DIFF_HEADS, 2, 2, half)).swapaxes(-2, -3).reshape(lead + (DIFF_HEADS * 2 * DIFF_DH,))


def _pack_w_in(w_in):
    splits = np.cumsum([256, 256, 512, 512, 32, 512, 512, 512, 512, 512, 512])
    gq, gk, gv, gg, gr, dq, dk, dv, nq, nk, nv, sg = jnp.split(w_in.astype(BF16), splits, axis=-1)
    gr = jnp.pad(gr, ((0, 0), (0, 0), (0, LANES - 2 * GLA_RANK)))
    return jnp.concatenate([gq, gk, gv, gg, gr, _rotary_order(dq), _rotary_order(dk), dv, nq, nk, nv, sg],
                           axis=-1)


def _rope_tables(n_lat):
    t = jnp.arange(n_lat)
    row = (t // GRID_W).astype(F32)
    col = (t % GRID_W).astype(F32)
    n_freq = DIFF_DH // 4
    inv = ROPE_BASE ** (-jnp.arange(n_freq, dtype=F32) / n_freq)
    ang = jnp.concatenate([row[:, None] * inv, col[:, None] * inv], -1)
    cos, sin = jnp.cos(ang), jnp.sin(ang)
    cos_t = jnp.concatenate([cos] * 4, axis=1)
    sin_t = jnp.concatenate([-sin, -sin, sin, sin], axis=1)
    cos_t = jnp.concatenate([jnp.ones((CTX_LEN, LANES), F32), cos_t], axis=0)
    sin_t = jnp.concatenate([jnp.zeros((CTX_LEN, LANES), F32), sin_t], axis=0)
    return cos_t, sin_t


def _pack_decay(w_decay, b_decay):
    depth = w_decay.shape[0]
    wd = jnp.zeros((depth, 2, LANES, GLA_HEADS * GLA_DK), F32)
    for d in range(2):
        wd = wd.at[:, d, d * GLA_RANK:(d + 1) * GLA_RANK].set(w_decay[:, d])
    return wd.astype(BF16), b_decay.reshape(depth, 2, 1, GLA_HEADS * GLA_DK)


def kernel(x, c, ctx, c_ctx, w_ada, b_ada, w_in, gla_w_decay, gla_b_decay, gla_norm_g, diff_lam_q,
           diff_lam_k, diff_norm_g, na_rpb, w_branch, w_o, ln_g, ln_b, w_router, b_router,
           w_exp_gate, w_exp_up, w_exp_down):
    b, l, d = x.shape
    lc = ctx.shape[1]
    ntok = lc + l
    t = b * ntok

    w_in_p = _pack_w_in(w_in)
    wd_p, bd_p = _pack_decay(gla_w_decay, gla_b_decay)
    cos_t, sin_t = _rope_tables(l)
    wb = w_branch.astype(BF16)
    wo = w_o.astype(BF16)
    wg = w_exp_gate.astype(BF16).reshape(DEPTH * N_EXPERTS, d, D_EXPERT)
    wu = w_exp_up.astype(BF16).reshape(DEPTH * N_EXPERTS, d, D_EXPERT)
    wdn = w_exp_down.astype(BF16).reshape(DEPTH * N_EXPERTS, D_EXPERT, d)
    wr_t = w_router.T

    cs = jnp.concatenate([c, c_ctx[None], jnp.zeros((16 - b - 1, d), F32)], axis=0)
    mods = _ada(cs, w_ada, b_ada).reshape(DEPTH, 16, 6, d)

    xs = jnp.concatenate([ctx, x], axis=1)
    for layer in range(DEPTH):
        lam_init = 0.8 - 0.6 * math.exp(-0.3 * layer)
        m_lat = mods[layer, :b]
        m_ctx = jnp.broadcast_to(mods[layer, b][None], (b, 6, d))
        mod2 = jnp.stack([m_ctx, m_lat], axis=1).reshape(2 * b, 6, d)
        mod2 = jnp.pad(mod2, ((0, 0), (0, 2), (0, 0)))

        gq, gk, gr, gv, gsg, dq, dk, dv, nq, nk, nv, sg = _proj(xs, mod2, w_in_p, cos_t, sin_t, layer)
        o_f, o_b = _gla(gq, gk, gv, gr, wd_p, bd_p, layer)
        d_o = _diff(dq, dk, dv, diff_lam_q[layer], diff_lam_k[layer], diff_norm_g[layer], lam_init)
        n_o = _na(nq, nk, nv, _na_bias(na_rpb[layer]))
        x1, h2, route, counts = _merge(xs, o_f, o_b, gsg, d_o, n_o, sg, mod2, gla_norm_g[layer],
                                       wb, wo, ln_g[layer, 0], ln_b[layer, 0], wr_t, b_router, layer)
        src, meta = _dispatch_plan(route, counts, t)
        y = _moe(h2.reshape(t, d + LANES), src, meta, wg, wu, wdn, layer)
        xs = _final(x1, y, mod2, ln_g[layer, 1], ln_b[layer, 1], latents_only=layer == DEPTH - 1)
    return xs
```

```python
import functools
import itertools
import math

import jax
import jax.numpy as jnp
import numpy as np
from jax import lax
from jax.experimental import pallas as pl
from jax.experimental.pallas import tpu as pltpu

F32 = jnp.float32
BF16 = jnp.bfloat16
HIGHEST = lax.Precision.HIGHEST

D_MODEL = 1024
DEPTH = 4
GRID_W = 64
CTX_LEN = 256
BRANCH_W = D_MODEL // 2
GLA_HEADS = 4
GLA_DV = 128
GLA_DK = 64
GLA_RANK = 16
GLA_TAU = 16.0
GLA_CHUNK = 64
GLA_BATCH = 2
DIFF_HEADS = 4
DIFF_DV = 128
DIFF_DH = 64
DIFF_VROWS = DIFF_DV + 16
NA_HEADS = 8
NA_DH = 64
NA_WIN_H = 8
NA_WIN_W = 16
N_EXPERTS = 16
N_GROUPS = 4
EXPERTS_PER_GROUP = 4
D_EXPERT = D_MODEL // 2
ROPE_BASE = 10000.0
LN_EPS = 1e-5
RMS_EPS = 1e-6
NEG_BIG = -1e30
DEEPNORM_ALPHA = (2 * DEPTH) ** 0.25
LOG2E = 1.4426950408889634

LANES = 128
TOK_TILE = 256
NA_TILE = GRID_W
NA_BATCH = 2
MERGE_BATCH = 2
MOE_TILE = 256
N_PAIRS = 6
N_SEG = N_GROUPS * N_PAIRS
SEG_ROWS = 32
PAIR_LO = (0, 0, 0, 1, 1, 2)
PAIR_HI = (1, 2, 3, 2, 3, 3)
VMEM_LIMIT = 56 * 1024 * 1024

W_GQ, W_GK, W_GV, W_GG, W_GR = 0, 256, 512, 1024, 1536
W_DQ, W_DK, W_DV = 1664, 2176, 2688
W_NQ, W_NK, W_NV = 3200, 3712, 4224
W_SG, W_END = 4736, 7808


def _nt(a, b):
    return lax.dot_general(a, b, (((1,), (1,)), ((), ())), preferred_element_type=F32)


def _tn(a, b):
    return lax.dot_general(a, b, (((0,), (0,)), ((), ())), preferred_element_type=F32)


def _mm(a, b):
    return jnp.dot(a, b, preferred_element_type=F32)


def _sigmoid(x):
    return 1.0 / (1.0 + jnp.exp(-x))


def _params(*sem):
    return pltpu.CompilerParams(dimension_semantics=sem, vmem_limit_bytes=VMEM_LIMIT)


def _ada_kernel(c_ref, w_ref, b_ref, o_ref):
    cs = c_ref[...]
    s = cs * _sigmoid(cs)
    o_ref[0] = jnp.dot(s, w_ref[0], precision=HIGHEST, preferred_element_type=F32) + b_ref[0]


def _ada(cs, w_ada, b_ada):
    depth, d, n = w_ada.shape
    bn = 1536
    return pl.pallas_call(
        _ada_kernel,
        grid=(depth, n // bn),
        in_specs=[
            pl.BlockSpec((cs.shape[0], d), lambda l, j: (0, 0)),
            pl.BlockSpec((1, d, bn), lambda l, j: (l, 0, j)),
            pl.BlockSpec((1, 1, bn), lambda l, j: (l, 0, j)),
        ],
        out_specs=pl.BlockSpec((1, cs.shape[0], bn), lambda l, j: (l, 0, j)),
        out_shape=jax.ShapeDtypeStruct((depth, cs.shape[0], n), F32),
        compiler_params=_params("arbitrary", "arbitrary"),
        name="ada",
    )(cs, w_ada, b_ada.reshape(depth, 1, n))


def _proj_kernel(x_ref, mod_ref, w_ref, cos_ref, sin_ref,
                 gq, gk, gr, gv, gsg, dq, dk, dv, nq, nk, nv, sg):
    x = x_ref[0]
    mod = mod_ref[0]
    h = (x * (1.0 + mod[1:2]) + mod[0:1]).astype(BF16)

    def mm(lo, hi):
        return _mm(h, w_ref[0, :, lo:hi])

    gq[0] = mm(W_GQ, W_GK) * (GLA_DK ** -0.5)
    gk[0] = mm(W_GK, W_GV)
    gv[0] = mm(W_GV, W_GG).astype(BF16)
    g = mm(W_GG, W_GR)
    gsg[0] = (g * _sigmoid(g)).astype(BF16)
    gr[0] = mm(W_GR, W_DQ)

    cos = cos_ref[...]
    sin = sin_ref[...]

    def rope(y, scale):
        parts = []
        for i in range(DIFF_HEADS):
            p = y[:, i * LANES:(i + 1) * LANES]
            parts.append(((p * cos + pltpu.roll(p, LANES // 2, 1) * sin) * scale).astype(BF16))
        return jnp.concatenate(parts, axis=1)

    dq[0] = rope(mm(W_DQ, W_DK), (DIFF_DH ** -0.5) * LOG2E)
    dk[0] = rope(mm(W_DK, W_DV), 1.0)
    v = mm(W_DV, W_NQ)
    sub = lax.broadcasted_iota(jnp.int32, (DIFF_VROWS - DIFF_DV, v.shape[0]), 0)
    ones_rows = jnp.where(sub == 0, 1.0, 0.0).astype(BF16)
    for i in range(DIFF_HEADS):
        dv[0, i * DIFF_VROWS:i * DIFF_VROWS + DIFF_DV, :] = v[:, i * LANES:(i + 1) * LANES].T.astype(BF16)
        dv[0, i * DIFF_VROWS + DIFF_DV:(i + 1) * DIFF_VROWS, :] = ones_rows
    nq[0] = (mm(W_NQ, W_NK) * ((NA_DH ** -0.5) * LOG2E)).astype(BF16)
    nk[0] = mm(W_NK, W_NV).astype(BF16)
    nv[0] = mm(W_NV, W_SG).astype(BF16)
    for i in range(3):
        lo = W_SG + i * D_MODEL
        sg[0, :, i * D_MODEL:(i + 1) * D_MODEL] = _sigmoid(mm(lo, lo + D_MODEL)).astype(BF16)


def _proj(xs, mod2, w, cos_t, sin_t, layer):
    b, ntok, d = xs.shape
    nt = ntok // TOK_TILE
    widths = (256, 256, 128, 512, 512, 512, 512, None, 512, 512, 512, 3072)
    dtypes = (F32, F32, F32, BF16, BF16, BF16, BF16, BF16, BF16, BF16, BF16, BF16)
    tok = lambda bi, ti: (bi, ti, 0)
    vrows = DIFF_HEADS * DIFF_VROWS

    def out_spec(wd):
        if wd is None:
            return pl.BlockSpec((1, vrows, TOK_TILE), lambda bi, ti: (bi, 0, ti))
        return pl.BlockSpec((1, TOK_TILE, wd), tok)

    def out_shape(wd, dt):
        return jax.ShapeDtypeStruct((b, vrows, ntok) if wd is None else (b, ntok, wd), dt)

    return pl.pallas_call(
        _proj_kernel,
        grid=(b, nt),
        in_specs=[
            pl.BlockSpec((1, TOK_TILE, d), tok),
            pl.BlockSpec((1, 8, d), lambda bi, ti: (2 * bi + jnp.minimum(ti, 1), 0, 0)),
            pl.BlockSpec((1, d, W_END), lambda bi, ti: (layer, 0, 0), pipeline_mode=pl.Buffered(1)),
            pl.BlockSpec((TOK_TILE, LANES), lambda bi, ti: (ti, 0)),
            pl.BlockSpec((TOK_TILE, LANES), lambda bi, ti: (ti, 0)),
        ],
        out_specs=[out_spec(wd) for wd in widths],
        out_shape=[out_shape(wd, dt) for wd, dt in zip(widths, dtypes)],
        compiler_params=_params("arbitrary", "arbitrary"),
        name="proj",
    )(xs, mod2, w, cos_t, sin_t)


def _log_sigmoid(z):
    return -(jnp.maximum(-z, 0.0) + jnp.log(1.0 + jnp.exp(-jnp.abs(z))))


def _gla_kernel(qf, kf, vf, rf, qb, kb, vb, rb, wd_ref, bd_ref, of_ref, ob_ref, *st_refs):
    n = pl.program_id(1)

    @pl.when(n == 0)
    def _():
        for st_ref in st_refs:
            st_ref[...] = jnp.zeros_like(st_ref)

    c = GLA_CHUNK
    row = lax.broadcasted_iota(jnp.int32, (c, c), 0)
    col = lax.broadcasted_iota(jnp.int32, (c, c), 1)
    lane = lax.broadcasted_iota(jnp.int32, (1, LANES), 1)
    head_mask = (lane < GLA_DK, lane >= GLA_DK)

    dirs = ((qf, kf, vf, rf, of_ref), (qb, kb, vb, rb, ob_ref))
    chains = list(itertools.product(range(qf.shape[0]), range(2)))
    causal = [(row >= col), (row <= col)]
    tri = [jnp.where(cz, 1.0, 0.0).astype(F32) for cz in causal]
    pairs = range(GLA_HEADS // 2)
    psl = [slice(p * LANES, (p + 1) * LANES) for p in pairs]

    z = [_mm(dirs[d][3][bb].astype(BF16), wd_ref[0, d]) + bd_ref[0, d] for bb, d in chains]
    cum = [jnp.dot(tri[d], _log_sigmoid(zc) * (1.0 / GLA_TAU), precision=HIGHEST,
                   preferred_element_type=F32) for (bb, d), zc in zip(chains, z)]
    q_in, k_in, k_st, dec = [], [], [], []
    for (bb, d), cm in zip(chains, cum):
        last = cm[c - 1:c] if d == 0 else cm[0:1]
        k = dirs[d][1][bb]
        q_in.append(dirs[d][0][bb] * jnp.exp(cm))
        k_in.append((k * jnp.exp(-cm)).astype(BF16))
        k_st.append(k * jnp.exp(last - cm))
        dec.append(jnp.exp(last))
    att = []
    for ci in range(len(chains)):
        for p in pairs:
            q2 = q_in[ci][:, psl[p]]
            lhs = jnp.concatenate([jnp.where(head_mask[0], q2, 0.0),
                                   jnp.where(head_mask[1], q2, 0.0)], axis=0).astype(BF16)
            att.append(_nt(lhs, k_in[ci][:, psl[p]]))
    for ci, (bb, d) in enumerate(chains):
        v = dirs[d][2][bb]
        for p in pairs:
            q2b = q_in[ci][:, psl[p]].astype(BF16)
            for hh in range(2):
                h = 2 * p + hh
                a = jnp.where(causal[d], att[ci * len(pairs) + p][hh * c:(hh + 1) * c], 0.0).astype(BF16)
                st = st_refs[(bb * 2 + d) * GLA_HEADS + h][...]
                dirs[d][4][bb, :, h * GLA_DV:(h + 1) * GLA_DV] = (
                    _mm(a, v[:, h * GLA_DV:(h + 1) * GLA_DV]) + _nt(q2b, st.astype(BF16)))
    for ci, (bb, d) in enumerate(chains):
        v = dirs[d][2][bb]
        for p in pairs:
            for hh in range(2):
                h = 2 * p + hh
                st_ref = st_refs[(bb * 2 + d) * GLA_HEADS + h]
                kh = jnp.where(head_mask[hh], k_st[ci][:, psl[p]], 0.0).astype(BF16)
                st_ref[...] = st_ref[...] * dec[ci][:, psl[p]] + _tn(v[:, h * GLA_DV:(h + 1) * GLA_DV], kh)


def _gla(gq, gk, gv, gr, wd, bd, layer):
    b, ntok, _ = gq.shape
    nch = ntok // GLA_CHUNK
    nctx = CTX_LEN // GLA_CHUNK
    nb = GLA_BATCH if b % GLA_BATCH == 0 else 1

    def fwd(bi, n):
        return (bi, n, 0)

    def bwd(bi, n):
        return (bi, jnp.where(n < nctx, nctx - 1 - n, nch + nctx - 1 - n), 0)

    def specs(im):
        return [pl.BlockSpec((nb, GLA_CHUNK, 256), im), pl.BlockSpec((nb, GLA_CHUNK, 256), im),
                pl.BlockSpec((nb, GLA_CHUNK, 512), im), pl.BlockSpec((nb, GLA_CHUNK, 128), im)]

    return pl.pallas_call(
        _gla_kernel,
        grid=(b // nb, nch),
        in_specs=specs(fwd) + specs(bwd) + [
            pl.BlockSpec((1, 2, LANES, 256), lambda bi, n: (layer, 0, 0, 0)),
            pl.BlockSpec((1, 2, 1, 256), lambda bi, n: (layer, 0, 0, 0)),
        ],
        out_specs=[pl.BlockSpec((nb, GLA_CHUNK, 512), fwd), pl.BlockSpec((nb, GLA_CHUNK, 512), bwd)],
        out_shape=[jax.ShapeDtypeStruct((b, ntok, 512), F32)] * 2,
        scratch_shapes=[pltpu.VMEM((GLA_DV, LANES), F32)] * (nb * 2 * GLA_HEADS),
        compiler_params=_params("arbitrary", "arbitrary"),
        name="gla",
    )(gq, gk, gv, gr, gq, gk, gv, gr, wd, bd)


DIFF_KC = 256


def _diff_kernel(lam_init, q_ref, k_ref, v_ref, lq_ref, lk_ref, g_ref, o_ref, s_even, s_odd):
    t = pl.program_id(1)
    tq = q_ref.shape[1]
    nkc = k_ref.shape[1] // DIFF_KC
    lane = lax.broadcasted_iota(jnp.int32, (1, LANES), 1)
    comp0 = (lane // (DIFF_DH // 2)) % 2 == 0
    lql = lq_ref[...] * lk_ref[...]
    lam = (jnp.exp(jnp.sum(lql[0:1], axis=1, keepdims=True))
           - jnp.exp(jnp.sum(lql[1:2], axis=1, keepdims=True)) + lam_init)
    s_bufs = (s_even, s_odd)

    def stacked_q(h):
        q = q_ref[0, :, h * LANES:(h + 1) * LANES]
        zero = jnp.zeros_like(q)
        return jnp.concatenate([jnp.where(comp0, q, zero), jnp.where(comp0, zero, q)], axis=0)

    def score_chunk(h, qq, ci, m8):
        s = _nt(k_ref[0, ci * DIFF_KC:(ci + 1) * DIFF_KC, h * LANES:(h + 1) * LANES], qq)
        s_bufs[h % 2][ci] = s
        mc = jnp.max(s.reshape(DIFF_KC // 8, 8, 2 * tq), axis=0)
        return mc if m8 is None else jnp.maximum(m8, mc)

    def value_chunk(h, mrow, ci, acc):
        p = jnp.exp2(s_bufs[h % 2][ci] - mrow).astype(BF16)
        part = _mm(v_ref[0, h * DIFF_VROWS:(h + 1) * DIFF_VROWS, ci * DIFF_KC:(ci + 1) * DIFF_KC], p)
        return part if acc is None else acc + part

    def finish(h, acc):
        o = acc[:DIFF_DV] / acc[DIFF_DV:DIFF_DV + 1]
        o = o[:, :tq] - lam * o[:, tq:]
        o = o * lax.rsqrt(jnp.mean(o * o, axis=0, keepdims=True) + RMS_EPS) * g_ref[...] * (1.0 - lam_init)
        o_ref[0, :, h * DIFF_DV:(h + 1) * DIFF_DV] = o.T.astype(BF16)

    def attend(nch):
        qq = stacked_q(0)
        m8 = None
        for ci in range(nch):
            m8 = score_chunk(0, qq, ci, m8)
        for h in range(1, DIFF_HEADS + 1):
            mrow = jnp.max(m8, axis=0, keepdims=True)
            if h < DIFF_HEADS:
                qq = stacked_q(h)
            acc = m8 = None
            for ci in range(nch):
                if h < DIFF_HEADS:
                    m8 = score_chunk(h, qq, ci, m8)
                acc = value_chunk(h - 1, mrow, ci, acc)
            finish(h - 1, acc)

    @pl.when(t == 0)
    def _():
        attend(1)

    @pl.when(t > 0)
    def _():
        attend(nkc)


def _diff(dq, dk, dv, lam_q, lam_k, norm_g, lam_init):
    b, ntok, w = dq.shape
    nt = ntok // TOK_TILE
    scores = pltpu.VMEM((ntok // DIFF_KC, DIFF_KC, 2 * TOK_TILE), F32)
    return pl.pallas_call(
        functools.partial(_diff_kernel, lam_init),
        grid=(b, nt),
        in_specs=[
            pl.BlockSpec((1, TOK_TILE, w), lambda bi, t: (bi, t, 0)),
            pl.BlockSpec((1, ntok, w), lambda bi, t: (bi, 0, 0), pipeline_mode=pl.Buffered(1)),
            pl.BlockSpec((1, DIFF_HEADS * DIFF_VROWS, ntok), lambda bi, t: (bi, 0, 0),
                         pipeline_mode=pl.Buffered(1)),
            pl.BlockSpec((2, DIFF_DH), lambda bi, t: (0, 0)),
            pl.BlockSpec((2, DIFF_DH), lambda bi, t: (0, 0)),
            pl.BlockSpec((DIFF_DV, 1), lambda bi, t: (0, 0)),
        ],
        out_specs=pl.BlockSpec((1, TOK_TILE, w), lambda bi, t: (bi, t, 0)),
        out_shape=jax.ShapeDtypeStruct((b, ntok, w), BF16),
        scratch_shapes=[scores, scores],
        compiler_params=_params("arbitrary", "arbitrary"),
        name="diff",
    )(dq, dk, dv, lam_q, lam_k, norm_g.reshape(DIFF_DV, 1))


def _na_kernel(rows, q_ref, k_ref, v_ref, b_ref, o_ref):
    t = pl.program_id(1)
    nctx = CTX_LEN // NA_TILE
    r = t - nctx
    rs = jnp.clip(r - NA_WIN_H // 2, 0, rows - NA_WIN_H)
    start = pl.multiple_of(CTX_LEN + rs * GRID_W, GRID_W)
    nloc = NA_WIN_H * GRID_W
    lane = lax.broadcasted_iota(jnp.int32, (1, LANES), 1)
    first = lane < NA_DH
    chains = list(itertools.product(range(q_ref.shape[0]), range(NA_HEADS // 2)))
    psl = [slice(p * LANES, (p + 1) * LANES) for p in range(NA_HEADS // 2)]
    scores = []
    for bb, p in chains:
        q2 = q_ref[bb, :, psl[p]]
        zero = jnp.zeros_like(q2)
        lhs = jnp.concatenate([jnp.where(first, q2, zero), jnp.where(first, zero, q2)], axis=0)
        s_loc = _nt(lhs, k_ref[bb, pl.ds(start, nloc), psl[p]]) + b_ref[p]
        s_ctx = _nt(lhs, k_ref[bb, 0:CTX_LEN, psl[p]])
        scores.append((s_loc, s_ctx))
    probs = []
    for s_loc, s_ctx in scores:
        m = jnp.maximum(jnp.max(s_loc, axis=1, keepdims=True), jnp.max(s_ctx, axis=1, keepdims=True))
        p_loc = jnp.exp2(s_loc - m)
        p_ctx = jnp.exp2(s_ctx - m)
        l = jnp.sum(p_loc, axis=1, keepdims=True) + jnp.sum(p_ctx, axis=1, keepdims=True)
        probs.append((p_loc.astype(BF16), p_ctx.astype(BF16), l))
    for (bb, p), (p_loc, p_ctx, l) in zip(chains, probs):
        o = (_mm(p_loc, v_ref[bb, pl.ds(start, nloc), psl[p]]) + _mm(p_ctx, v_ref[bb, 0:CTX_LEN, psl[p]])) / l
        o_ref[bb, :, psl[p]] = jnp.where(first, o[:NA_TILE], o[NA_TILE:]).astype(BF16)


def _na(nq, nk, nv, bias):
    b, ntok, w = nq.shape
    nt = ntok // NA_TILE
    nctx = CTX_LEN // NA_TILE
    rows = (ntok - CTX_LEN) // GRID_W
    nb = NA_BATCH if b % NA_BATCH == 0 else 1

    def cfg(bi, t):
        r = t - nctx
        return (jnp.where(t < nctx, NA_WIN_H, r - jnp.clip(r - NA_WIN_H // 2, 0, rows - NA_WIN_H)), 0, 0)

    kv_spec = pl.BlockSpec((nb, ntok, w), lambda bi, t: (bi, 0, 0), pipeline_mode=pl.Buffered(1))
    return pl.pallas_call(
        functools.partial(_na_kernel, rows),
        grid=(b // nb, nt),
        in_specs=[
            pl.BlockSpec((nb, NA_TILE, w), lambda bi, t: (bi, t, 0)),
            kv_spec, kv_spec,
            pl.BlockSpec((NA_HEADS // 2, 2 * NA_TILE, NA_WIN_H * GRID_W), cfg),
        ],
        out_specs=pl.BlockSpec((nb, NA_TILE, w), lambda bi, t: (bi, t, 0)),
        out_shape=jax.ShapeDtypeStruct((b, ntok, w), BF16),
        compiler_params=_params("arbitrary", "arbitrary"),
        name="na",
    )(nq, nk, nv, bias)


def _na_bias(rpb):
    qc = np.arange(GRID_W)[:, None]
    kc = np.arange(GRID_W)[None, :]
    col_idx = np.clip(kc - qc + NA_WIN_W - 1, 0, 2 * NA_WIN_W - 2)
    cstart = np.clip(qc - NA_WIN_W // 2, 0, GRID_W - NA_WIN_W)
    win = (kc >= cstart) & (kc < cstart + NA_WIN_W)
    row_idx = np.arange(NA_WIN_H)[None, :] - np.arange(NA_WIN_H)[:, None] + (NA_WIN_H - 1)
    tbl = rpb.astype(F32)[:, row_idx][..., col_idx]
    tbl = jnp.where(win[None, None, None], tbl * LOG2E, NEG_BIG)
    tbl = tbl.transpose(1, 0, 3, 2, 4)
    tbl = tbl.reshape(NA_WIN_H, NA_HEADS // 2, 2 * GRID_W, NA_WIN_H * GRID_W)
    masked = jnp.full((1,) + tbl.shape[1:], NEG_BIG, F32)
    return jnp.concatenate([tbl, masked], axis=0).reshape(-1, 2 * GRID_W, NA_WIN_H * GRID_W)


def _layernorm(x, g, b):
    mu = jnp.mean(x, axis=1, keepdims=True)
    xc = x - mu
    var = jnp.mean(xc * xc, axis=1, keepdims=True)
    return xc * lax.rsqrt(var + LN_EPS) * g + b


def _route(logits, bias):
    aff = _sigmoid(logits)
    sel = aff + bias
    srow = [sel[e:e + 1] for e in range(N_EXPERTS)]
    arow = [aff[e:e + 1] for e in range(N_EXPERTS)]
    gscore = []
    for g in range(N_GROUPS):
        a0, a1, a2, a3 = srow[4 * g:4 * g + 4]
        hi01, lo01 = jnp.maximum(a0, a1), jnp.minimum(a0, a1)
        hi23, lo23 = jnp.maximum(a2, a3), jnp.minimum(a2, a3)
        top1 = jnp.maximum(hi01, hi23)
        top2 = jnp.maximum(jnp.minimum(hi01, hi23), jnp.maximum(lo01, lo23))
        gscore.append(top1 + top2)
    best = jnp.zeros_like(gscore[0])
    bestv = gscore[0]
    for g in range(1, N_GROUPS):
        better = gscore[g] > bestv
        best = jnp.where(better, float(g), best)
        bestv = jnp.where(better, gscore[g], bestv)

    def pick(rows_, i):
        out = rows_[i]
        for g in range(1, N_GROUPS):
            out = jnp.where(best == float(g), rows_[4 * g + i], out)
        return out

    s4 = [pick(srow, i) for i in range(4)]
    f4 = [pick(arow, i) for i in range(4)]
    chosen = []
    for i in range(4):
        rank = jnp.zeros_like(best)
        for j in range(4):
            if j == i:
                continue
            ahead = (s4[j] > s4[i]) | ((s4[j] == s4[i]) & (j < i))
            rank = rank + jnp.where(ahead, 1.0, 0.0)
        chosen.append(rank < 2.0)
    c0, c1, c2, c3 = chosen
    pidx = jnp.where(c0, jnp.where(c1, 0.0, jnp.where(c2, 1.0, 2.0)),
                     jnp.where(c1, jnp.where(c2, 3.0, 4.0), 5.0))
    a_lo = jnp.where(c0, f4[0], jnp.where(c1, f4[1], f4[2]))
    a_hi = jnp.where(c3, f4[3], jnp.where(c2, f4[2], f4[1]))
    den = a_lo + a_hi
    return best * float(N_PAIRS) + pidx, a_lo / den, a_hi / den


def _merge_kernel(x_ref, of_ref, ob_ref, gsg_ref, do_ref, no_ref, sg_ref, mod_ref, gg_ref,
                  wb_ref, wo_ref, lng_ref, lnb_ref, wr_ref, br_ref,
                  x1_ref, h2_ref, route_ref, cnt_ref, run_ref):
    first_step = jnp.logical_and(pl.program_id(0) == 0, pl.program_id(1) == 0)

    @pl.when(first_step)
    def _():
        run_ref[...] = jnp.zeros_like(run_ref)

    nb, tm = x_ref.shape[0], x_ref.shape[1]
    tiles = range(nb)
    mods = [mod_ref[bb, 0] for bb in tiles]
    gg = gg_ref[...]

    branches = []
    for bb in tiles:
        o = of_ref[bb] + ob_ref[bb]
        parts = []
        for h in range(GLA_HEADS):
            oh = o[:, h * GLA_DV:(h + 1) * GLA_DV]
            parts.append(oh * lax.rsqrt(jnp.mean(oh * oh, axis=1, keepdims=True) + RMS_EPS) * gg)
        gla = (jnp.concatenate(parts, axis=1) * gsg_ref[bb].astype(F32)).astype(BF16)
        branches.append((gla, do_ref[bb], no_ref[bb]))
    ys = [None] * nb
    for i in range(3):
        for bb in tiles:
            term = sg_ref[bb, :, i * D_MODEL:(i + 1) * D_MODEL].astype(F32) * _mm(branches[bb][i], wb_ref[0, i])
            ys[bb] = term if i == 0 else ys[bb] + term
    ys = [_mm(y.astype(BF16), wo_ref[0]) for y in ys]
    h2s = []
    for bb in tiles:
        x1 = _layernorm(DEEPNORM_ALPHA * x_ref[bb] + mods[bb][2:3] * ys[bb], lng_ref[...], lnb_ref[...])
        x1_ref[bb] = x1
        h2 = x1 * (1.0 + mods[bb][4:5]) + mods[bb][3:4]
        h2_ref[bb, :, :D_MODEL] = h2
        h2s.append(h2)
    logits = [lax.dot_general(wr_ref[...], h2, (((1,), (1,)), ((), ())),
                              precision=HIGHEST, preferred_element_type=F32) for h2 in h2s]
    routes = [_route(lg, br_ref[...]) for lg in logits]

    srows = lax.broadcasted_iota(jnp.int32, (SEG_ROWS, tm), 0).astype(F32)
    ii = lax.broadcasted_iota(jnp.int32, (tm, tm), 0)
    jj = lax.broadcasted_iota(jnp.int32, (tm, tm), 1)
    before = jnp.where(ii < jj, 1.0, 0.0).astype(BF16)
    onehots = [jnp.where(srows == seg, 1.0, 0.0) for seg, _, _ in routes]
    prefixes = [_mm(oh.astype(BF16), before) for oh in onehots]
    run = run_ref[...]
    for bb in tiles:
        seg, w_lo, w_hi = routes[bb]
        rank = jnp.sum(onehots[bb] * (prefixes[bb] + run[:, 0:1]), axis=0, keepdims=True)
        run = run + jnp.sum(onehots[bb], axis=1, keepdims=True)
        zrow = jnp.zeros_like(seg)
        route = jnp.concatenate([seg, rank, w_lo, w_hi, zrow, zrow, zrow, zrow], axis=0)
        route_ref[bb, 0] = route
        wide = jnp.concatenate([route, jnp.zeros((LANES - 8, tm), F32)], axis=0)
        h2_ref[bb, :, D_MODEL:] = wide.T
    run_ref[...] = run
    cnt_ref[...] = run


def _merge(xs, o_f, o_b, gsg, d_o, n_o, sg, mod2, gla_g, wb, wo, ln_g, ln_b, wr_t, br, layer):
    b, ntok, d = xs.shape
    nt = ntok // TOK_TILE
    nb = MERGE_BATCH if b % MERGE_BATCH == 0 else 1
    tok = lambda bi, ti: (bi, ti, 0)
    const2 = lambda bi, ti: (0, 0)

    def tokspec(wd):
        return pl.BlockSpec((nb, TOK_TILE, wd), tok)

    return pl.pallas_call(
        _merge_kernel,
        grid=(b // nb, nt),
        in_specs=[
            tokspec(d), tokspec(512), tokspec(512), tokspec(512), tokspec(512), tokspec(512),
            tokspec(3 * d),
            pl.BlockSpec((nb, 1, 8, d), lambda bi, ti: (bi, jnp.minimum(ti, 1), 0, 0)),
            pl.BlockSpec((1, GLA_DV), const2),
            pl.BlockSpec((1, 3, BRANCH_W, d), lambda bi, ti: (layer, 0, 0, 0)),
            pl.BlockSpec((1, d, d), lambda bi, ti: (layer, 0, 0)),
            pl.BlockSpec((1, d), const2),
            pl.BlockSpec((1, d), const2),
            pl.BlockSpec((N_EXPERTS, d), const2),
            pl.BlockSpec((N_EXPERTS, 1), const2),
        ],
        out_specs=[
            tokspec(d), tokspec(d + LANES),
            pl.BlockSpec((nb, 1, 8, TOK_TILE), lambda bi, ti: (bi, ti, 0, 0)),
            pl.BlockSpec((SEG_ROWS, LANES), const2),
        ],
        out_shape=[
            jax.ShapeDtypeStruct((b, ntok, d), F32),
            jax.ShapeDtypeStruct((b, ntok, d + LANES), F32),
            jax.ShapeDtypeStruct((b, nt, 8, TOK_TILE), F32),
            jax.ShapeDtypeStruct((SEG_ROWS, LANES), F32),
        ],
        scratch_shapes=[pltpu.VMEM((SEG_ROWS, LANES), F32)],
        compiler_params=_params("arbitrary", "arbitrary"),
        name="merge",
    )(xs, o_f, o_b, gsg, d_o, n_o, sg, mod2.reshape(b, 2, 8, d), gla_g.reshape(1, GLA_DV), wb, wo,
      ln_g.reshape(1, d), ln_b.reshape(1, d), wr_t, br.reshape(N_EXPERTS, 1))


def _plan_kernel(dest_ref, src_ref):
    def clear(j, carry):
        src_ref[j] = 0
        return carry
    lax.fori_loop(0, src_ref.shape[0], clear, 0, unroll=8)

    def place(j, carry):
        src_ref[dest_ref[j]] = j
        return carry
    lax.fori_loop(0, dest_ref.shape[0], place, 0, unroll=8)


def _plan(dest, tpad):
    grid_spec = pltpu.PrefetchScalarGridSpec(
        num_scalar_prefetch=1, grid=(1,), in_specs=[],
        out_specs=pl.BlockSpec(memory_space=pltpu.SMEM))
    return pl.pallas_call(
        _plan_kernel,
        grid_spec=grid_spec,
        out_shape=jax.ShapeDtypeStruct((tpad,), jnp.int32),
        compiler_params=_params("arbitrary"),
        name="plan",
    )(dest)


def _moe_kernel(src, meta, h_hbm, wg1, wu1, wd1, wg2, wu2, wd2, y_hbm,
                xbuf, ybuf, sem_in, sem_out):
    i = pl.program_id(0)
    ntile = pl.num_programs(0)
    n_used = meta[2 * ntile]
    slot = i % 2
    tm = MOE_TILE

    def n_real(tile):
        return meta[2 * ntile + 1 + tile]

    def gather(tile, sl):
        def body(j, carry):
            tok = src[tile * tm + j]
            pltpu.make_async_copy(h_hbm.at[pl.ds(tok, 1)], xbuf.at[sl, pl.ds(j, 1)], sem_in.at[sl]).start()
            return carry
        lax.fori_loop(0, tm, body, 0, unroll=8)

    def wait_in(sl):
        pltpu.make_async_copy(h_hbm.at[pl.ds(0, tm)], xbuf.at[sl], sem_in.at[sl]).wait()

    def row_out(tile, sl, j):
        tok = src[tile * tm + j]
        return pltpu.make_async_copy(ybuf.at[sl, pl.ds(j, 1)], y_hbm.at[pl.ds(tok, 1)], sem_out.at[sl])

    def scatter(tile, sl):
        nr = n_real(tile)

        def body(j, carry):
            row_out(tile, sl, j).start()
            return carry

        @pl.when(nr == tm)
        def _():
            for j in range(tm):
                row_out(tile, sl, j).start()

        @pl.when(nr < tm)
        def _():
            lax.fori_loop(0, nr, body, 0)

    def wait_out(tile, sl):
        nr = n_real(tile)

        @pl.when(nr == tm)
        def _():
            pltpu.make_async_copy(ybuf.at[sl], y_hbm.at[pl.ds(0, tm)], sem_out.at[sl]).wait()

        @pl.when(nr < tm)
        def _():
            def body(j, carry):
                row_out(tile, sl, j).wait()
                return carry
            lax.fori_loop(0, nr, body, 0)

    @pl.when(i == 0)
    def _():
        gather(0, 0)

    @pl.when(i < n_used)
    def _():
        wait_in(slot)

        @pl.when(i >= 2)
        def _():
            wait_out(i - 2, slot)

        x = xbuf[slot, :, :D_MODEL].astype(BF16)
        w = xbuf[slot, :, D_MODEL:]

        for j in range(tm):
            tok = src[(i + 1) * tm + j]
            pltpu.make_async_copy(h_hbm.at[pl.ds(tok, 1)], xbuf.at[1 - slot, pl.ds(j, 1)],
                                  sem_in.at[1 - slot]).start()

        a1, u1 = _mm(x, wg1[0]), _mm(x, wu1[0])
        a2, u2 = _mm(x, wg2[0]), _mm(x, wu2[0])
        y1 = _mm((a1 * _sigmoid(a1) * u1).astype(BF16), wd1[0])
        y2 = _mm((a2 * _sigmoid(a2) * u2).astype(BF16), wd2[0])
        ybuf[slot] = w[:, 2:3] * y1 + w[:, 3:4] * y2
        scatter(i, slot)

    @pl.when(i == ntile - 1)
    def _():
        wait_in(n_used % 2)
        wait_out(n_used - 1, (n_used - 1) % 2)

        @pl.when(n_used >= 2)
        def _():
            wait_out(n_used - 2, n_used % 2)


def _moe(h2, src, meta, wg, wu, wd, layer):
    t, dw = h2.shape
    d = dw - LANES
    ntile = src.shape[0] // MOE_TILE - 1
    lo = lambda i, s, m: (layer * N_EXPERTS + m[2 * i], 0, 0)
    hi = lambda i, s, m: (layer * N_EXPERTS + m[2 * i + 1], 0, 0)
    grid_spec = pltpu.PrefetchScalarGridSpec(
        num_scalar_prefetch=2,
        grid=(ntile,),
        in_specs=[
            pl.BlockSpec(memory_space=pl.ANY),
            pl.BlockSpec((1, d, D_EXPERT), lo), pl.BlockSpec((1, d, D_EXPERT), lo),
            pl.BlockSpec((1, D_EXPERT, d), lo),
            pl.BlockSpec((1, d, D_EXPERT), hi), pl.BlockSpec((1, d, D_EXPERT), hi),
            pl.BlockSpec((1, D_EXPERT, d), hi),
        ],
        out_specs=pl.BlockSpec(memory_space=pl.ANY),
        scratch_shapes=[pltpu.VMEM((2, MOE_TILE, dw), F32), pltpu.VMEM((2, MOE_TILE, d), F32),
                        pltpu.SemaphoreType.DMA((2,)), pltpu.SemaphoreType.DMA((2,))],
    )
    return pl.pallas_call(
        _moe_kernel,
        grid_spec=grid_spec,
        out_shape=jax.ShapeDtypeStruct((t, d), F32),
        compiler_params=_params("arbitrary"),
        name="moe",
    )(src, meta, h2, wg, wu, wd, wg, wu, wd)


def _lookup(idx, table):
    n = table.shape[0]
    hit = idx[:, None] == jnp.arange(n, dtype=jnp.int32)[None, :]
    return jnp.sum(jnp.where(hit, table[None, :], 0), axis=1)


def _dispatch_plan(route, counts, t):
    ntile = (t + N_SEG * (MOE_TILE - 1)) // MOE_TILE
    seg = route[:, :, 0, :].reshape(t).astype(jnp.int32)
    rank = route[:, :, 1, :].reshape(t).astype(jnp.int32)
    cnt = counts[:N_SEG, 0].astype(jnp.int32)
    seg_tiles = (cnt + MOE_TILE - 1) // MOE_TILE
    upto = jnp.arange(N_SEG)[None, :] <= jnp.arange(N_SEG)[:, None]
    tile_end = jnp.sum(jnp.where(upto, seg_tiles[None, :], 0), axis=1)
    first_tile = tile_end - seg_tiles
    dest = _lookup(seg, first_tile * MOE_TILE) + rank
    n_used = tile_end[-1]
    tiles = jnp.arange(ntile, dtype=jnp.int32)
    tile_seg = jnp.sum((tile_end[None, :] <= jnp.minimum(tiles, n_used - 1)[:, None]).astype(jnp.int32), axis=1)
    tile_seg = jnp.minimum(tile_seg, N_SEG - 1)
    base = (tile_seg // N_PAIRS) * EXPERTS_PER_GROUP
    pair = tile_seg % N_PAIRS
    e_lo = base + _lookup(pair, jnp.asarray(PAIR_LO, jnp.int32))
    e_hi = base + _lookup(pair, jnp.asarray(PAIR_HI, jnp.int32))
    left = _lookup(tile_seg, cnt) - (tiles - _lookup(tile_seg, first_tile)) * MOE_TILE
    n_real = jnp.where(tiles < n_used, jnp.clip(left, 0, MOE_TILE), 0)
    meta = jnp.concatenate([jnp.stack([e_lo, e_hi], axis=1).reshape(-1), n_used[None], n_real]).astype(jnp.int32)
    return _plan(dest, (ntile + 1) * MOE_TILE), meta


def _final_kernel(x_ref, y_ref, mod_ref, g_ref, b_ref, o_ref):
    mod = mod_ref[0]
    o_ref[0] = _layernorm(DEEPNORM_ALPHA * x_ref[0] + mod[5:6] * y_ref[...], g_ref[...], b_ref[...])


def _final(x1, y, mod2, ln_g, ln_b, latents_only):
    b, ntok, d = x1.shape
    nt = ntok // TOK_TILE
    skip = CTX_LEN // TOK_TILE if latents_only else 0
    return pl.pallas_call(
        _final_kernel,
        grid=(b, nt - skip),
        in_specs=[
            pl.BlockSpec((1, TOK_TILE, d), lambda bi, ti: (bi, ti + skip, 0)),
            pl.BlockSpec((TOK_TILE, d), lambda bi, ti: (bi * nt + ti + skip, 0)),
            pl.BlockSpec((1, 8, d), lambda bi, ti: (2 * bi + jnp.minimum(ti + skip, 1), 0, 0)),
            pl.BlockSpec((1, d), lambda bi, ti: (0, 0)),
            pl.BlockSpec((1, d), lambda bi, ti: (0, 0)),
        ],
        out_specs=pl.BlockSpec((1, TOK_TILE, d), lambda bi, ti: (bi, ti, 0)),
        out_shape=jax.ShapeDtypeStruct((b, ntok - skip * TOK_TILE, d), F32),
        compiler_params=_params("arbitrary", "arbitrary"),
        name="final_ln",
    )(x1, y, mod2, ln_g.reshape(1, d), ln_b.reshape(1, d))


def _rotary_order(w):
    lead = w.shape[:-1]
    half = DIFF_DH // 2
    return w.reshape(lead + (DIFF_HEADS, 2, 2, half)).swapaxes(-2, -3).reshape(lead + (DIFF_HEADS * 2 * DIFF_DH,))


def _pack_w_in(w_in):
    splits = np.cumsum([256, 256, 512, 512, 32, 512, 512, 512, 512, 512, 512])
    gq, gk, gv, gg, gr, dq, dk, dv, nq, nk, nv, sg = jnp.split(w_in.astype(BF16), splits, axis=-1)
    gr = jnp.pad(gr, ((0, 0), (0, 0), (0, LANES - 2 * GLA_RANK)))
    return jnp.concatenate([gq, gk, gv, gg, gr, _rotary_order(dq), _rotary_order(dk), dv, nq, nk, nv, sg],
                           axis=-1)


def _rope_tables(n_lat):
    t = jnp.arange(n_lat)
    row = (t // GRID_W).astype(F32)
    col = (t % GRID_W).astype(F32)
    n_freq = DIFF_DH // 4
    inv = ROPE_BASE ** (-jnp.arange(n_freq, dtype=F32) / n_freq)
    ang = jnp.concatenate([row[:, None] * inv, col[:, None] * inv], -1)
    cos, sin = jnp.cos(ang), jnp.sin(ang)
    cos_t = jnp.concatenate([cos] * 4, axis=1)
    sin_t = jnp.concatenate([-sin, -sin, sin, sin], axis=1)
    cos_t = jnp.concatenate([jnp.ones((CTX_LEN, LANES), F32), cos_t], axis=0)
    sin_t = jnp.concatenate([jnp.zeros((CTX_LEN, LANES), F32), sin_t], axis=0)
    return cos_t, sin_t


def _pack_decay(w_decay, b_decay):
    depth = w_decay.shape[0]
    wd = jnp.zeros((depth, 2, LANES, GLA_HEADS * GLA_DK), F32)
    for d in range(2):
        wd = wd.at[:, d, d * GLA_RANK:(d + 1) * GLA_RANK].set(w_decay[:, d])
    return wd.astype(BF16), b_decay.reshape(depth, 2, 1, GLA_HEADS * GLA_DK)


def kernel(x, c, ctx, c_ctx, w_ada, b_ada, w_in, gla_w_decay, gla_b_decay, gla_norm_g, diff_lam_q,
           diff_lam_k, diff_norm_g, na_rpb, w_branch, w_o, ln_g, ln_b, w_router, b_router,
           w_exp_gate, w_exp_up, w_exp_down):
    b, l, d = x.shape
    lc = ctx.shape[1]
    ntok = lc + l
    t = b * ntok

    w_in_p = _pack_w_in(w_in)
    wd_p, bd_p = _pack_decay(gla_w_decay, gla_b_decay)
    cos_t, sin_t = _rope_tables(l)
    wb = w_branch.astype(BF16)
    wo = w_o.astype(BF16)
    wg = w_exp_gate.astype(BF16).reshape(DEPTH * N_EXPERTS, d, D_EXPERT)
    wu = w_exp_up.astype(BF16).reshape(DEPTH * N_EXPERTS, d, D_EXPERT)
    wdn = w_exp_down.astype(BF16).reshape(DEPTH * N_EXPERTS, D_EXPERT, d)
    wr_t = w_router.T

    cs = jnp.concatenate([c, c_ctx[None], jnp.zeros((16 - b - 1, d), F32)], axis=0)
    mods = _ada(cs, w_ada, b_ada).reshape(DEPTH, 16, 6, d)

    xs = jnp.concatenate([ctx, x], axis=1)
    for layer in range(DEPTH):
        lam_init = 0.8 - 0.6 * math.exp(-0.3 * layer)
        m_lat = mods[layer, :b]
        m_ctx = jnp.broadcast_to(mods[layer, b][None], (b, 6, d))
        mod2 = jnp.stack([m_ctx, m_lat], axis=1).reshape(2 * b, 6, d)
        mod2 = jnp.pad(mod2, ((0, 0), (0, 2), (0, 0)))

        gq, gk, gr, gv, gsg, dq, dk, dv, nq, nk, nv, sg = _proj(xs, mod2, w_in_p, cos_t, sin_t, layer)
        o_f, o_b = _gla(gq, gk, gv, gr, wd_p, bd_p, layer)
        d_o = _diff(dq, dk, dv, diff_lam_q[layer], diff_lam_k[layer], diff_norm_g[layer], lam_init)
        n_o = _na(nq, nk, nv, _na_bias(na_rpb[layer]))
        x1, h2, route, counts = _merge(xs, o_f, o_b, gsg, d_o, n_o, sg, mod2, gla_norm_g[layer],
                                       wb, wo, ln_g[layer, 0], ln_b[layer, 0], wr_t, b_router, layer)
        src, meta = _dispatch_plan(route, counts, t)
        y = _moe(h2.reshape(t, d + LANES), src, meta, wg, wu, wdn, layer)
        xs = _final(x1, y, mod2, ln_g[layer, 1], ln_b[layer, 1], latents_only=layer == DEPTH - 1)
    return xs
```

```python
import functools
import itertools
import math

import jax
import jax.numpy as jnp
import numpy as np
from jax import lax
from jax.experimental import pallas as pl
from jax.experimental.pallas import tpu as pltpu

F32 = jnp.float32
BF16 = jnp.bfloat16
HIGHEST = lax.Precision.HIGHEST

D_MODEL = 1024
DEPTH = 4
GRID_W = 64
CTX_LEN = 256
BRANCH_W = D_MODEL // 2
GLA_HEADS = 4
GLA_DV = 128
GLA_DK = 64
GLA_RANK = 16
GLA_TAU = 16.0
GLA_CHUNK = 64
GLA_BATCH = 4
DIFF_HEADS = 4
DIFF_DV = 128
DIFF_DH = 64
DIFF_VROWS = DIFF_DV + 16
NA_HEADS = 8
NA_DH = 64
NA_WIN_H = 8
NA_WIN_W = 16
N_EXPERTS = 16
N_GROUPS = 4
EXPERTS_PER_GROUP = 4
D_EXPERT = D_MODEL // 2
ROPE_BASE = 10000.0
LN_EPS = 1e-5
RMS_EPS = 1e-6
NEG_BIG = -1e30
DEEPNORM_ALPHA = (2 * DEPTH) ** 0.25
LOG2E = 1.4426950408889634

LANES = 128
TOK_TILE = 256
NA_TILE = GRID_W
NA_BATCH = 2
MERGE_BATCH = 2
MOE_TILE = 256
N_PAIRS = 6
N_SEG = N_GROUPS * N_PAIRS
SEG_ROWS = 32
PAIR_LO = (0, 0, 0, 1, 1, 2)
PAIR_HI = (1, 2, 3, 2, 3, 3)
VMEM_LIMIT = 56 * 1024 * 1024

W_GQ, W_GK, W_GV, W_GG, W_GR = 0, 256, 512, 1024, 1536
W_DQ, W_DK, W_DV = 1664, 2176, 2688
W_NQ, W_NK, W_NV = 3200, 3712, 4224
W_SG, W_END = 4736, 7808


def _nt(a, b):
    return lax.dot_general(a, b, (((1,), (1,)), ((), ())), preferred_element_type=F32)


def _tn(a, b):
    return lax.dot_general(a, b, (((0,), (0,)), ((), ())), preferred_element_type=F32)


def _mm(a, b):
    return jnp.dot(a, b, preferred_element_type=F32)


def _sigmoid(x):
    return 1.0 / (1.0 + jnp.exp(-x))


def _params(*sem):
    return pltpu.CompilerParams(dimension_semantics=sem, vmem_limit_bytes=VMEM_LIMIT)


def _ada_kernel(c_ref, w_ref, b_ref, o_ref):
    cs = c_ref[...]
    s = cs * _sigmoid(cs)
    o_ref[0] = jnp.dot(s, w_ref[0], precision=HIGHEST, preferred_element_type=F32) + b_ref[0]


def _ada(cs, w_ada, b_ada):
    depth, d, n = w_ada.shape
    bn = 1536
    return pl.pallas_call(
        _ada_kernel,
        grid=(depth, n // bn),
        in_specs=[
            pl.BlockSpec((cs.shape[0], d), lambda l, j: (0, 0)),
            pl.BlockSpec((1, d, bn), lambda l, j: (l, 0, j)),
            pl.BlockSpec((1, 1, bn), lambda l, j: (l, 0, j)),
        ],
        out_specs=pl.BlockSpec((1, cs.shape[0], bn), lambda l, j: (l, 0, j)),
        out_shape=jax.ShapeDtypeStruct((depth, cs.shape[0], n), F32),
        compiler_params=_params("arbitrary", "arbitrary"),
        name="ada",
    )(cs, w_ada, b_ada.reshape(depth, 1, n))


def _proj_kernel(after_moe, *refs):
    if after_moe:
        (x1_ref, y_ref, pmod_ref, lng_ref, lnb_ref, mod_ref, w_ref, cos_ref, sin_ref, xs_ref,
         gq, gk, gr, gv, gsg, dq, dk, dv, nq, nk, nv, sg) = refs
        x = _layernorm(DEEPNORM_ALPHA * x1_ref[0] + pmod_ref[0][5:6] * y_ref[...], lng_ref[...], lnb_ref[...])
        xs_ref[0] = x
    else:
        x_ref, mod_ref, w_ref, cos_ref, sin_ref, gq, gk, gr, gv, gsg, dq, dk, dv, nq, nk, nv, sg = refs
        x = x_ref[0]
    mod = mod_ref[0]
    h = (x * (1.0 + mod[1:2]) + mod[0:1]).astype(BF16)

    def mm(lo, hi):
        return _mm(h, w_ref[0, :, lo:hi])

    gq[0] = mm(W_GQ, W_GK) * (GLA_DK ** -0.5)
    gk[0] = mm(W_GK, W_GV)
    gv[0] = mm(W_GV, W_GG).astype(BF16)
    g = mm(W_GG, W_GR)
    gsg[0] = (g * _sigmoid(g)).astype(BF16)
    gr[0] = mm(W_GR, W_DQ)

    cos = cos_ref[...]
    sin = sin_ref[...]

    def rope(y, scale):
        parts = []
        for i in range(DIFF_HEADS):
            p = y[:, i * LANES:(i + 1) * LANES]
            parts.append(((p * cos + pltpu.roll(p, LANES // 2, 1) * sin) * scale).astype(BF16))
        return jnp.concatenate(parts, axis=1)

    dq[0] = rope(mm(W_DQ, W_DK), (DIFF_DH ** -0.5) * LOG2E)
    dk[0] = rope(mm(W_DK, W_DV), 1.0)
    v = mm(W_DV, W_NQ)
    sub = lax.broadcasted_iota(jnp.int32, (DIFF_VROWS - DIFF_DV, v.shape[0]), 0)
    ones_rows = jnp.where(sub == 0, 1.0, 0.0).astype(BF16)
    for i in range(DIFF_HEADS):
        dv[0, i * DIFF_VROWS:i * DIFF_VROWS + DIFF_DV, :] = v[:, i * LANES:(i + 1) * LANES].T.astype(BF16)
        dv[0, i * DIFF_VROWS + DIFF_DV:(i + 1) * DIFF_VROWS, :] = ones_rows
    nq[0] = (mm(W_NQ, W_NK) * ((NA_DH ** -0.5) * LOG2E)).astype(BF16)
    nk[0] = mm(W_NK, W_NV).astype(BF16)
    nv[0] = mm(W_NV, W_SG).astype(BF16)
    for i in range(3):
        lo = W_SG + i * D_MODEL
        sg[0, :, i * D_MODEL:(i + 1) * D_MODEL] = _sigmoid(mm(lo, lo + D_MODEL)).astype(BF16)


def _proj(xs, mod2, w, cos_t, sin_t, layer, prev=None):
    b, ntok, d = (xs if prev is None else prev[0]).shape
    nt = ntok // TOK_TILE
    widths = (256, 256, 128, 512, 512, 512, 512, None, 512, 512, 512, 3072)
    dtypes = (F32, F32, F32, BF16, BF16, BF16, BF16, BF16, BF16, BF16, BF16, BF16)
    tok = lambda bi, ti: (bi, ti, 0)
    vrows = DIFF_HEADS * DIFF_VROWS

    def out_spec(wd):
        if wd is None:
            return pl.BlockSpec((1, vrows, TOK_TILE), lambda bi, ti: (bi, 0, ti))
        return pl.BlockSpec((1, TOK_TILE, wd), tok)

    def out_shape(wd, dt):
        return jax.ShapeDtypeStruct((b, vrows, ntok) if wd is None else (b, ntok, wd), dt)

    mod_spec = pl.BlockSpec((1, 8, d), lambda bi, ti: (2 * bi + jnp.minimum(ti, 1), 0, 0))
    vec_spec = pl.BlockSpec((1, d), lambda bi, ti: (0, 0))
    common_specs = [
        mod_spec,
        pl.BlockSpec((1, d, W_END), lambda bi, ti: (layer, 0, 0), pipeline_mode=pl.Buffered(1)),
        pl.BlockSpec((TOK_TILE, LANES), lambda bi, ti: (ti, 0)),
        pl.BlockSpec((TOK_TILE, LANES), lambda bi, ti: (ti, 0)),
    ]
    out_specs = [out_spec(wd) for wd in widths]
    out_shapes = [out_shape(wd, dt) for wd, dt in zip(widths, dtypes)]
    if prev is None:
        in_specs = [pl.BlockSpec((1, TOK_TILE, d), tok)] + common_specs
        args = (xs, mod2, w, cos_t, sin_t)
    else:
        x1, y, pmod2, ln_g, ln_b = prev
        in_specs = [pl.BlockSpec((1, TOK_TILE, d), tok),
                    pl.BlockSpec((TOK_TILE, d), lambda bi, ti: (bi * nt + ti, 0)),
                    mod_spec, vec_spec, vec_spec] + common_specs
        out_specs = [pl.BlockSpec((1, TOK_TILE, d), tok)] + out_specs
        out_shapes = [jax.ShapeDtypeStruct((b, ntok, d), F32)] + out_shapes
        args = (x1, y, pmod2, ln_g.reshape(1, d), ln_b.reshape(1, d), mod2, w, cos_t, sin_t)
    outs = pl.pallas_call(
        functools.partial(_proj_kernel, prev is not None),
        grid=(b, nt),
        in_specs=in_specs,
        out_specs=out_specs,
        out_shape=out_shapes,
        compiler_params=_params("arbitrary", "arbitrary"),
        name="proj",
    )(*args)
    return (xs, outs) if prev is None else (outs[0], outs[1:])


def _log_sigmoid(z):
    return -(jnp.maximum(-z, 0.0) + jnp.log(1.0 + jnp.exp(-jnp.abs(z))))


def _gla_kernel(qf, kf, vf, rf, qb, kb, vb, rb, wd_ref, bd_ref, of_ref, ob_ref, *st_refs):
    n = pl.program_id(1)

    @pl.when(n == 0)
    def _():
        for st_ref in st_refs:
            st_ref[...] = jnp.zeros_like(st_ref)

    c = GLA_CHUNK
    row = lax.broadcasted_iota(jnp.int32, (c, c), 0)
    col = lax.broadcasted_iota(jnp.int32, (c, c), 1)
    lane = lax.broadcasted_iota(jnp.int32, (1, LANES), 1)
    head_mask = (lane < GLA_DK, lane >= GLA_DK)

    dirs = ((qf, kf, vf, rf, of_ref), (qb, kb, vb, rb, ob_ref))
    chains = list(itertools.product(range(qf.shape[0]), range(2)))
    causal = [(row >= col), (row <= col)]
    tri = [jnp.where(cz, 1.0, 0.0).astype(F32) for cz in causal]
    pairs = range(GLA_HEADS // 2)
    psl = [slice(p * LANES, (p + 1) * LANES) for p in pairs]

    z = [_mm(dirs[d][3][bb].astype(BF16), wd_ref[0, d]) + bd_ref[0, d] for bb, d in chains]
    cum = [jnp.dot(tri[d], _log_sigmoid(zc) * (1.0 / GLA_TAU), precision=HIGHEST,
                   preferred_element_type=F32) for (bb, d), zc in zip(chains, z)]
    q_in, k_in, k_st, dec = [], [], [], []
    for (bb, d), cm in zip(chains, cum):
        last = cm[c - 1:c] if d == 0 else cm[0:1]
        k = dirs[d][1][bb]
        q_in.append(dirs[d][0][bb] * jnp.exp(cm))
        k_in.append((k * jnp.exp(-cm)).astype(BF16))
        k_st.append(k * jnp.exp(last - cm))
        dec.append(jnp.exp(last))
    att = []
    for ci in range(len(chains)):
        for p in pairs:
            q2 = q_in[ci][:, psl[p]]
            lhs = jnp.concatenate([jnp.where(head_mask[0], q2, 0.0),
                                   jnp.where(head_mask[1], q2, 0.0)], axis=0).astype(BF16)
            att.append(_nt(lhs, k_in[ci][:, psl[p]]))
    for ci, (bb, d) in enumerate(chains):
        v = dirs[d][2][bb]
        for p in pairs:
            q2b = q_in[ci][:, psl[p]].astype(BF16)
            for hh in range(2):
                h = 2 * p + hh
                a = jnp.where(causal[d], att[ci * len(pairs) + p][hh * c:(hh + 1) * c], 0.0).astype(BF16)
                st = st_refs[(bb * 2 + d) * GLA_HEADS + h][...]
                dirs[d][4][bb, :, h * GLA_DV:(h + 1) * GLA_DV] = (
                    _mm(a, v[:, h * GLA_DV:(h + 1) * GLA_DV]) + _nt(q2b, st.astype(BF16)))
    for ci, (bb, d) in enumerate(chains):
        v = dirs[d][2][bb]
        for p in pairs:
            for hh in range(2):
                h = 2 * p + hh
                st_ref = st_refs[(bb * 2 + d) * GLA_HEADS + h]
                kh = jnp.where(head_mask[hh], k_st[ci][:, psl[p]], 0.0).astype(BF16)
                st_ref[...] = st_ref[...] * dec[ci][:, psl[p]] + _tn(v[:, h * GLA_DV:(h + 1) * GLA_DV], kh)


def _gla(gq, gk, gv, gr, wd, bd, layer):
    b, ntok, _ = gq.shape
    nch = ntok // GLA_CHUNK
    nctx = CTX_LEN // GLA_CHUNK
    nb = GLA_BATCH if b % GLA_BATCH == 0 else 1

    def fwd(bi, n):
        return (bi, n, 0)

    def bwd(bi, n):
        return (bi, jnp.where(n < nctx, nctx - 1 - n, nch + nctx - 1 - n), 0)

    def specs(im):
        return [pl.BlockSpec((nb, GLA_CHUNK, 256), im), pl.BlockSpec((nb, GLA_CHUNK, 256), im),
                pl.BlockSpec((nb, GLA_CHUNK, 512), im), pl.BlockSpec((nb, GLA_CHUNK, 128), im)]

    return pl.pallas_call(
        _gla_kernel,
        grid=(b // nb, nch),
        in_specs=specs(fwd) + specs(bwd) + [
            pl.BlockSpec((1, 2, LANES, 256), lambda bi, n: (layer, 0, 0, 0)),
            pl.BlockSpec((1, 2, 1, 256), lambda bi, n: (layer, 0, 0, 0)),
        ],
        out_specs=[pl.BlockSpec((nb, GLA_CHUNK, 512), fwd), pl.BlockSpec((nb, GLA_CHUNK, 512), bwd)],
        out_shape=[jax.ShapeDtypeStruct((b, ntok, 512), F32)] * 2,
        scratch_shapes=[pltpu.VMEM((GLA_DV, LANES), F32)] * (nb * 2 * GLA_HEADS),
        compiler_params=_params("arbitrary", "arbitrary"),
        name="gla",
    )(gq, gk, gv, gr, gq, gk, gv, gr, wd, bd)


DIFF_KC = 256


def _diff_kernel(lam_init, q_ref, k_ref, v_ref, lq_ref, lk_ref, g_ref, o_ref, s_even, s_odd):
    t = pl.program_id(1)
    tq = q_ref.shape[1]
    nkc = k_ref.shape[1] // DIFF_KC
    lane = lax.broadcasted_iota(jnp.int32, (1, LANES), 1)
    comp0 = (lane // (DIFF_DH // 2)) % 2 == 0
    lql = lq_ref[...] * lk_ref[...]
    lam = (jnp.exp(jnp.sum(lql[0:1], axis=1, keepdims=True))
           - jnp.exp(jnp.sum(lql[1:2], axis=1, keepdims=True)) + lam_init)
    s_bufs = (s_even, s_odd)

    def stacked_q(h):
        q = q_ref[0, :, h * LANES:(h + 1) * LANES]
        zero = jnp.zeros_like(q)
        return jnp.concatenate([jnp.where(comp0, q, zero), jnp.where(comp0, zero, q)], axis=0)

    def score_chunk(h, qq, ci, m8):
        s = _nt(k_ref[0, ci * DIFF_KC:(ci + 1) * DIFF_KC, h * LANES:(h + 1) * LANES], qq)
        s_bufs[h % 2][ci] = s
        mc = jnp.max(s.reshape(DIFF_KC // 8, 8, 2 * tq), axis=0)
        return mc if m8 is None else jnp.maximum(m8, mc)

    def value_chunk(h, mrow, ci, acc):
        p = jnp.exp2(s_bufs[h % 2][ci] - mrow).astype(BF16)
        part = _mm(v_ref[0, h * DIFF_VROWS:(h + 1) * DIFF_VROWS, ci * DIFF_KC:(ci + 1) * DIFF_KC], p)
        return part if acc is None else acc + part

    def finish(h, acc):
        o = acc[:DIFF_DV] / acc[DIFF_DV:DIFF_DV + 1]
        o = o[:, :tq] - lam * o[:, tq:]
        o = o * lax.rsqrt(jnp.mean(o * o, axis=0, keepdims=True) + RMS_EPS) * g_ref[...] * (1.0 - lam_init)
        o_ref[0, :, h * DIFF_DV:(h + 1) * DIFF_DV] = o.T.astype(BF16)

    def attend(nch):
        qq = stacked_q(0)
        m8 = None
        for ci in range(nch):
            m8 = score_chunk(0, qq, ci, m8)
        for h in range(1, DIFF_HEADS + 1):
            mrow = jnp.max(m8, axis=0, keepdims=True)
            if h < DIFF_HEADS:
                qq = stacked_q(h)
            acc = m8 = None
            for ci in range(nch):
                if h < DIFF_HEADS:
                    m8 = score_chunk(h, qq, ci, m8)
                acc = value_chunk(h - 1, mrow, ci, acc)
            finish(h - 1, acc)

    @pl.when(t == 0)
    def _():
        attend(1)

    @pl.when(t > 0)
    def _():
        attend(nkc)


def _diff(dq, dk, dv, lam_q, lam_k, norm_g, lam_init):
    b, ntok, w = dq.shape
    nt = ntok // TOK_TILE
    scores = pltpu.VMEM((ntok // DIFF_KC, DIFF_KC, 2 * TOK_TILE), F32)
    return pl.pallas_call(
        functools.partial(_diff_kernel, lam_init),
        grid=(b, nt),
        in_specs=[
            pl.BlockSpec((1, TOK_TILE, w), lambda bi, t: (bi, t, 0)),
            pl.BlockSpec((1, ntok, w), lambda bi, t: (bi, 0, 0), pipeline_mode=pl.Buffered(1)),
            pl.BlockSpec((1, DIFF_HEADS * DIFF_VROWS, ntok), lambda bi, t: (bi, 0, 0),
                         pipeline_mode=pl.Buffered(1)),
            pl.BlockSpec((2, DIFF_DH), lambda bi, t: (0, 0)),
            pl.BlockSpec((2, DIFF_DH), lambda bi, t: (0, 0)),
            pl.BlockSpec((DIFF_DV, 1), lambda bi, t: (0, 0)),
        ],
        out_specs=pl.BlockSpec((1, TOK_TILE, w), lambda bi, t: (bi, t, 0)),
        out_shape=jax.ShapeDtypeStruct((b, ntok, w), BF16),
        scratch_shapes=[scores, scores],
        compiler_params=_params("arbitrary", "arbitrary"),
        name="diff",
    )(dq, dk, dv, lam_q, lam_k, norm_g.reshape(DIFF_DV, 1))


def _na_kernel(rows, q_ref, k_ref, v_ref, b_ref, o_ref):
    t = pl.program_id(1)
    nctx = CTX_LEN // NA_TILE
    r = t - nctx
    rs = jnp.clip(r - NA_WIN_H // 2, 0, rows - NA_WIN_H)
    start = pl.multiple_of(CTX_LEN + rs * GRID_W, GRID_W)
    nloc = NA_WIN_H * GRID_W
    lane = lax.broadcasted_iota(jnp.int32, (1, LANES), 1)
    first = lane < NA_DH
    chains = list(itertools.product(range(q_ref.shape[0]), range(NA_HEADS // 2)))
    psl = [slice(p * LANES, (p + 1) * LANES) for p in range(NA_HEADS // 2)]
    scores = []
    for bb, p in chains:
        q2 = q_ref[bb, :, psl[p]]
        zero = jnp.zeros_like(q2)
        lhs = jnp.concatenate([jnp.where(first, q2, zero), jnp.where(first, zero, q2)], axis=0)
        s_loc = _nt(lhs, k_ref[bb, pl.ds(start, nloc), psl[p]]) + b_ref[p]
        s_ctx = _nt(lhs, k_ref[bb, 0:CTX_LEN, psl[p]])
        scores.append((s_loc, s_ctx))
    probs = []
    for s_loc, s_ctx in scores:
        m = jnp.maximum(jnp.max(s_loc, axis=1, keepdims=True), jnp.max(s_ctx, axis=1, keepdims=True))
        p_loc = jnp.exp2(s_loc - m)
        p_ctx = jnp.exp2(s_ctx - m)
        l = jnp.sum(p_loc, axis=1, keepdims=True) + jnp.sum(p_ctx, axis=1, keepdims=True)
        probs.append((p_loc.astype(BF16), p_ctx.astype(BF16), l))
    for (bb, p), (p_loc, p_ctx, l) in zip(chains, probs):
        o = (_mm(p_loc, v_ref[bb, pl.ds(start, nloc), psl[p]]) + _mm(p_ctx, v_ref[bb, 0:CTX_LEN, psl[p]])) / l
        o_ref[bb, :, psl[p]] = jnp.where(first, o[:NA_TILE], o[NA_TILE:]).astype(BF16)


def _na(nq, nk, nv, bias):
    b, ntok, w = nq.shape
    nt = ntok // NA_TILE
    nctx = CTX_LEN // NA_TILE
    rows = (ntok - CTX_LEN) // GRID_W
    nb = NA_BATCH if b % NA_BATCH == 0 else 1

    def cfg(bi, t):
        r = t - nctx
        return (jnp.where(t < nctx, NA_WIN_H, r - jnp.clip(r - NA_WIN_H // 2, 0, rows - NA_WIN_H)), 0, 0)

    kv_spec = pl.BlockSpec((nb, ntok, w), lambda bi, t: (bi, 0, 0), pipeline_mode=pl.Buffered(1))
    return pl.pallas_call(
        functools.partial(_na_kernel, rows),
        grid=(b // nb, nt),
        in_specs=[
            pl.BlockSpec((nb, NA_TILE, w), lambda bi, t: (bi, t, 0)),
            kv_spec, kv_spec,
            pl.BlockSpec((NA_HEADS // 2, 2 * NA_TILE, NA_WIN_H * GRID_W), cfg),
        ],
        out_specs=pl.BlockSpec((nb, NA_TILE, w), lambda bi, t: (bi, t, 0)),
        out_shape=jax.ShapeDtypeStruct((b, ntok, w), BF16),
        compiler_params=_params("arbitrary", "arbitrary"),
        name="na",
    )(nq, nk, nv, bias)


def _na_bias(rpb):
    qc = np.arange(GRID_W)[:, None]
    kc = np.arange(GRID_W)[None, :]
    col_idx = np.clip(kc - qc + NA_WIN_W - 1, 0, 2 * NA_WIN_W - 2)
    cstart = np.clip(qc - NA_WIN_W // 2, 0, GRID_W - NA_WIN_W)
    win = (kc >= cstart) & (kc < cstart + NA_WIN_W)
    toep = jnp.where(win[None, None], rpb.astype(F32)[..., col_idx] * LOG2E, NEG_BIG)
    toep = toep.transpose(0, 2, 1, 3)
    cfgs = [toep[:, :, NA_WIN_H - 1 - c:2 * NA_WIN_H - 1 - c] for c in range(NA_WIN_H)]
    cfgs.append(jnp.full(cfgs[0].shape, NEG_BIG, F32))
    return jnp.stack(cfgs, axis=0).reshape(-1, 2 * GRID_W, NA_WIN_H * GRID_W)


def _layernorm(x, g, b):
    mu = jnp.mean(x, axis=1, keepdims=True)
    xc = x - mu
    var = jnp.mean(xc * xc, axis=1, keepdims=True)
    return xc * lax.rsqrt(var + LN_EPS) * g + b


def _route(logits, bias):
    aff = _sigmoid(logits)
    sel = aff + bias
    srow = [sel[e:e + 1] for e in range(N_EXPERTS)]
    arow = [aff[e:e + 1] for e in range(N_EXPERTS)]
    gscore = []
    for g in range(N_GROUPS):
        a0, a1, a2, a3 = srow[4 * g:4 * g + 4]
        hi01, lo01 = jnp.maximum(a0, a1), jnp.minimum(a0, a1)
        hi23, lo23 = jnp.maximum(a2, a3), jnp.minimum(a2, a3)
        top1 = jnp.maximum(hi01, hi23)
        top2 = jnp.maximum(jnp.minimum(hi01, hi23), jnp.maximum(lo01, lo23))
        gscore.append(top1 + top2)
    best = jnp.zeros_like(gscore[0])
    bestv = gscore[0]
    for g in range(1, N_GROUPS):
        better = gscore[g] > bestv
        best = jnp.where(better, float(g), best)
        bestv = jnp.where(better, gscore[g], bestv)

    def pick(rows_, i):
        out = rows_[i]
        for g in range(1, N_GROUPS):
            out = jnp.where(best == float(g), rows_[4 * g + i], out)
        return out

    s4 = [pick(srow, i) for i in range(4)]
    f4 = [pick(arow, i) for i in range(4)]
    chosen = []
    for i in range(4):
        rank = jnp.zeros_like(best)
        for j in range(4):
            if j == i:
                continue
            ahead = (s4[j] > s4[i]) | ((s4[j] == s4[i]) & (j < i))
            rank = rank + jnp.where(ahead, 1.0, 0.0)
        chosen.append(rank < 2.0)
    c0, c1, c2, c3 = chosen
    pidx = jnp.where(c0, jnp.where(c1, 0.0, jnp.where(c2, 1.0, 2.0)),
                     jnp.where(c1, jnp.where(c2, 3.0, 4.0), 5.0))
    a_lo = jnp.where(c0, f4[0], jnp.where(c1, f4[1], f4[2]))
    a_hi = jnp.where(c3, f4[3], jnp.where(c2, f4[2], f4[1]))
    den = a_lo + a_hi
    return best * float(N_PAIRS) + pidx, a_lo / den, a_hi / den


def _merge_kernel(x_ref, of_ref, ob_ref, gsg_ref, do_ref, no_ref, sg_ref, mod_ref, gg_ref,
                  wb_ref, wo_ref, lng_ref, lnb_ref, wr_ref, br_ref,
                  x1_ref, h2_ref, route_ref, cnt_ref, run_ref):
    first_step = jnp.logical_and(pl.program_id(0) == 0, pl.program_id(1) == 0)

    @pl.when(first_step)
    def _():
        run_ref[...] = jnp.zeros_like(run_ref)

    nb, tm = x_ref.shape[0], x_ref.shape[1]
    tiles = range(nb)
    mods = [mod_ref[bb, 0] for bb in tiles]
    gg = gg_ref[...]

    branches = []
    for bb in tiles:
        o = of_ref[bb] + ob_ref[bb]
        parts = []
        for h in range(GLA_HEADS):
            oh = o[:, h * GLA_DV:(h + 1) * GLA_DV]
            parts.append(oh * lax.rsqrt(jnp.mean(oh * oh, axis=1, keepdims=True) + RMS_EPS) * gg)
        gla = (jnp.concatenate(parts, axis=1) * gsg_ref[bb].astype(F32)).astype(BF16)
        branches.append((gla, do_ref[bb], no_ref[bb]))
    ys = [None] * nb
    for i in range(3):
        for bb in tiles:
            term = sg_ref[bb, :, i * D_MODEL:(i + 1) * D_MODEL].astype(F32) * _mm(branches[bb][i], wb_ref[0, i])
            ys[bb] = term if i == 0 else ys[bb] + term
    ys = [_mm(y.astype(BF16), wo_ref[0]) for y in ys]
    h2s = []
    for bb in tiles:
        x1 = _layernorm(DEEPNORM_ALPHA * x_ref[bb] + mods[bb][2:3] * ys[bb], lng_ref[...], lnb_ref[...])
        x1_ref[bb] = x1
        h2 = x1 * (1.0 + mods[bb][4:5]) + mods[bb][3:4]
        h2_ref[bb, :, :D_MODEL] = h2
        h2s.append(h2)
    logits = [lax.dot_general(wr_ref[...], h2, (((1,), (1,)), ((), ())),
                              precision=HIGHEST, preferred_element_type=F32) for h2 in h2s]
    routes = [_route(lg, br_ref[...]) for lg in logits]

    srows = lax.broadcasted_iota(jnp.int32, (SEG_ROWS, tm), 0).astype(F32)
    ii = lax.broadcasted_iota(jnp.int32, (tm, tm), 0)
    jj = lax.broadcasted_iota(jnp.int32, (tm, tm), 1)
    before = jnp.where(ii < jj, 1.0, 0.0).astype(BF16)
    onehots = [jnp.where(srows == seg, 1.0, 0.0) for seg, _, _ in routes]
    prefixes = [_mm(oh.astype(BF16), before) for oh in onehots]
    run = run_ref[...]
    for bb in tiles:
        seg, w_lo, w_hi = routes[bb]
        rank = jnp.sum(onehots[bb] * (prefixes[bb] + run[:, 0:1]), axis=0, keepdims=True)
        run = run + jnp.sum(onehots[bb], axis=1, keepdims=True)
        zrow = jnp.zeros_like(seg)
        route = jnp.concatenate([seg, rank, w_lo, w_hi, zrow, zrow, zrow, zrow], axis=0)
        route_ref[bb, 0] = route
        wide = jnp.concatenate([route, jnp.zeros((LANES - 8, tm), F32)], axis=0)
        h2_ref[bb, :, D_MODEL:] = wide.T
    run_ref[...] = run
    cnt_ref[...] = run


def _merge(xs, o_f, o_b, gsg, d_o, n_o, sg, mod2, gla_g, wb, wo, ln_g, ln_b, wr_t, br, layer):
    b, ntok, d = xs.shape
    nt = ntok // TOK_TILE
    nb = MERGE_BATCH if b % MERGE_BATCH == 0 else 1
    tok = lambda bi, ti: (bi, ti, 0)
    const2 = lambda bi, ti: (0, 0)

    def tokspec(wd):
        return pl.BlockSpec((nb, TOK_TILE, wd), tok)

    return pl.pallas_call(
        _merge_kernel,
        grid=(b // nb, nt),
        in_specs=[
            tokspec(d), tokspec(512), tokspec(512), tokspec(512), tokspec(512), tokspec(512),
            tokspec(3 * d),
            pl.BlockSpec((nb, 1, 8, d), lambda bi, ti: (bi, jnp.minimum(ti, 1), 0, 0)),
            pl.BlockSpec((1, GLA_DV), const2),
            pl.BlockSpec((1, 3, BRANCH_W, d), lambda bi, ti: (layer, 0, 0, 0)),
            pl.BlockSpec((1, d, d), lambda bi, ti: (layer, 0, 0)),
            pl.BlockSpec((1, d), const2),
            pl.BlockSpec((1, d), const2),
            pl.BlockSpec((N_EXPERTS, d), const2),
            pl.BlockSpec((N_EXPERTS, 1), const2),
        ],
        out_specs=[
            tokspec(d), tokspec(d + LANES),
            pl.BlockSpec((nb, 1, 8, TOK_TILE), lambda bi, ti: (bi, ti, 0, 0)),
            pl.BlockSpec((SEG_ROWS, LANES), const2),
        ],
        out_shape=[
            jax.ShapeDtypeStruct((b, ntok, d), F32),
            jax.ShapeDtypeStruct((b, ntok, d + LANES), F32),
            jax.ShapeDtypeStruct((b, nt, 8, TOK_TILE), F32),
            jax.ShapeDtypeStruct((SEG_ROWS, LANES), F32),
        ],
        scratch_shapes=[pltpu.VMEM((SEG_ROWS, LANES), F32)],
        compiler_params=_params("arbitrary", "arbitrary"),
        name="merge",
    )(xs, o_f, o_b, gsg, d_o, n_o, sg, mod2.reshape(b, 2, 8, d), gla_g.reshape(1, GLA_DV), wb, wo,
      ln_g.reshape(1, d), ln_b.reshape(1, d), wr_t, br.reshape(N_EXPERTS, 1))


def _plan_kernel(dest_ref, src_ref):
    def clear(j, carry):
        src_ref[j] = 0
        return carry
    lax.fori_loop(0, src_ref.shape[0], clear, 0, unroll=8)

    def place(j, carry):
        src_ref[dest_ref[j]] = j
        return carry
    lax.fori_loop(0, dest_ref.shape[0], place, 0, unroll=8)


def _plan(dest, tpad):
    grid_spec = pltpu.PrefetchScalarGridSpec(
        num_scalar_prefetch=1, grid=(1,), in_specs=[],
        out_specs=pl.BlockSpec(memory_space=pltpu.SMEM))
    return pl.pallas_call(
        _plan_kernel,
        grid_spec=grid_spec,
        out_shape=jax.ShapeDtypeStruct((tpad,), jnp.int32),
        compiler_params=_params("arbitrary"),
        name="plan",
    )(dest)


def _moe_kernel(src, meta, h_hbm, wg1, wu1, wd1, wg2, wu2, wd2, y_hbm,
                xbuf, ybuf, sem_in, sem_out):
    i = pl.program_id(0)
    ntile = pl.num_programs(0)
    n_used = meta[2 * ntile]
    slot = i % 2
    tm = MOE_TILE

    def n_real(tile):
        return meta[2 * ntile + 1 + tile]

    def gather(tile, sl):
        def body(j, carry):
            tok = src[tile * tm + j]
            pltpu.make_async_copy(h_hbm.at[pl.ds(tok, 1)], xbuf.at[sl, pl.ds(j, 1)], sem_in.at[sl]).start()
            return carry
        lax.fori_loop(0, tm, body, 0, unroll=8)

    def wait_in(sl):
        pltpu.make_async_copy(h_hbm.at[pl.ds(0, tm)], xbuf.at[sl], sem_in.at[sl]).wait()

    def row_out(tile, sl, j):
        tok = src[tile * tm + j]
        return pltpu.make_async_copy(ybuf.at[sl, pl.ds(j, 1)], y_hbm.at[pl.ds(tok, 1)], sem_out.at[sl])

    def scatter(tile, sl):
        nr = n_real(tile)

        def body(j, carry):
            row_out(tile, sl, j).start()
            return carry

        @pl.when(nr == tm)
        def _():
            for j in range(tm):
                row_out(tile, sl, j).start()

        @pl.when(nr < tm)
        def _():
            lax.fori_loop(0, nr, body, 0)

    def wait_out(tile, sl):
        nr = n_real(tile)

        @pl.when(nr == tm)
        def _():
            pltpu.make_async_copy(ybuf.at[sl], y_hbm.at[pl.ds(0, tm)], sem_out.at[sl]).wait()

        @pl.when(nr < tm)
        def _():
            def body(j, carry):
                row_out(tile, sl, j).wait()
                return carry
            lax.fori_loop(0, nr, body, 0)

    @pl.when(i == 0)
    def _():
        gather(0, 0)

    @pl.when(i < n_used)
    def _():
        wait_in(slot)

        @pl.when(i >= 2)
        def _():
            wait_out(i - 2, slot)

        for j in range(tm):
            tok = src[(i + 1) * tm + j]
            pltpu.make_async_copy(h_hbm.at[pl.ds(tok, 1)], xbuf.at[1 - slot, pl.ds(j, 1)],
                                  sem_in.at[1 - slot]).start()

        x = xbuf[slot, :, :D_MODEL].astype(BF16)
        w = xbuf[slot, :, D_MODEL:]

        a1, u1 = _mm(x, wg1[0]), _mm(x, wu1[0])
        a2, u2 = _mm(x, wg2[0]), _mm(x, wu2[0])
        y1 = _mm((a1 * _sigmoid(a1) * u1).astype(BF16), wd1[0])
        y2 = _mm((a2 * _sigmoid(a2) * u2).astype(BF16), wd2[0])
        ybuf[slot] = w[:, 2:3] * y1 + w[:, 3:4] * y2
        scatter(i, slot)

    @pl.when(i == ntile - 1)
    def _():
        wait_in(n_used % 2)
        wait_out(n_used - 1, (n_used - 1) % 2)

        @pl.when(n_used >= 2)
        def _():
            wait_out(n_used - 2, n_used % 2)


def _moe(h2, src, meta, wg, wu, wd, layer):
    t, dw = h2.shape
    d = dw - LANES
    ntile = src.shape[0] // MOE_TILE - 1
    lo = lambda i, s, m: (layer * N_EXPERTS + m[2 * i], 0, 0)
    hi = lambda i, s, m: (layer * N_EXPERTS + m[2 * i + 1], 0, 0)
    grid_spec = pltpu.PrefetchScalarGridSpec(
        num_scalar_prefetch=2,
        grid=(ntile,),
        in_specs=[
            pl.BlockSpec(memory_space=pl.ANY),
            pl.BlockSpec((1, d, D_EXPERT), lo), pl.BlockSpec((1, d, D_EXPERT), lo),
            pl.BlockSpec((1, D_EXPERT, d), lo),
            pl.BlockSpec((1, d, D_EXPERT), hi), pl.BlockSpec((1, d, D_EXPERT), hi),
            pl.BlockSpec((1, D_EXPERT, d), hi),
        ],
        out_specs=pl.BlockSpec(memory_space=pl.ANY),
        scratch_shapes=[pltpu.VMEM((2, MOE_TILE, dw), F32), pltpu.VMEM((2, MOE_TILE, d), F32),
                        pltpu.SemaphoreType.DMA((2,)), pltpu.SemaphoreType.DMA((2,))],
    )
    return pl.pallas_call(
        _moe_kernel,
        grid_spec=grid_spec,
        out_shape=jax.ShapeDtypeStruct((t, d), F32),
        compiler_params=_params("arbitrary"),
        name="moe",
    )(src, meta, h2, wg, wu, wd, wg, wu, wd)


def _lookup(idx, table):
    n = table.shape[0]
    hit = idx[:, None] == jnp.arange(n, dtype=jnp.int32)[None, :]
    return jnp.sum(jnp.where(hit, table[None, :], 0), axis=1)


def _dispatch_plan(route, counts, t):
    ntile = (t + N_SEG * (MOE_TILE - 1)) // MOE_TILE
    seg = route[:, :, 0, :].reshape(t).astype(jnp.int32)
    rank = route[:, :, 1, :].reshape(t).astype(jnp.int32)
    cnt = counts[:N_SEG, 0].astype(jnp.int32)
    seg_tiles = (cnt + MOE_TILE - 1) // MOE_TILE
    upto = jnp.arange(N_SEG)[None, :] <= jnp.arange(N_SEG)[:, None]
    tile_end = jnp.sum(jnp.where(upto, seg_tiles[None, :], 0), axis=1)
    first_tile = tile_end - seg_tiles
    dest = _lookup(seg, first_tile * MOE_TILE) + rank
    n_used = tile_end[-1]
    tiles = jnp.arange(ntile, dtype=jnp.int32)
    tile_seg = jnp.sum((tile_end[None, :] <= jnp.minimum(tiles, n_used - 1)[:, None]).astype(jnp.int32), axis=1)
    tile_seg = jnp.minimum(tile_seg, N_SEG - 1)
    base = (tile_seg // N_PAIRS) * EXPERTS_PER_GROUP
    pair = tile_seg % N_PAIRS
    e_lo = base + _lookup(pair, jnp.asarray(PAIR_LO, jnp.int32))
    e_hi = base + _lookup(pair, jnp.asarray(PAIR_HI, jnp.int32))
    left = _lookup(tile_seg, cnt) - (tiles - _lookup(tile_seg, first_tile)) * MOE_TILE
    n_real = jnp.where(tiles < n_used, jnp.clip(left, 0, MOE_TILE), 0)
    meta = jnp.concatenate([jnp.stack([e_lo, e_hi], axis=1).reshape(-1), n_used[None], n_real]).astype(jnp.int32)
    return _plan(dest, (ntile + 1) * MOE_TILE), meta


def _final_kernel(x_ref, y_ref, mod_ref, g_ref, b_ref, o_ref):
    mod = mod_ref[0]
    o_ref[0] = _layernorm(DEEPNORM_ALPHA * x_ref[0] + mod[5:6] * y_ref[...], g_ref[...], b_ref[...])


def _final(x1, y, mod2, ln_g, ln_b, latents_only):
    b, ntok, d = x1.shape
    nt = ntok // TOK_TILE
    skip = CTX_LEN // TOK_TILE if latents_only else 0
    return pl.pallas_call(
        _final_kernel,
        grid=(b, nt - skip),
        in_specs=[
            pl.BlockSpec((1, TOK_TILE, d), lambda bi, ti: (bi, ti + skip, 0)),
            pl.BlockSpec((TOK_TILE, d), lambda bi, ti: (bi * nt + ti + skip, 0)),
            pl.BlockSpec((1, 8, d), lambda bi, ti: (2 * bi + jnp.minimum(ti + skip, 1), 0, 0)),
            pl.BlockSpec((1, d), lambda bi, ti: (0, 0)),
            pl.BlockSpec((1, d), lambda bi, ti: (0, 0)),
        ],
        out_specs=pl.BlockSpec((1, TOK_TILE, d), lambda bi, ti: (bi, ti, 0)),
        out_shape=jax.ShapeDtypeStruct((b, ntok - skip * TOK_TILE, d), F32),
        compiler_params=_params("arbitrary", "arbitrary"),
        name="final_ln",
    )(x1, y, mod2, ln_g.reshape(1, d), ln_b.reshape(1, d))


def _rotary_order(w):
    lead = w.shape[:-1]
    half = DIFF_DH // 2
    return w.reshape(lead + (DIFF_HEADS, 2, 2, half)).swapaxes(-2, -3).reshape(lead + (DIFF_HEADS * 2 * DIFF_DH,))


def _pack_w_in(w_in):
    splits = np.cumsum([256, 256, 512, 512, 32, 512, 512, 512, 512, 512, 512])
    gq, gk, gv, gg, gr, dq, dk, dv, nq, nk, nv, sg = jnp.split(w_in.astype(BF16), splits, axis=-1)
    gr = jnp.pad(gr, ((0, 0), (0, 0), (0, LANES - 2 * GLA_RANK)))
    return jnp.concatenate([gq, gk, gv, gg, gr, _rotary_order(dq), _rotary_order(dk), dv, nq, nk, nv, sg],
                           axis=-1)


def _rope_tables(n_lat):
    t = jnp.arange(n_lat)
    row = (t // GRID_W).astype(F32)
    col = (t % GRID_W).astype(F32)
    n_freq = DIFF_DH // 4
    inv = ROPE_BASE ** (-jnp.arange(n_freq, dtype=F32) / n_freq)
    ang = jnp.concatenate([row[:, None] * inv, col[:, None] * inv], -1)
    cos, sin = jnp.cos(ang), jnp.sin(ang)
    cos_t = jnp.concatenate([cos] * 4, axis=1)
    sin_t = jnp.concatenate([-sin, -sin, sin, sin], axis=1)
    cos_t = jnp.concatenate([jnp.ones((CTX_LEN, LANES), F32), cos_t], axis=0)
    sin_t = jnp.concatenate([jnp.zeros((CTX_LEN, LANES), F32), sin_t], axis=0)
    return cos_t, sin_t


def _pack_decay(w_decay, b_decay):
    depth = w_decay.shape[0]
    wd = jnp.zeros((depth, 2, LANES, GLA_HEADS * GLA_DK), F32)
    for d in range(2):
        wd = wd.at[:, d, d * GLA_RANK:(d + 1) * GLA_RANK].set(w_decay[:, d])
    return wd.astype(BF16), b_decay.reshape(depth, 2, 1, GLA_HEADS * GLA_DK)


def kernel(x, c, ctx, c_ctx, w_ada, b_ada, w_in, gla_w_decay, gla_b_decay, gla_norm_g, diff_lam_q,
           diff_lam_k, diff_norm_g, na_rpb, w_branch, w_o, ln_g, ln_b, w_router, b_router,
           w_exp_gate, w_exp_up, w_exp_down):
    b, l, d = x.shape
    lc = ctx.shape[1]
    ntok = lc + l
    t = b * ntok

    w_in_p = _pack_w_in(w_in)
    wd_p, bd_p = _pack_decay(gla_w_decay, gla_b_decay)
    cos_t, sin_t = _rope_tables(l)
    wb = w_branch.astype(BF16)
    wo = w_o.astype(BF16)
    wg = w_exp_gate.astype(BF16).reshape(DEPTH * N_EXPERTS, d, D_EXPERT)
    wu = w_exp_up.astype(BF16).reshape(DEPTH * N_EXPERTS, d, D_EXPERT)
    wdn = w_exp_down.astype(BF16).reshape(DEPTH * N_EXPERTS, D_EXPERT, d)
    wr_t = w_router.T

    cs = jnp.concatenate([c, c_ctx[None], jnp.zeros((16 - b - 1, d), F32)], axis=0)
    mods = _ada(cs, w_ada, b_ada).reshape(DEPTH, 16, 6, d)

    xs = jnp.concatenate([ctx, x], axis=1)
    prev = None
    for layer in range(DEPTH):
        lam_init = 0.8 - 0.6 * math.exp(-0.3 * layer)
        m_lat = mods[layer, :b]
        m_ctx = jnp.broadcast_to(mods[layer, b][None], (b, 6, d))
        mod2 = jnp.stack([m_ctx, m_lat], axis=1).reshape(2 * b, 6, d)
        mod2 = jnp.pad(mod2, ((0, 0), (0, 2), (0, 0)))

        xs, (gq, gk, gr, gv, gsg, dq, dk, dv, nq, nk, nv, sg) = _proj(
            xs, mod2, w_in_p, cos_t, sin_t, layer, prev)
        o_f, o_b = _gla(gq, gk, gv, gr, wd_p, bd_p, layer)
        d_o = _diff(dq, dk, dv, diff_lam_q[layer], diff_lam_k[layer], diff_norm_g[layer], lam_init)
        n_o = _na(nq, nk, nv, _na_bias(na_rpb[layer]))
        x1, h2, route, counts = _merge(xs, o_f, o_b, gsg, d_o, n_o, sg, mod2, gla_norm_g[layer],
                                       wb, wo, ln_g[layer, 0], ln_b[layer, 0], wr_t, b_router, layer)
        src, meta = _dispatch_plan(route, counts, t)
        y = _moe(h2.reshape(t, d + LANES), src, meta, wg, wu, wdn, layer)
        prev = (x1, y, mod2, ln_g[layer, 1], ln_b[layer, 1])
    return _final(*prev, latents_only=True)
```

```python
import functools
import itertools
import math

import jax
import jax.numpy as jnp
import numpy as np
from jax import lax
from jax.experimental import pallas as pl
from jax.experimental.pallas import tpu as pltpu

F32 = jnp.float32
BF16 = jnp.bfloat16
HIGHEST = lax.Precision.HIGHEST

D_MODEL = 1024
DEPTH = 4
GRID_W = 64
CTX_LEN = 256
BRANCH_W = D_MODEL // 2
GLA_HEADS = 4
GLA_DV = 128
GLA_DK = 64
GLA_RANK = 16
GLA_TAU = 16.0
GLA_CHUNK = 64
GLA_BATCH = 8
DIFF_HEADS = 4
DIFF_DV = 128
DIFF_DH = 64
DIFF_VROWS = DIFF_DV + 16
NA_HEADS = 8
NA_DH = 64
NA_WIN_H = 8
NA_WIN_W = 16
N_EXPERTS = 16
N_GROUPS = 4
EXPERTS_PER_GROUP = 4
D_EXPERT = D_MODEL // 2
ROPE_BASE = 10000.0
LN_EPS = 1e-5
RMS_EPS = 1e-6
NEG_BIG = -1e30
DEEPNORM_ALPHA = (2 * DEPTH) ** 0.25
LOG2E = 1.4426950408889634

LANES = 128
TOK_TILE = 256
NA_TILE = GRID_W
NA_BATCH = 4
MERGE_BATCH = 2
MOE_TILE = 256
N_PAIRS = 6
N_SEG = N_GROUPS * N_PAIRS
SEG_ROWS = 32
PAIR_LO = (0, 0, 0, 1, 1, 2)
PAIR_HI = (1, 2, 3, 2, 3, 3)
VMEM_LIMIT = 56 * 1024 * 1024

W_GQ, W_GK, W_GV, W_GG, W_GR = 0, 256, 512, 1024, 1536
W_DQ, W_DK, W_DV = 1664, 2176, 2688
W_NQ, W_NK, W_NV = 3200, 3712, 4224
W_SG, W_END = 4736, 7808


def _nt(a, b):
    return lax.dot_general(a, b, (((1,), (1,)), ((), ())), preferred_element_type=F32)


def _tn(a, b):
    return lax.dot_general(a, b, (((0,), (0,)), ((), ())), preferred_element_type=F32)


def _mm(a, b):
    return jnp.dot(a, b, preferred_element_type=F32)


def _sigmoid(x):
    return 1.0 / (1.0 + jnp.exp(-x))


def _params(*sem):
    return pltpu.CompilerParams(dimension_semantics=sem, vmem_limit_bytes=VMEM_LIMIT)


def _ada_kernel(c_ref, w_ref, b_ref, o_ref):
    cs = c_ref[...]
    s = cs * _sigmoid(cs)
    o_ref[0] = jnp.dot(s, w_ref[0], precision=HIGHEST, preferred_element_type=F32) + b_ref[0]


def _ada(cs, w_ada, b_ada):
    depth, d, n = w_ada.shape
    bn = 1536
    return pl.pallas_call(
        _ada_kernel,
        grid=(depth, n // bn),
        in_specs=[
            pl.BlockSpec((cs.shape[0], d), lambda l, j: (0, 0)),
            pl.BlockSpec((1, d, bn), lambda l, j: (l, 0, j)),
            pl.BlockSpec((1, 1, bn), lambda l, j: (l, 0, j)),
        ],
        out_specs=pl.BlockSpec((1, cs.shape[0], bn), lambda l, j: (l, 0, j)),
        out_shape=jax.ShapeDtypeStruct((depth, cs.shape[0], n), F32),
        compiler_params=_params("arbitrary", "arbitrary"),
        name="ada",
    )(cs, w_ada, b_ada.reshape(depth, 1, n))


def _proj_kernel(after_moe, *refs):
    if after_moe:
        (x1_ref, y_ref, pmod_ref, lng_ref, lnb_ref, mod_ref, w_ref, cos_ref, sin_ref, xs_ref,
         gq, gk, gr, gv, gsg, dq, dk, dv, nq, nk, nv, sg) = refs
        x = _layernorm(DEEPNORM_ALPHA * x1_ref[0] + pmod_ref[0][5:6] * y_ref[...], lng_ref[...], lnb_ref[...])
        xs_ref[0] = x
    else:
        x_ref, mod_ref, w_ref, cos_ref, sin_ref, gq, gk, gr, gv, gsg, dq, dk, dv, nq, nk, nv, sg = refs
        x = x_ref[0]
    mod = mod_ref[0]
    h = (x * (1.0 + mod[1:2]) + mod[0:1]).astype(BF16)

    def mm(lo, hi):
        return _mm(h, w_ref[0, :, lo:hi])

    gq[0] = mm(W_GQ, W_GK) * (GLA_DK ** -0.5)
    gk[0] = mm(W_GK, W_GV)
    gv[0] = mm(W_GV, W_GG).astype(BF16)
    g = mm(W_GG, W_GR)
    gsg[0] = (g * _sigmoid(g)).astype(BF16)
    gr[0] = mm(W_GR, W_DQ)

    cos = cos_ref[...]
    sin = sin_ref[...]

    def rope(y, scale):
        parts = []
        for i in range(DIFF_HEADS):
            p = y[:, i * LANES:(i + 1) * LANES]
            parts.append(((p * cos + pltpu.roll(p, LANES // 2, 1) * sin) * scale).astype(BF16))
        return jnp.concatenate(parts, axis=1)

    dq[0] = rope(mm(W_DQ, W_DK), (DIFF_DH ** -0.5) * LOG2E)
    dk[0] = rope(mm(W_DK, W_DV), 1.0)
    v = mm(W_DV, W_NQ)
    sub = lax.broadcasted_iota(jnp.int32, (DIFF_VROWS - DIFF_DV, v.shape[0]), 0)
    ones_rows = jnp.where(sub == 0, 1.0, 0.0).astype(BF16)
    for i in range(DIFF_HEADS):
        dv[0, i * DIFF_VROWS:i * DIFF_VROWS + DIFF_DV, :] = v[:, i * LANES:(i + 1) * LANES].T.astype(BF16)
        dv[0, i * DIFF_VROWS + DIFF_DV:(i + 1) * DIFF_VROWS, :] = ones_rows
    nq[0] = (mm(W_NQ, W_NK) * ((NA_DH ** -0.5) * LOG2E)).astype(BF16)
    nk[0] = mm(W_NK, W_NV).astype(BF16)
    nv[0] = mm(W_NV, W_SG).astype(BF16)
    for i in range(3):
        lo = W_SG + i * D_MODEL
        sg[0, :, i * D_MODEL:(i + 1) * D_MODEL] = _sigmoid(mm(lo, lo + D_MODEL)).astype(BF16)


def _proj(xs, mod2, w, cos_t, sin_t, layer, prev=None):
    b, ntok, d = (xs if prev is None else prev[0]).shape
    nt = ntok // TOK_TILE
    widths = (256, 256, 128, 512, 512, 512, 512, None, 512, 512, 512, 3072)
    dtypes = (F32, F32, F32, BF16, BF16, BF16, BF16, BF16, BF16, BF16, BF16, BF16)
    tok = lambda bi, ti: (bi, ti, 0)
    vrows = DIFF_HEADS * DIFF_VROWS

    def out_spec(wd):
        if wd is None:
            return pl.BlockSpec((1, vrows, TOK_TILE), lambda bi, ti: (bi, 0, ti))
        return pl.BlockSpec((1, TOK_TILE, wd), tok)

    def out_shape(wd, dt):
        return jax.ShapeDtypeStruct((b, vrows, ntok) if wd is None else (b, ntok, wd), dt)

    mod_spec = pl.BlockSpec((1, 8, d), lambda bi, ti: (2 * bi + jnp.minimum(ti, 1), 0, 0))
    vec_spec = pl.BlockSpec((1, d), lambda bi, ti: (0, 0))
    common_specs = [
        mod_spec,
        pl.BlockSpec((1, d, W_END), lambda bi, ti: (layer, 0, 0), pipeline_mode=pl.Buffered(1)),
        pl.BlockSpec((TOK_TILE, LANES), lambda bi, ti: (ti, 0)),
        pl.BlockSpec((TOK_TILE, LANES), lambda bi, ti: (ti, 0)),
    ]
    out_specs = [out_spec(wd) for wd in widths]
    out_shapes = [out_shape(wd, dt) for wd, dt in zip(widths, dtypes)]
    if prev is None:
        in_specs = [pl.BlockSpec((1, TOK_TILE, d), tok)] + common_specs
        args = (xs, mod2, w, cos_t, sin_t)
    else:
        x1, y, pmod2, ln_g, ln_b = prev
        in_specs = [pl.BlockSpec((1, TOK_TILE, d), tok),
                    pl.BlockSpec((TOK_TILE, d), lambda bi, ti: (bi * nt + ti, 0)),
                    mod_spec, vec_spec, vec_spec] + common_specs
        out_specs = [pl.BlockSpec((1, TOK_TILE, d), tok)] + out_specs
        out_shapes = [jax.ShapeDtypeStruct((b, ntok, d), F32)] + out_shapes
        args = (x1, y, pmod2, ln_g.reshape(1, d), ln_b.reshape(1, d), mod2, w, cos_t, sin_t)
    outs = pl.pallas_call(
        functools.partial(_proj_kernel, prev is not None),
        grid=(b, nt),
        in_specs=in_specs,
        out_specs=out_specs,
        out_shape=out_shapes,
        compiler_params=_params("arbitrary", "arbitrary"),
        name="proj",
    )(*args)
    return (xs, outs) if prev is None else (outs[0], outs[1:])


def _log_sigmoid(z):
    return -(jnp.maximum(-z, 0.0) + jnp.log(1.0 + jnp.exp(-jnp.abs(z))))


def _gla_kernel(qf, kf, vf, rf, qb, kb, vb, rb, wd_ref, bd_ref, of_ref, ob_ref, *st_refs):
    n = pl.program_id(1)

    @pl.when(n == 0)
    def _():
        for st_ref in st_refs:
            st_ref[...] = jnp.zeros_like(st_ref)

    c = GLA_CHUNK
    row = lax.broadcasted_iota(jnp.int32, (c, c), 0)
    col = lax.broadcasted_iota(jnp.int32, (c, c), 1)
    lane = lax.broadcasted_iota(jnp.int32, (1, LANES), 1)
    head_mask = (lane < GLA_DK, lane >= GLA_DK)

    dirs = ((qf, kf, vf, rf, of_ref), (qb, kb, vb, rb, ob_ref))
    chains = list(itertools.product(range(qf.shape[0]), range(2)))
    causal = [(row >= col), (row <= col)]
    tri = [jnp.where(cz, 1.0, 0.0).astype(F32) for cz in causal]
    pairs = range(GLA_HEADS // 2)
    psl = [slice(p * LANES, (p + 1) * LANES) for p in pairs]

    z = [_mm(dirs[d][3][bb].astype(BF16), wd_ref[0, d]) + bd_ref[0, d] for bb, d in chains]
    cum = [jnp.dot(tri[d], _log_sigmoid(zc) * (1.0 / GLA_TAU), precision=HIGHEST,
                   preferred_element_type=F32) for (bb, d), zc in zip(chains, z)]
    q_in, k_in, k_st, dec = [], [], [], []
    for (bb, d), cm in zip(chains, cum):
        last = cm[c - 1:c] if d == 0 else cm[0:1]
        k = dirs[d][1][bb]
        q_in.append(dirs[d][0][bb] * jnp.exp(cm))
        k_in.append((k * jnp.exp(-cm)).astype(BF16))
        k_st.append(k * jnp.exp(last - cm))
        dec.append(jnp.exp(last))
    att = []
    for ci in range(len(chains)):
        for p in pairs:
            q2 = q_in[ci][:, psl[p]]
            lhs = jnp.concatenate([jnp.where(head_mask[0], q2, 0.0),
                                   jnp.where(head_mask[1], q2, 0.0)], axis=0).astype(BF16)
            att.append(_nt(lhs, k_in[ci][:, psl[p]]))
    for ci, (bb, d) in enumerate(chains):
        v = dirs[d][2][bb]
        for p in pairs:
            q2b = q_in[ci][:, psl[p]].astype(BF16)
            for hh in range(2):
                h = 2 * p + hh
                a = jnp.where(causal[d], att[ci * len(pairs) + p][hh * c:(hh + 1) * c], 0.0).astype(BF16)
                st = st_refs[(bb * 2 + d) * GLA_HEADS + h][...]
                dirs[d][4][bb, :, h * GLA_DV:(h + 1) * GLA_DV] = (
                    _mm(a, v[:, h * GLA_DV:(h + 1) * GLA_DV]) + _nt(q2b, st.astype(BF16)))
    for ci, (bb, d) in enumerate(chains):
        v = dirs[d][2][bb]
        for p in pairs:
            for hh in range(2):
                h = 2 * p + hh
                st_ref = st_refs[(bb * 2 + d) * GLA_HEADS + h]
                kh = jnp.where(head_mask[hh], k_st[ci][:, psl[p]], 0.0).astype(BF16)
                st_ref[...] = st_ref[...] * dec[ci][:, psl[p]] + _tn(v[:, h * GLA_DV:(h + 1) * GLA_DV], kh)


def _gla(gq, gk, gv, gr, wd, bd, layer):
    b, ntok, _ = gq.shape
    nch = ntok // GLA_CHUNK
    nctx = CTX_LEN // GLA_CHUNK
    nb = GLA_BATCH if b % GLA_BATCH == 0 else 1

    def fwd(bi, n):
        return (bi, n, 0)

    def bwd(bi, n):
        return (bi, jnp.where(n < nctx, nctx - 1 - n, nch + nctx - 1 - n), 0)

    def specs(im):
        return [pl.BlockSpec((nb, GLA_CHUNK, 256), im), pl.BlockSpec((nb, GLA_CHUNK, 256), im),
                pl.BlockSpec((nb, GLA_CHUNK, 512), im), pl.BlockSpec((nb, GLA_CHUNK, 128), im)]

    return pl.pallas_call(
        _gla_kernel,
        grid=(b // nb, nch),
        in_specs=specs(fwd) + specs(bwd) + [
            pl.BlockSpec((1, 2, LANES, 256), lambda bi, n: (layer, 0, 0, 0)),
            pl.BlockSpec((1, 2, 1, 256), lambda bi, n: (layer, 0, 0, 0)),
        ],
        out_specs=[pl.BlockSpec((nb, GLA_CHUNK, 512), fwd), pl.BlockSpec((nb, GLA_CHUNK, 512), bwd)],
        out_shape=[jax.ShapeDtypeStruct((b, ntok, 512), F32)] * 2,
        scratch_shapes=[pltpu.VMEM((GLA_DV, LANES), F32)] * (nb * 2 * GLA_HEADS),
        compiler_params=_params("arbitrary", "arbitrary"),
        name="gla",
    )(gq, gk, gv, gr, gq, gk, gv, gr, wd, bd)


DIFF_KC = 256


def _diff_kernel(lam_init, q_ref, k_ref, v_ref, lq_ref, lk_ref, g_ref, o_ref, s_even, s_odd):
    t = pl.program_id(1)
    tq = q_ref.shape[1]
    nkc = k_ref.shape[1] // DIFF_KC
    lane = lax.broadcasted_iota(jnp.int32, (1, LANES), 1)
    comp0 = (lane // (DIFF_DH // 2)) % 2 == 0
    lql = lq_ref[...] * lk_ref[...]
    lam = (jnp.exp(jnp.sum(lql[0:1], axis=1, keepdims=True))
           - jnp.exp(jnp.sum(lql[1:2], axis=1, keepdims=True)) + lam_init)
    s_bufs = (s_even, s_odd)

    def stacked_q(h):
        q = q_ref[0, :, h * LANES:(h + 1) * LANES]
        zero = jnp.zeros_like(q)
        return jnp.concatenate([jnp.where(comp0, q, zero), jnp.where(comp0, zero, q)], axis=0)

    def score_chunk(h, qq, ci, m8):
        s = _nt(k_ref[0, ci * DIFF_KC:(ci + 1) * DIFF_KC, h * LANES:(h + 1) * LANES], qq)
        s_bufs[h % 2][ci] = s
        mc = jnp.max(s.reshape(DIFF_KC // 8, 8, 2 * tq), axis=0)
        return mc if m8 is None else jnp.maximum(m8, mc)

    def value_chunk(h, mrow, ci, acc):
        p = jnp.exp2(s_bufs[h % 2][ci] - mrow).astype(BF16)
        part = _mm(v_ref[0, h * DIFF_VROWS:(h + 1) * DIFF_VROWS, ci * DIFF_KC:(ci + 1) * DIFF_KC], p)
        return part if acc is None else acc + part

    def finish(h, acc):
        o = acc[:DIFF_DV] / acc[DIFF_DV:DIFF_DV + 1]
        o = o[:, :tq] - lam * o[:, tq:]
        o = o * lax.rsqrt(jnp.mean(o * o, axis=0, keepdims=True) + RMS_EPS) * g_ref[...] * (1.0 - lam_init)
        o_ref[0, :, h * DIFF_DV:(h + 1) * DIFF_DV] = o.T.astype(BF16)

    def attend(nch):
        qq = stacked_q(0)
        m8 = None
        for ci in range(nch):
            m8 = score_chunk(0, qq, ci, m8)
        for h in range(1, DIFF_HEADS + 1):
            mrow = jnp.max(m8, axis=0, keepdims=True)
            if h < DIFF_HEADS:
                qq = stacked_q(h)
            acc = m8 = None
            for ci in range(nch):
                if h < DIFF_HEADS:
                    m8 = score_chunk(h, qq, ci, m8)
                acc = value_chunk(h - 1, mrow, ci, acc)
            finish(h - 1, acc)

    @pl.when(t == 0)
    def _():
        attend(1)

    @pl.when(t > 0)
    def _():
        attend(nkc)


def _diff(dq, dk, dv, lam_q, lam_k, norm_g, lam_init):
    b, ntok, w = dq.shape
    nt = ntok // TOK_TILE
    scores = pltpu.VMEM((ntok // DIFF_KC, DIFF_KC, 2 * TOK_TILE), F32)
    return pl.pallas_call(
        functools.partial(_diff_kernel, lam_init),
        grid=(b, nt),
        in_specs=[
            pl.BlockSpec((1, TOK_TILE, w), lambda bi, t: (bi, t, 0)),
            pl.BlockSpec((1, ntok, w), lambda bi, t: (bi, 0, 0), pipeline_mode=pl.Buffered(1)),
            pl.BlockSpec((1, DIFF_HEADS * DIFF_VROWS, ntok), lambda bi, t: (bi, 0, 0),
                         pipeline_mode=pl.Buffered(1)),
            pl.BlockSpec((2, DIFF_DH), lambda bi, t: (0, 0)),
            pl.BlockSpec((2, DIFF_DH), lambda bi, t: (0, 0)),
            pl.BlockSpec((DIFF_DV, 1), lambda bi, t: (0, 0)),
        ],
        out_specs=pl.BlockSpec((1, TOK_TILE, w), lambda bi, t: (bi, t, 0)),
        out_shape=jax.ShapeDtypeStruct((b, ntok, w), BF16),
        scratch_shapes=[scores, scores],
        compiler_params=_params("arbitrary", "arbitrary"),
        name="diff",
    )(dq, dk, dv, lam_q, lam_k, norm_g.reshape(DIFF_DV, 1))


def _na_kernel(rows, q_ref, k_ref, v_ref, b_ref, o_ref):
    t = pl.program_id(1)
    nctx = CTX_LEN // NA_TILE
    r = t - nctx
    rs = jnp.clip(r - NA_WIN_H // 2, 0, rows - NA_WIN_H)
    start = pl.multiple_of(CTX_LEN + rs * GRID_W, GRID_W)
    nloc = NA_WIN_H * GRID_W
    lane = lax.broadcasted_iota(jnp.int32, (1, LANES), 1)
    first = lane < NA_DH
    chains = list(itertools.product(range(q_ref.shape[0]), range(NA_HEADS // 2)))
    psl = [slice(p * LANES, (p + 1) * LANES) for p in range(NA_HEADS // 2)]
    scores = []
    for bb, p in chains:
        q2 = q_ref[bb, :, psl[p]]
        zero = jnp.zeros_like(q2)
        lhs = jnp.concatenate([jnp.where(first, q2, zero), jnp.where(first, zero, q2)], axis=0)
        s_loc = _nt(lhs, k_ref[bb, pl.ds(start, nloc), psl[p]]) + b_ref[p]
        s_ctx = _nt(lhs, k_ref[bb, 0:CTX_LEN, psl[p]])
        scores.append((s_loc, s_ctx))
    probs = []
    for s_loc, s_ctx in scores:
        m = jnp.maximum(jnp.max(s_loc, axis=1, keepdims=True), jnp.max(s_ctx, axis=1, keepdims=True))
        p_loc = jnp.exp2(s_loc - m)
        p_ctx = jnp.exp2(s_ctx - m)
        l = jnp.sum(p_loc, axis=1, keepdims=True) + jnp.sum(p_ctx, axis=1, keepdims=True)
        probs.append((p_loc.astype(BF16), p_ctx.astype(BF16), l))
    for (bb, p), (p_loc, p_ctx, l) in zip(chains, probs):
        o = (_mm(p_loc, v_ref[bb, pl.ds(start, nloc), psl[p]]) + _mm(p_ctx, v_ref[bb, 0:CTX_LEN, psl[p]])) / l
        o_ref[bb, :, psl[p]] = jnp.where(first, o[:NA_TILE], o[NA_TILE:]).astype(BF16)


def _na(nq, nk, nv, bias):
    b, ntok, w = nq.shape
    nt = ntok // NA_TILE
    nctx = CTX_LEN // NA_TILE
    rows = (ntok - CTX_LEN) // GRID_W
    nb = NA_BATCH if b % NA_BATCH == 0 else 1

    def cfg(bi, t):
        r = t - nctx
        return (jnp.where(t < nctx, NA_WIN_H, r - jnp.clip(r - NA_WIN_H // 2, 0, rows - NA_WIN_H)), 0, 0)

    kv_spec = pl.BlockSpec((nb, ntok, w), lambda bi, t: (bi, 0, 0), pipeline_mode=pl.Buffered(1))
    return pl.pallas_call(
        functools.partial(_na_kernel, rows),
        grid=(b // nb, nt),
        in_specs=[
            pl.BlockSpec((nb, NA_TILE, w), lambda bi, t: (bi, t, 0)),
            kv_spec, kv_spec,
            pl.BlockSpec((NA_HEADS // 2, 2 * NA_TILE, NA_WIN_H * GRID_W), cfg),
        ],
        out_specs=pl.BlockSpec((nb, NA_TILE, w), lambda bi, t: (bi, t, 0)),
        out_shape=jax.ShapeDtypeStruct((b, ntok, w), BF16),
        compiler_params=_params("arbitrary", "arbitrary"),
        name="na",
    )(nq, nk, nv, bias)


def _na_bias(rpb):
    qc = np.arange(GRID_W)[:, None]
    kc = np.arange(GRID_W)[None, :]
    col_idx = np.clip(kc - qc + NA_WIN_W - 1, 0, 2 * NA_WIN_W - 2)
    cstart = np.clip(qc - NA_WIN_W // 2, 0, GRID_W - NA_WIN_W)
    win = (kc >= cstart) & (kc < cstart + NA_WIN_W)
    toep = jnp.where(win[None, None], rpb.astype(F32)[..., col_idx] * LOG2E, NEG_BIG)
    toep = toep.transpose(0, 2, 1, 3)
    cfgs = [toep[:, :, NA_WIN_H - 1 - c:2 * NA_WIN_H - 1 - c] for c in range(NA_WIN_H)]
    cfgs.append(jnp.full(cfgs[0].shape, NEG_BIG, F32))
    return jnp.stack(cfgs, axis=0).reshape(-1, 2 * GRID_W, NA_WIN_H * GRID_W)


def _layernorm(x, g, b):
    mu = jnp.mean(x, axis=1, keepdims=True)
    xc = x - mu
    var = jnp.mean(xc * xc, axis=1, keepdims=True)
    return xc * lax.rsqrt(var + LN_EPS) * g + b


def _route(logits, bias):
    aff = _sigmoid(logits)
    sel = aff + bias
    srow = [sel[e:e + 1] for e in range(N_EXPERTS)]
    arow = [aff[e:e + 1] for e in range(N_EXPERTS)]
    gscore = []
    for g in range(N_GROUPS):
        a0, a1, a2, a3 = srow[4 * g:4 * g + 4]
        hi01, lo01 = jnp.maximum(a0, a1), jnp.minimum(a0, a1)
        hi23, lo23 = jnp.maximum(a2, a3), jnp.minimum(a2, a3)
        top1 = jnp.maximum(hi01, hi23)
        top2 = jnp.maximum(jnp.minimum(hi01, hi23), jnp.maximum(lo01, lo23))
        gscore.append(top1 + top2)
    best = jnp.zeros_like(gscore[0])
    bestv = gscore[0]
    for g in range(1, N_GROUPS):
        better = gscore[g] > bestv
        best = jnp.where(better, float(g), best)
        bestv = jnp.where(better, gscore[g], bestv)

    def pick(rows_, i):
        out = rows_[i]
        for g in range(1, N_GROUPS):
            out = jnp.where(best == float(g), rows_[4 * g + i], out)
        return out

    s4 = [pick(srow, i) for i in range(4)]
    f4 = [pick(arow, i) for i in range(4)]
    chosen = []
    for i in range(4):
        rank = jnp.zeros_like(best)
        for j in range(4):
            if j == i:
                continue
            ahead = (s4[j] > s4[i]) | ((s4[j] == s4[i]) & (j < i))
            rank = rank + jnp.where(ahead, 1.0, 0.0)
        chosen.append(rank < 2.0)
    c0, c1, c2, c3 = chosen
    pidx = jnp.where(c0, jnp.where(c1, 0.0, jnp.where(c2, 1.0, 2.0)),
                     jnp.where(c1, jnp.where(c2, 3.0, 4.0), 5.0))
    a_lo = jnp.where(c0, f4[0], jnp.where(c1, f4[1], f4[2]))
    a_hi = jnp.where(c3, f4[3], jnp.where(c2, f4[2], f4[1]))
    den = a_lo + a_hi
    return best * float(N_PAIRS) + pidx, a_lo / den, a_hi / den


def _merge_kernel(x_ref, of_ref, ob_ref, gsg_ref, do_ref, no_ref, sg_ref, mod_ref, gg_ref,
                  wb_ref, wo_ref, lng_ref, lnb_ref, wr_ref, br_ref,
                  x1_ref, h2_ref, route_ref, cnt_ref, run_ref):
    first_step = jnp.logical_and(pl.program_id(0) == 0, pl.program_id(1) == 0)

    @pl.when(first_step)
    def _():
        run_ref[...] = jnp.zeros_like(run_ref)

    nb, tm = x_ref.shape[0], x_ref.shape[1]
    tiles = range(nb)
    mods = [mod_ref[bb, 0] for bb in tiles]
    gg = gg_ref[...]

    branches = []
    for bb in tiles:
        o = of_ref[bb] + ob_ref[bb]
        parts = []
        for h in range(GLA_HEADS):
            oh = o[:, h * GLA_DV:(h + 1) * GLA_DV]
            parts.append(oh * lax.rsqrt(jnp.mean(oh * oh, axis=1, keepdims=True) + RMS_EPS) * gg)
        gla = (jnp.concatenate(parts, axis=1) * gsg_ref[bb].astype(F32)).astype(BF16)
        branches.append((gla, do_ref[bb], no_ref[bb]))
    ys = [None] * nb
    for i in range(3):
        for bb in tiles:
            term = sg_ref[bb, :, i * D_MODEL:(i + 1) * D_MODEL].astype(F32) * _mm(branches[bb][i], wb_ref[0, i])
            ys[bb] = term if i == 0 else ys[bb] + term
    ys = [_mm(y.astype(BF16), wo_ref[0]) for y in ys]
    h2s = []
    for bb in tiles:
        x1 = _layernorm(DEEPNORM_ALPHA * x_ref[bb] + mods[bb][2:3] * ys[bb], lng_ref[...], lnb_ref[...])
        x1_ref[bb] = x1
        h2 = x1 * (1.0 + mods[bb][4:5]) + mods[bb][3:4]
        h2_ref[bb, :, :D_MODEL] = h2
        h2s.append(h2)
    logits = [lax.dot_general(wr_ref[...], h2, (((1,), (1,)), ((), ())),
                              precision=HIGHEST, preferred_element_type=F32) for h2 in h2s]
    routes = [_route(lg, br_ref[...]) for lg in logits]

    srows = lax.broadcasted_iota(jnp.int32, (SEG_ROWS, tm), 0).astype(F32)
    ii = lax.broadcasted_iota(jnp.int32, (tm, tm), 0)
    jj = lax.broadcasted_iota(jnp.int32, (tm, tm), 1)
    before = jnp.where(ii < jj, 1.0, 0.0).astype(BF16)
    onehots = [jnp.where(srows == seg, 1.0, 0.0) for seg, _, _ in routes]
    prefixes = [_mm(oh.astype(BF16), before) for oh in onehots]
    run = run_ref[...]
    for bb in tiles:
        seg, w_lo, w_hi = routes[bb]
        rank = jnp.sum(onehots[bb] * (prefixes[bb] + run[:, 0:1]), axis=0, keepdims=True)
        run = run + jnp.sum(onehots[bb], axis=1, keepdims=True)
        zrow = jnp.zeros_like(seg)
        route = jnp.concatenate([seg, rank, w_lo, w_hi, zrow, zrow, zrow, zrow], axis=0)
        route_ref[bb, 0] = route
        wide = jnp.concatenate([route, jnp.zeros((LANES - 8, tm), F32)], axis=0)
        h2_ref[bb, :, D_MODEL:] = wide.T
    run_ref[...] = run
    cnt_ref[...] = run


def _merge(xs, o_f, o_b, gsg, d_o, n_o, sg, mod2, gla_g, wb, wo, ln_g, ln_b, wr_t, br, layer):
    b, ntok, d = xs.shape
    nt = ntok // TOK_TILE
    nb = MERGE_BATCH if b % MERGE_BATCH == 0 else 1
    tok = lambda bi, ti: (bi, ti, 0)
    const2 = lambda bi, ti: (0, 0)

    def tokspec(wd):
        return pl.BlockSpec((nb, TOK_TILE, wd), tok)

    return pl.pallas_call(
        _merge_kernel,
        grid=(b // nb, nt),
        in_specs=[
            tokspec(d), tokspec(512), tokspec(512), tokspec(512), tokspec(512), tokspec(512),
            tokspec(3 * d),
            pl.BlockSpec((nb, 1, 8, d), lambda bi, ti: (bi, jnp.minimum(ti, 1), 0, 0)),
            pl.BlockSpec((1, GLA_DV), const2),
            pl.BlockSpec((1, 3, BRANCH_W, d), lambda bi, ti: (layer, 0, 0, 0)),
            pl.BlockSpec((1, d, d), lambda bi, ti: (layer, 0, 0)),
            pl.BlockSpec((1, d), const2),
            pl.BlockSpec((1, d), const2),
            pl.BlockSpec((N_EXPERTS, d), const2),
            pl.BlockSpec((N_EXPERTS, 1), const2),
        ],
        out_specs=[
            tokspec(d), tokspec(d + LANES),
            pl.BlockSpec((nb, 1, 8, TOK_TILE), lambda bi, ti: (bi, ti, 0, 0)),
            pl.BlockSpec((SEG_ROWS, LANES), const2),
        ],
        out_shape=[
            jax.ShapeDtypeStruct((b, ntok, d), F32),
            jax.ShapeDtypeStruct((b, ntok, d + LANES), F32),
            jax.ShapeDtypeStruct((b, nt, 8, TOK_TILE), F32),
            jax.ShapeDtypeStruct((SEG_ROWS, LANES), F32),
        ],
        scratch_shapes=[pltpu.VMEM((SEG_ROWS, LANES), F32)],
        compiler_params=_params("arbitrary", "arbitrary"),
        name="merge",
    )(xs, o_f, o_b, gsg, d_o, n_o, sg, mod2.reshape(b, 2, 8, d), gla_g.reshape(1, GLA_DV), wb, wo,
      ln_g.reshape(1, d), ln_b.reshape(1, d), wr_t, br.reshape(N_EXPERTS, 1))


def _moe_kernel(dest, meta, h_hbm, wg1, wu1, wd1, wg2, wu2, wd2, y_hbm,
                src, xbuf, ybuf, sem_in, sem_out):
    i = pl.program_id(0)
    ntile = pl.num_programs(0)
    n_used = meta[2 * ntile]
    slot = i % 2
    tm = MOE_TILE

    def n_real(tile):
        return meta[2 * ntile + 1 + tile]

    def gather(tile, sl):
        def body(j, carry):
            tok = src[tile * tm + j]
            pltpu.make_async_copy(h_hbm.at[pl.ds(tok, 1)], xbuf.at[sl, pl.ds(j, 1)], sem_in.at[sl]).start()
            return carry
        lax.fori_loop(0, tm, body, 0, unroll=8)

    def wait_in(sl):
        pltpu.make_async_copy(h_hbm.at[pl.ds(0, tm)], xbuf.at[sl], sem_in.at[sl]).wait()

    def row_out(tile, sl, j):
        tok = src[tile * tm + j]
        return pltpu.make_async_copy(ybuf.at[sl, pl.ds(j, 1)], y_hbm.at[pl.ds(tok, 1)], sem_out.at[sl])

    def scatter(tile, sl):
        nr = n_real(tile)

        def body(j, carry):
            row_out(tile, sl, j).start()
            return carry

        @pl.when(nr == tm)
        def _():
            for j in range(tm):
                row_out(tile, sl, j).start()

        @pl.when(nr < tm)
        def _():
            lax.fori_loop(0, nr, body, 0)

    def wait_out(tile, sl):
        nr = n_real(tile)

        @pl.when(nr == tm)
        def _():
            pltpu.make_async_copy(ybuf.at[sl], y_hbm.at[pl.ds(0, tm)], sem_out.at[sl]).wait()

        @pl.when(nr < tm)
        def _():
            def body(j, carry):
                row_out(tile, sl, j).wait()
                return carry
            lax.fori_loop(0, nr, body, 0)

    @pl.when(i == 0)
    def _():
        def clear(j, carry):
            src[j] = 0
            return carry
        lax.fori_loop(0, src.shape[0], clear, 0, unroll=8)

        def place(j, carry):
            src[dest[j]] = j
            return carry
        lax.fori_loop(0, dest.shape[0], place, 0, unroll=8)
        gather(0, 0)

    @pl.when(i < n_used)
    def _():
        wait_in(slot)

        @pl.when(i >= 2)
        def _():
            wait_out(i - 2, slot)

        for j in range(tm):
            tok = src[(i + 1) * tm + j]
            pltpu.make_async_copy(h_hbm.at[pl.ds(tok, 1)], xbuf.at[1 - slot, pl.ds(j, 1)],
                                  sem_in.at[1 - slot]).start()

        x = xbuf[slot, :, :D_MODEL].astype(BF16)
        w = xbuf[slot, :, D_MODEL:]

        a1, u1 = _mm(x, wg1[0]), _mm(x, wu1[0])
        a2, u2 = _mm(x, wg2[0]), _mm(x, wu2[0])
        y1 = _mm((a1 * _sigmoid(a1) * u1).astype(BF16), wd1[0])
        y2 = _mm((a2 * _sigmoid(a2) * u2).astype(BF16), wd2[0])
        ybuf[slot] = w[:, 2:3] * y1 + w[:, 3:4] * y2
        scatter(i, slot)

    @pl.when(i == ntile - 1)
    def _():
        wait_in(n_used % 2)
        wait_out(n_used - 1, (n_used - 1) % 2)

        @pl.when(n_used >= 2)
        def _():
            wait_out(n_used - 2, n_used % 2)


def _moe(h2, dest, meta, wg, wu, wd, layer):
    t, dw = h2.shape
    d = dw - LANES
    ntile = (meta.shape[0] - 1) // 3
    lo = lambda i, s, m: (layer * N_EXPERTS + m[2 * i], 0, 0)
    hi = lambda i, s, m: (layer * N_EXPERTS + m[2 * i + 1], 0, 0)
    grid_spec = pltpu.PrefetchScalarGridSpec(
        num_scalar_prefetch=2,
        grid=(ntile,),
        in_specs=[
            pl.BlockSpec(memory_space=pl.ANY),
            pl.BlockSpec((1, d, D_EXPERT), lo), pl.BlockSpec((1, d, D_EXPERT), lo),
            pl.BlockSpec((1, D_EXPERT, d), lo),
            pl.BlockSpec((1, d, D_EXPERT), hi), pl.BlockSpec((1, d, D_EXPERT), hi),
            pl.BlockSpec((1, D_EXPERT, d), hi),
        ],
        out_specs=pl.BlockSpec(memory_space=pl.ANY),
        scratch_shapes=[pltpu.SMEM(((ntile + 1) * MOE_TILE,), jnp.int32),
                        pltpu.VMEM((2, MOE_TILE, dw), F32), pltpu.VMEM((2, MOE_TILE, d), F32),
                        pltpu.SemaphoreType.DMA((2,)), pltpu.SemaphoreType.DMA((2,))],
    )
    return pl.pallas_call(
        _moe_kernel,
        grid_spec=grid_spec,
        out_shape=jax.ShapeDtypeStruct((t, d), F32),
        compiler_params=_params("arbitrary"),
        name="moe",
    )(dest, meta, h2, wg, wu, wd, wg, wu, wd)


def _lookup(idx, table):
    n = table.shape[0]
    hit = idx[:, None] == jnp.arange(n, dtype=jnp.int32)[None, :]
    return jnp.sum(jnp.where(hit, table[None, :], 0), axis=1)


def _dispatch_plan(route, counts, t):
    ntile = (t + N_SEG * (MOE_TILE - 1)) // MOE_TILE
    seg = route[:, :, 0, :].reshape(t).astype(jnp.int32)
    rank = route[:, :, 1, :].reshape(t).astype(jnp.int32)
    cnt = counts[:N_SEG, 0].astype(jnp.int32)
    seg_tiles = (cnt + MOE_TILE - 1) // MOE_TILE
    upto = jnp.arange(N_SEG)[None, :] <= jnp.arange(N_SEG)[:, None]
    tile_end = jnp.sum(jnp.where(upto, seg_tiles[None, :], 0), axis=1)
    first_tile = tile_end - seg_tiles
    dest = _lookup(seg, first_tile * MOE_TILE) + rank
    n_used = tile_end[-1]
    tiles = jnp.arange(ntile, dtype=jnp.int32)
    tile_seg = jnp.sum((tile_end[None, :] <= jnp.minimum(tiles, n_used - 1)[:, None]).astype(jnp.int32), axis=1)
    tile_seg = jnp.minimum(tile_seg, N_SEG - 1)
    base = (tile_seg // N_PAIRS) * EXPERTS_PER_GROUP
    pair = tile_seg % N_PAIRS
    e_lo = base + _lookup(pair, jnp.asarray(PAIR_LO, jnp.int32))
    e_hi = base + _lookup(pair, jnp.asarray(PAIR_HI, jnp.int32))
    left = _lookup(tile_seg, cnt) - (tiles - _lookup(tile_seg, first_tile)) * MOE_TILE
    n_real = jnp.where(tiles < n_used, jnp.clip(left, 0, MOE_TILE), 0)
    meta = jnp.concatenate([jnp.stack([e_lo, e_hi], axis=1).reshape(-1), n_used[None], n_real]).astype(jnp.int32)
    return dest, meta


def _final_kernel(x_ref, y_ref, mod_ref, g_ref, b_ref, o_ref):
    mod = mod_ref[0]
    o_ref[0] = _layernorm(DEEPNORM_ALPHA * x_ref[0] + mod[5:6] * y_ref[...], g_ref[...], b_ref[...])


def _final(x1, y, mod2, ln_g, ln_b, latents_only):
    b, ntok, d = x1.shape
    nt = ntok // TOK_TILE
    skip = CTX_LEN // TOK_TILE if latents_only else 0
    return pl.pallas_call(
        _final_kernel,
        grid=(b, nt - skip),
        in_specs=[
            pl.BlockSpec((1, TOK_TILE, d), lambda bi, ti: (bi, ti + skip, 0)),
            pl.BlockSpec((TOK_TILE, d), lambda bi, ti: (bi * nt + ti + skip, 0)),
            pl.BlockSpec((1, 8, d), lambda bi, ti: (2 * bi + jnp.minimum(ti + skip, 1), 0, 0)),
            pl.BlockSpec((1, d), lambda bi, ti: (0, 0)),
            pl.BlockSpec((1, d), lambda bi, ti: (0, 0)),
        ],
        out_specs=pl.BlockSpec((1, TOK_TILE, d), lambda bi, ti: (bi, ti, 0)),
        out_shape=jax.ShapeDtypeStruct((b, ntok - skip * TOK_TILE, d), F32),
        compiler_params=_params("arbitrary", "arbitrary"),
        name="final_ln",
    )(x1, y, mod2, ln_g.reshape(1, d), ln_b.reshape(1, d))


def _rotary_order(w):
    lead = w.shape[:-1]
    half = DIFF_DH // 2
    return w.reshape(lead + (DIFF_HEADS, 2, 2, half)).swapaxes(-2, -3).reshape(lead + (DIFF_HEADS * 2 * DIFF_DH,))


def _pack_w_in(w_in):
    splits = np.cumsum([256, 256, 512, 512, 32, 512, 512, 512, 512, 512, 512])
    gq, gk, gv, gg, gr, dq, dk, dv, nq, nk, nv, sg = jnp.split(w_in.astype(BF16), splits, axis=-1)
    gr = jnp.pad(gr, ((0, 0), (0, 0), (0, LANES - 2 * GLA_RANK)))
    return jnp.concatenate([gq, gk, gv, gg, gr, _rotary_order(dq), _rotary_order(dk), dv, nq, nk, nv, sg],
                           axis=-1)


def _rope_tables(n_lat):
    t = jnp.arange(n_lat)
    row = (t // GRID_W).astype(F32)
    col = (t % GRID_W).astype(F32)
    n_freq = DIFF_DH // 4
    inv = ROPE_BASE ** (-jnp.arange(n_freq, dtype=F32) / n_freq)
    ang = jnp.concatenate([row[:, None] * inv, col[:, None] * inv], -1)
    cos, sin = jnp.cos(ang), jnp.sin(ang)
    cos_t = jnp.concatenate([cos] * 4, axis=1)
    sin_t = jnp.concatenate([-sin, -sin, sin, sin], axis=1)
    cos_t = jnp.concatenate([jnp.ones((CTX_LEN, LANES), F32), cos_t], axis=0)
    sin_t = jnp.concatenate([jnp.zeros((CTX_LEN, LANES), F32), sin_t], axis=0)
    return cos_t, sin_t


def _pack_decay(w_decay, b_decay):
    depth = w_decay.shape[0]
    wd = jnp.zeros((depth, 2, LANES, GLA_HEADS * GLA_DK), F32)
    for d in range(2):
        wd = wd.at[:, d, d * GLA_RANK:(d + 1) * GLA_RANK].set(w_decay[:, d])
    return wd.astype(BF16), b_decay.reshape(depth, 2, 1, GLA_HEADS * GLA_DK)


def kernel(x, c, ctx, c_ctx, w_ada, b_ada, w_in, gla_w_decay, gla_b_decay, gla_norm_g, diff_lam_q,
           diff_lam_k, diff_norm_g, na_rpb, w_branch, w_o, ln_g, ln_b, w_router, b_router,
           w_exp_gate, w_exp_up, w_exp_down):
    b, l, d = x.shape
    lc = ctx.shape[1]
    ntok = lc + l
    t = b * ntok

    w_in_p = _pack_w_in(w_in)
    wd_p, bd_p = _pack_decay(gla_w_decay, gla_b_decay)
    cos_t, sin_t = _rope_tables(l)
    wb = w_branch.astype(BF16)
    wo = w_o.astype(BF16)
    wg = w_exp_gate.astype(BF16).reshape(DEPTH * N_EXPERTS, d, D_EXPERT)
    wu = w_exp_up.astype(BF16).reshape(DEPTH * N_EXPERTS, d, D_EXPERT)
    wdn = w_exp_down.astype(BF16).reshape(DEPTH * N_EXPERTS, D_EXPERT, d)
    wr_t = w_router.T

    cs = jnp.concatenate([c, c_ctx[None], jnp.zeros((16 - b - 1, d), F32)], axis=0)
    mods = _ada(cs, w_ada, b_ada).reshape(DEPTH, 16, 6, d)

    xs = jnp.concatenate([ctx, x], axis=1)
    prev = None
    for layer in range(DEPTH):
        lam_init = 0.8 - 0.6 * math.exp(-0.3 * layer)
        m_lat = mods[layer, :b]
        m_ctx = jnp.broadcast_to(mods[layer, b][None], (b, 6, d))
        mod2 = jnp.stack([m_ctx, m_lat], axis=1).reshape(2 * b, 6, d)
        mod2 = jnp.pad(mod2, ((0, 0), (0, 2), (0, 0)))

        xs, (gq, gk, gr, gv, gsg, dq, dk, dv, nq, nk, nv, sg) = _proj(
            xs, mod2, w_in_p, cos_t, sin_t, layer, prev)
        o_f, o_b = _gla(gq, gk, gv, gr, wd_p, bd_p, layer)
        d_o = _diff(dq, dk, dv, diff_lam_q[layer], diff_lam_k[layer], diff_norm_g[layer], lam_init)
        n_o = _na(nq, nk, nv, _na_bias(na_rpb[layer]))
        x1, h2, route, counts = _merge(xs, o_f, o_b, gsg, d_o, n_o, sg, mod2, gla_norm_g[layer],
                                       wb, wo, ln_g[layer, 0], ln_b[layer, 0], wr_t, b_router, layer)
        dest, meta = _dispatch_plan(route, counts, t)
        y = _moe(h2.reshape(t, d + LANES), dest, meta, wg, wu, wdn, layer)
        prev = (x1, y, mod2, ln_g[layer, 1], ln_b[layer, 1])
    return _final(*prev, latents_only=True)
```

```python
import functools
import itertools
import math

import jax
import jax.numpy as jnp
import numpy as np
from jax import lax
from jax.experimental import pallas as pl
from jax.experimental.pallas import tpu as pltpu

F32 = jnp.float32
BF16 = jnp.bfloat16
HIGHEST = lax.Precision.HIGHEST

D_MODEL = 1024
DEPTH = 4
GRID_W = 64
CTX_LEN = 256
BRANCH_W = D_MODEL // 2
GLA_HEADS = 4
GLA_DV = 128
GLA_DK = 64
GLA_RANK = 16
GLA_TAU = 16.0
GLA_CHUNK = 64
GLA_BATCH = 8
DIFF_HEADS = 4
DIFF_DV = 128
DIFF_DH = 64
DIFF_VROWS = DIFF_DV + 16
NA_HEADS = 8
NA_DH = 64
NA_WIN_H = 8
NA_WIN_W = 16
N_EXPERTS = 16
N_GROUPS = 4
EXPERTS_PER_GROUP = 4
D_EXPERT = D_MODEL // 2
ROPE_BASE = 10000.0
LN_EPS = 1e-5
RMS_EPS = 1e-6
NEG_BIG = -1e30
DEEPNORM_ALPHA = (2 * DEPTH) ** 0.25
LOG2E = 1.4426950408889634

LANES = 128
TOK_TILE = 256
NA_TILE = GRID_W
NA_BATCH = 4
MERGE_BATCH = 2
MOE_TILE = 256
N_PAIRS = 6
N_SEG = N_GROUPS * N_PAIRS
SEG_ROWS = 32
PAIR_LO = (0, 0, 0, 1, 1, 2)
PAIR_HI = (1, 2, 3, 2, 3, 3)
VMEM_LIMIT = 56 * 1024 * 1024

W_GQ, W_GK, W_GV, W_GG, W_GR = 0, 256, 512, 1024, 1536
W_DQ, W_DK, W_DV = 1664, 2176, 2688
W_NQ, W_NK, W_NV = 3200, 3712, 4224
W_SG, W_END = 4736, 7808


def _nt(a, b):
    return lax.dot_general(a, b, (((1,), (1,)), ((), ())), preferred_element_type=F32)


def _tn(a, b):
    return lax.dot_general(a, b, (((0,), (0,)), ((), ())), preferred_element_type=F32)


def _mm(a, b):
    return jnp.dot(a, b, preferred_element_type=F32)


def _sigmoid(x):
    return 1.0 / (1.0 + jnp.exp(-x))


def _params(*sem):
    return pltpu.CompilerParams(dimension_semantics=sem, vmem_limit_bytes=VMEM_LIMIT)


def _ada_kernel(c_ref, w_ref, b_ref, o_ref):
    cs = c_ref[...]
    s = cs * _sigmoid(cs)
    o_ref[0] = jnp.dot(s, w_ref[0], precision=HIGHEST, preferred_element_type=F32) + b_ref[0]


def _ada(cs, w_ada, b_ada):
    depth, d, n = w_ada.shape
    bn = 1536
    return pl.pallas_call(
        _ada_kernel,
        grid=(depth, n // bn),
        in_specs=[
            pl.BlockSpec((cs.shape[0], d), lambda l, j: (0, 0)),
            pl.BlockSpec((1, d, bn), lambda l, j: (l, 0, j)),
            pl.BlockSpec((1, 1, bn), lambda l, j: (l, 0, j)),
        ],
        out_specs=pl.BlockSpec((1, cs.shape[0], bn), lambda l, j: (l, 0, j)),
        out_shape=jax.ShapeDtypeStruct((depth, cs.shape[0], n), F32),
        compiler_params=_params("arbitrary", "arbitrary"),
        name="ada",
    )(cs, w_ada, b_ada.reshape(depth, 1, n))


def _proj_kernel(after_moe, *refs):
    if after_moe:
        (x1_ref, y_ref, pmod_ref, lng_ref, lnb_ref, mod_ref, w_ref, cos_ref, sin_ref, xs_ref,
         gq, gk, gr, gv, gsg, dq, dk, dv, nq, nk, nv, sg) = refs
        x = _layernorm(DEEPNORM_ALPHA * x1_ref[0] + pmod_ref[0][5:6] * y_ref[...], lng_ref[...], lnb_ref[...])
        xs_ref[0] = x
    else:
        x_ref, mod_ref, w_ref, cos_ref, sin_ref, gq, gk, gr, gv, gsg, dq, dk, dv, nq, nk, nv, sg = refs
        x = x_ref[0]
    mod = mod_ref[0]
    h = (x * (1.0 + mod[1:2]) + mod[0:1]).astype(BF16)

    def mm(lo, hi):
        return _mm(h, w_ref[0, :, lo:hi])

    gq[0] = mm(W_GQ, W_GK) * (GLA_DK ** -0.5)
    gk[0] = mm(W_GK, W_GV)
    gv[0] = mm(W_GV, W_GG).astype(BF16)
    g = mm(W_GG, W_GR)
    gsg[0] = (g * _sigmoid(g)).astype(BF16)
    gr[0] = mm(W_GR, W_DQ)

    cos = cos_ref[...]
    sin = sin_ref[...]

    def rope(y, scale):
        parts = []
        for i in range(DIFF_HEADS):
            p = y[:, i * LANES:(i + 1) * LANES]
            parts.append(((p * cos + pltpu.roll(p, LANES // 2, 1) * sin) * scale).astype(BF16))
        return jnp.concatenate(parts, axis=1)

    dq[0] = rope(mm(W_DQ, W_DK), (DIFF_DH ** -0.5) * LOG2E)
    dk[0] = rope(mm(W_DK, W_DV), 1.0)
    v = mm(W_DV, W_NQ)
    sub = lax.broadcasted_iota(jnp.int32, (DIFF_VROWS - DIFF_DV, v.shape[0]), 0)
    ones_rows = jnp.where(sub == 0, 1.0, 0.0).astype(BF16)
    for i in range(DIFF_HEADS):
        dv[0, i * DIFF_VROWS:i * DIFF_VROWS + DIFF_DV, :] = v[:, i * LANES:(i + 1) * LANES].T.astype(BF16)
        dv[0, i * DIFF_VROWS + DIFF_DV:(i + 1) * DIFF_VROWS, :] = ones_rows
    nq[0] = (mm(W_NQ, W_NK) * ((NA_DH ** -0.5) * LOG2E)).astype(BF16)
    nk[0] = mm(W_NK, W_NV).astype(BF16)
    nv[0] = mm(W_NV, W_SG).astype(BF16)
    for i in range(3):
        lo = W_SG + i * D_MODEL
        sg[0, :, i * D_MODEL:(i + 1) * D_MODEL] = _sigmoid(mm(lo, lo + D_MODEL)).astype(BF16)


def _proj(xs, mod2, w, cos_t, sin_t, layer, prev=None):
    b, ntok, d = (xs if prev is None else prev[0]).shape
    nt = ntok // TOK_TILE
    widths = (256, 256, 128, 512, 512, 512, 512, None, 512, 512, 512, 3072)
    dtypes = (F32, F32, F32, BF16, BF16, BF16, BF16, BF16, BF16, BF16, BF16, BF16)
    tok = lambda bi, ti: (bi, ti, 0)
    vrows = DIFF_HEADS * DIFF_VROWS

    def out_spec(wd):
        if wd is None:
            return pl.BlockSpec((1, vrows, TOK_TILE), lambda bi, ti: (bi, 0, ti))
        return pl.BlockSpec((1, TOK_TILE, wd), tok)

    def out_shape(wd, dt):
        return jax.ShapeDtypeStruct((b, vrows, ntok) if wd is None else (b, ntok, wd), dt)

    mod_spec = pl.BlockSpec((1, 8, d), lambda bi, ti: (2 * bi + jnp.minimum(ti, 1), 0, 0))
    vec_spec = pl.BlockSpec((1, d), lambda bi, ti: (0, 0))
    common_specs = [
        mod_spec,
        pl.BlockSpec((1, d, W_END), lambda bi, ti: (layer, 0, 0), pipeline_mode=pl.Buffered(1)),
        pl.BlockSpec((TOK_TILE, LANES), lambda bi, ti: (ti, 0)),
        pl.BlockSpec((TOK_TILE, LANES), lambda bi, ti: (ti, 0)),
    ]
    out_specs = [out_spec(wd) for wd in widths]
    out_shapes = [out_shape(wd, dt) for wd, dt in zip(widths, dtypes)]
    if prev is None:
        in_specs = [pl.BlockSpec((1, TOK_TILE, d), tok)] + common_specs
        args = (xs, mod2, w, cos_t, sin_t)
    else:
        x1, y, pmod2, ln_g, ln_b = prev
        in_specs = [pl.BlockSpec((1, TOK_TILE, d), tok),
                    pl.BlockSpec((TOK_TILE, d), lambda bi, ti: (bi * nt + ti, 0)),
                    mod_spec, vec_spec, vec_spec] + common_specs
        out_specs = [pl.BlockSpec((1, TOK_TILE, d), tok)] + out_specs
        out_shapes = [jax.ShapeDtypeStruct((b, ntok, d), F32)] + out_shapes
        args = (x1, y, pmod2, ln_g.reshape(1, d), ln_b.reshape(1, d), mod2, w, cos_t, sin_t)
    outs = pl.pallas_call(
        functools.partial(_proj_kernel, prev is not None),
        grid=(b, nt),
        in_specs=in_specs,
        out_specs=out_specs,
        out_shape=out_shapes,
        compiler_params=_params("arbitrary", "arbitrary"),
        name="proj",
    )(*args)
    return (xs, outs) if prev is None else (outs[0], outs[1:])


def _log_sigmoid(z):
    return -(jnp.maximum(-z, 0.0) + jnp.log(1.0 + jnp.exp(-jnp.abs(z))))


def _gla_kernel(qf, kf, vf, rf, qb, kb, vb, rb, wd_ref, bd_ref, of_ref, ob_ref, *st_refs):
    n = pl.program_id(1)

    @pl.when(n == 0)
    def _():
        for st_ref in st_refs:
            st_ref[...] = jnp.zeros_like(st_ref)

    c = GLA_CHUNK
    row = lax.broadcasted_iota(jnp.int32, (c, c), 0)
    col = lax.broadcasted_iota(jnp.int32, (c, c), 1)
    lane = lax.broadcasted_iota(jnp.int32, (1, LANES), 1)
    head_mask = (lane < GLA_DK, lane >= GLA_DK)

    dirs = ((qf, kf, vf, rf, of_ref), (qb, kb, vb, rb, ob_ref))
    chains = list(itertools.product(range(qf.shape[0]), range(2)))
    causal = [(row >= col), (row <= col)]
    tri = [jnp.where(cz, 1.0, 0.0).astype(F32) for cz in causal]
    pairs = range(GLA_HEADS // 2)
    psl = [slice(p * LANES, (p + 1) * LANES) for p in pairs]

    z = [_mm(dirs[d][3][bb].astype(BF16), wd_ref[0, d]) + bd_ref[0, d] for bb, d in chains]
    cum = [jnp.dot(tri[d], _log_sigmoid(zc) * (1.0 / GLA_TAU), precision=HIGHEST,
                   preferred_element_type=F32) for (bb, d), zc in zip(chains, z)]
    q_in, k_in, k_st, dec = [], [], [], []
    for (bb, d), cm in zip(chains, cum):
        last = cm[c - 1:c] if d == 0 else cm[0:1]
        k = dirs[d][1][bb]
        q_in.append(dirs[d][0][bb] * jnp.exp(cm))
        k_in.append((k * jnp.exp(-cm)).astype(BF16))
        k_st.append(k * jnp.exp(last - cm))
        dec.append(jnp.exp(last))
    att = []
    for ci in range(len(chains)):
        for p in pairs:
            q2 = q_in[ci][:, psl[p]]
            lhs = jnp.concatenate([jnp.where(head_mask[0], q2, 0.0),
                                   jnp.where(head_mask[1], q2, 0.0)], axis=0).astype(BF16)
            att.append(_nt(lhs, k_in[ci][:, psl[p]]))
    for ci, (bb, d) in enumerate(chains):
        v = dirs[d][2][bb]
        for p in pairs:
            q2b = q_in[ci][:, psl[p]].astype(BF16)
            for hh in range(2):
                h = 2 * p + hh
                a = jnp.where(causal[d], att[ci * len(pairs) + p][hh * c:(hh + 1) * c], 0.0).astype(BF16)
                st = st_refs[(bb * 2 + d) * GLA_HEADS + h][...]
                dirs[d][4][bb, :, h * GLA_DV:(h + 1) * GLA_DV] = (
                    _mm(a, v[:, h * GLA_DV:(h + 1) * GLA_DV]) + _nt(q2b, st.astype(BF16)))
    for ci, (bb, d) in enumerate(chains):
        v = dirs[d][2][bb]
        for p in pairs:
            for hh in range(2):
                h = 2 * p + hh
                st_ref = st_refs[(bb * 2 + d) * GLA_HEADS + h]
                kh = jnp.where(head_mask[hh], k_st[ci][:, psl[p]], 0.0).astype(BF16)
                st_ref[...] = st_ref[...] * dec[ci][:, psl[p]] + _tn(v[:, h * GLA_DV:(h + 1) * GLA_DV], kh)


def _gla(gq, gk, gv, gr, wd, bd, layer):
    b, ntok, _ = gq.shape
    nch = ntok // GLA_CHUNK
    nctx = CTX_LEN // GLA_CHUNK
    nb = GLA_BATCH if b % GLA_BATCH == 0 else 1

    def fwd(bi, n):
        return (bi, n, 0)

    def bwd(bi, n):
        return (bi, jnp.where(n < nctx, nctx - 1 - n, nch + nctx - 1 - n), 0)

    def specs(im):
        return [pl.BlockSpec((nb, GLA_CHUNK, 256), im), pl.BlockSpec((nb, GLA_CHUNK, 256), im),
                pl.BlockSpec((nb, GLA_CHUNK, 512), im), pl.BlockSpec((nb, GLA_CHUNK, 128), im)]

    return pl.pallas_call(
        _gla_kernel,
        grid=(b // nb, nch),
        in_specs=specs(fwd) + specs(bwd) + [
            pl.BlockSpec((1, 2, LANES, 256), lambda bi, n: (layer, 0, 0, 0)),
            pl.BlockSpec((1, 2, 1, 256), lambda bi, n: (layer, 0, 0, 0)),
        ],
        out_specs=[pl.BlockSpec((nb, GLA_CHUNK, 512), fwd), pl.BlockSpec((nb, GLA_CHUNK, 512), bwd)],
        out_shape=[jax.ShapeDtypeStruct((b, ntok, 512), F32)] * 2,
        scratch_shapes=[pltpu.VMEM((GLA_DV, LANES), F32)] * (nb * 2 * GLA_HEADS),
        compiler_params=_params("arbitrary", "arbitrary"),
        name="gla",
    )(gq, gk, gv, gr, gq, gk, gv, gr, wd, bd)


DIFF_KC = 256
DIFF_BATCH = 2


def _diff_kernel(lam_init, q_ref, k_ref, v_ref, lq_ref, lk_ref, g_ref, o_ref, s_even, s_odd):
    t = pl.program_id(1)
    tq = q_ref.shape[1]
    nkc = k_ref.shape[1] // DIFF_KC
    lane = lax.broadcasted_iota(jnp.int32, (1, LANES), 1)
    comp0 = (lane // (DIFF_DH // 2)) % 2 == 0
    lql = lq_ref[...] * lk_ref[...]
    lam = (jnp.exp(jnp.sum(lql[0:1], axis=1, keepdims=True))
           - jnp.exp(jnp.sum(lql[1:2], axis=1, keepdims=True)) + lam_init)
    s_bufs = (s_even, s_odd)

    units = list(itertools.product(range(q_ref.shape[0]), range(DIFF_HEADS)))

    def stacked_q(u):
        bb, h = units[u]
        q = q_ref[bb, :, h * LANES:(h + 1) * LANES]
        zero = jnp.zeros_like(q)
        return jnp.concatenate([jnp.where(comp0, q, zero), jnp.where(comp0, zero, q)], axis=0)

    def score_chunk(u, qq, ci, m8):
        bb, h = units[u]
        s = _nt(k_ref[bb, ci * DIFF_KC:(ci + 1) * DIFF_KC, h * LANES:(h + 1) * LANES], qq)
        s_bufs[u % 2][ci] = s
        mc = jnp.max(s.reshape(DIFF_KC // 8, 8, 2 * tq), axis=0)
        return mc if m8 is None else jnp.maximum(m8, mc)

    def value_chunk(u, mrow, ci, acc):
        bb, h = units[u]
        p = jnp.exp2(s_bufs[u % 2][ci] - mrow).astype(BF16)
        part = _mm(v_ref[bb, h * DIFF_VROWS:(h + 1) * DIFF_VROWS, ci * DIFF_KC:(ci + 1) * DIFF_KC], p)
        return part if acc is None else acc + part

    def finish(u, acc):
        bb, h = units[u]
        o = acc[:DIFF_DV] / acc[DIFF_DV:DIFF_DV + 1]
        o = o[:, :tq] - lam * o[:, tq:]
        o = o * lax.rsqrt(jnp.mean(o * o, axis=0, keepdims=True) + RMS_EPS) * g_ref[...] * (1.0 - lam_init)
        o_ref[bb, :, h * DIFF_DV:(h + 1) * DIFF_DV] = o.T.astype(BF16)

    def attend(nch):
        qq = stacked_q(0)
        m8 = None
        for ci in range(nch):
            m8 = score_chunk(0, qq, ci, m8)
        for u in range(1, len(units) + 1):
            mrow = jnp.max(m8, axis=0, keepdims=True)
            if u < len(units):
                qq = stacked_q(u)
            acc = m8 = None
            for ci in range(nch):
                if u < len(units):
                    m8 = score_chunk(u, qq, ci, m8)
                acc = value_chunk(u - 1, mrow, ci, acc)
            finish(u - 1, acc)

    @pl.when(t == 0)
    def _():
        attend(1)

    @pl.when(t > 0)
    def _():
        attend(nkc)


def _diff(dq, dk, dv, lam_q, lam_k, norm_g, lam_init):
    b, ntok, w = dq.shape
    nt = ntok // TOK_TILE
    nb = DIFF_BATCH if b % DIFF_BATCH == 0 else 1
    scores = pltpu.VMEM((ntok // DIFF_KC, DIFF_KC, 2 * TOK_TILE), F32)
    return pl.pallas_call(
        functools.partial(_diff_kernel, lam_init),
        grid=(b // nb, nt),
        in_specs=[
            pl.BlockSpec((nb, TOK_TILE, w), lambda bi, t: (bi, t, 0)),
            pl.BlockSpec((nb, ntok, w), lambda bi, t: (bi, 0, 0), pipeline_mode=pl.Buffered(1)),
            pl.BlockSpec((nb, DIFF_HEADS * DIFF_VROWS, ntok), lambda bi, t: (bi, 0, 0),
                         pipeline_mode=pl.Buffered(1)),
            pl.BlockSpec((2, DIFF_DH), lambda bi, t: (0, 0)),
            pl.BlockSpec((2, DIFF_DH), lambda bi, t: (0, 0)),
            pl.BlockSpec((DIFF_DV, 1), lambda bi, t: (0, 0)),
        ],
        out_specs=pl.BlockSpec((nb, TOK_TILE, w), lambda bi, t: (bi, t, 0)),
        out_shape=jax.ShapeDtypeStruct((b, ntok, w), BF16),
        scratch_shapes=[scores, scores],
        compiler_params=_params("arbitrary", "arbitrary"),
        name="diff",
    )(dq, dk, dv, lam_q, lam_k, norm_g.reshape(DIFF_DV, 1))


def _na_kernel(rows, q_ref, k_ref, v_ref, b_ref, o_ref):
    t = pl.program_id(1)
    nctx = CTX_LEN // NA_TILE
    r = t - nctx
    rs = jnp.clip(r - NA_WIN_H // 2, 0, rows - NA_WIN_H)
    start = pl.multiple_of(CTX_LEN + rs * GRID_W, GRID_W)
    nloc = NA_WIN_H * GRID_W
    lane = lax.broadcasted_iota(jnp.int32, (1, LANES), 1)
    first = lane < NA_DH
    chains = list(itertools.product(range(q_ref.shape[0]), range(NA_HEADS // 2)))
    psl = [slice(p * LANES, (p + 1) * LANES) for p in range(NA_HEADS // 2)]
    scores = []
    for bb, p in chains:
        q2 = q_ref[bb, :, psl[p]]
        zero = jnp.zeros_like(q2)
        lhs = jnp.concatenate([jnp.where(first, q2, zero), jnp.where(first, zero, q2)], axis=0)
        s_loc = _nt(lhs, k_ref[bb, pl.ds(start, nloc), psl[p]]) + b_ref[p]
        s_ctx = _nt(lhs, k_ref[bb, 0:CTX_LEN, psl[p]])
        scores.append((s_loc, s_ctx))
    probs = []
    for s_loc, s_ctx in scores:
        m = jnp.maximum(jnp.max(s_loc, axis=1, keepdims=True), jnp.max(s_ctx, axis=1, keepdims=True))
        p_loc = jnp.exp2(s_loc - m)
        p_ctx = jnp.exp2(s_ctx - m)
        l = jnp.sum(p_loc, axis=1, keepdims=True) + jnp.sum(p_ctx, axis=1, keepdims=True)
        probs.append((p_loc.astype(BF16), p_ctx.astype(BF16), l))
    for (bb, p), (p_loc, p_ctx, l) in zip(chains, probs):
        o = (_mm(p_loc, v_ref[bb, pl.ds(start, nloc), psl[p]]) + _mm(p_ctx, v_ref[bb, 0:CTX_LEN, psl[p]])) / l
        o_ref[bb, :, psl[p]] = jnp.where(first, o[:NA_TILE], o[NA_TILE:]).astype(BF16)


def _na(nq, nk, nv, bias):
    b, ntok, w = nq.shape
    nt = ntok // NA_TILE
    nctx = CTX_LEN // NA_TILE
    rows = (ntok - CTX_LEN) // GRID_W
    nb = NA_BATCH if b % NA_BATCH == 0 else 1

    def cfg(bi, t):
        r = t - nctx
        return (jnp.where(t < nctx, NA_WIN_H, r - jnp.clip(r - NA_WIN_H // 2, 0, rows - NA_WIN_H)), 0, 0)

    kv_spec = pl.BlockSpec((nb, ntok, w), lambda bi, t: (bi, 0, 0), pipeline_mode=pl.Buffered(1))
    return pl.pallas_call(
        functools.partial(_na_kernel, rows),
        grid=(b // nb, nt),
        in_specs=[
            pl.BlockSpec((nb, NA_TILE, w), lambda bi, t: (bi, t, 0)),
            kv_spec, kv_spec,
            pl.BlockSpec((NA_HEADS // 2, 2 * NA_TILE, NA_WIN_H * GRID_W), cfg),
        ],
        out_specs=pl.BlockSpec((nb, NA_TILE, w), lambda bi, t: (bi, t, 0)),
        out_shape=jax.ShapeDtypeStruct((b, ntok, w), BF16),
        compiler_params=_params("arbitrary", "arbitrary"),
        name="na",
    )(nq, nk, nv, bias)


def _na_bias(rpb):
    qc = np.arange(GRID_W)[:, None]
    kc = np.arange(GRID_W)[None, :]
    col_idx = np.clip(kc - qc + NA_WIN_W - 1, 0, 2 * NA_WIN_W - 2)
    cstart = np.clip(qc - NA_WIN_W // 2, 0, GRID_W - NA_WIN_W)
    win = (kc >= cstart) & (kc < cstart + NA_WIN_W)
    toep = jnp.where(win[None, None], rpb.astype(F32)[..., col_idx] * LOG2E, NEG_BIG)
    toep = toep.transpose(0, 2, 1, 3)
    cfgs = [toep[:, :, NA_WIN_H - 1 - c:2 * NA_WIN_H - 1 - c] for c in range(NA_WIN_H)]
    cfgs.append(jnp.full(cfgs[0].shape, NEG_BIG, F32))
    return jnp.stack(cfgs, axis=0).reshape(-1, 2 * GRID_W, NA_WIN_H * GRID_W)


def _layernorm(x, g, b):
    mu = jnp.mean(x, axis=1, keepdims=True)
    xc = x - mu
    var = jnp.mean(xc * xc, axis=1, keepdims=True)
    return xc * lax.rsqrt(var + LN_EPS) * g + b


def _route(logits, bias):
    aff = _sigmoid(logits)
    sel = aff + bias
    srow = [sel[e:e + 1] for e in range(N_EXPERTS)]
    arow = [aff[e:e + 1] for e in range(N_EXPERTS)]
    gscore = []
    for g in range(N_GROUPS):
        a0, a1, a2, a3 = srow[4 * g:4 * g + 4]
        hi01, lo01 = jnp.maximum(a0, a1), jnp.minimum(a0, a1)
        hi23, lo23 = jnp.maximum(a2, a3), jnp.minimum(a2, a3)
        top1 = jnp.maximum(hi01, hi23)
        top2 = jnp.maximum(jnp.minimum(hi01, hi23), jnp.maximum(lo01, lo23))
        gscore.append(top1 + top2)
    best = jnp.zeros_like(gscore[0])
    bestv = gscore[0]
    for g in range(1, N_GROUPS):
        better = gscore[g] > bestv
        best = jnp.where(better, float(g), best)
        bestv = jnp.where(better, gscore[g], bestv)

    def pick(rows_, i):
        out = rows_[i]
        for g in range(1, N_GROUPS):
            out = jnp.where(best == float(g), rows_[4 * g + i], out)
        return out

    s4 = [pick(srow, i) for i in range(4)]
    f4 = [pick(arow, i) for i in range(4)]
    chosen = []
    for i in range(4):
        rank = jnp.zeros_like(best)
        for j in range(4):
            if j == i:
                continue
            ahead = (s4[j] > s4[i]) | ((s4[j] == s4[i]) & (j < i))
            rank = rank + jnp.where(ahead, 1.0, 0.0)
        chosen.append(rank < 2.0)
    c0, c1, c2, c3 = chosen
    pidx = jnp.where(c0, jnp.where(c1, 0.0, jnp.where(c2, 1.0, 2.0)),
                     jnp.where(c1, jnp.where(c2, 3.0, 4.0), 5.0))
    a_lo = jnp.where(c0, f4[0], jnp.where(c1, f4[1], f4[2]))
    a_hi = jnp.where(c3, f4[3], jnp.where(c2, f4[2], f4[1]))
    den = a_lo + a_hi
    return best * float(N_PAIRS) + pidx, a_lo / den, a_hi / den


def _merge_kernel(x_ref, of_ref, ob_ref, gsg_ref, do_ref, no_ref, sg_ref, mod_ref, gg_ref,
                  wb_ref, wo_ref, lng_ref, lnb_ref, wr_ref, br_ref,
                  x1_ref, h2_ref, route_ref, cnt_ref, run_ref):
    first_step = jnp.logical_and(pl.program_id(0) == 0, pl.program_id(1) == 0)

    @pl.when(first_step)
    def _():
        run_ref[...] = jnp.zeros_like(run_ref)

    nb, tm = x_ref.shape[0], x_ref.shape[1]
    tiles = range(nb)
    mods = [mod_ref[bb, 0] for bb in tiles]
    gg = gg_ref[...]

    branches = []
    for bb in tiles:
        o = of_ref[bb] + ob_ref[bb]
        parts = []
        for h in range(GLA_HEADS):
            oh = o[:, h * GLA_DV:(h + 1) * GLA_DV]
            parts.append(oh * lax.rsqrt(jnp.mean(oh * oh, axis=1, keepdims=True) + RMS_EPS) * gg)
        gla = (jnp.concatenate(parts, axis=1) * gsg_ref[bb].astype(F32)).astype(BF16)
        branches.append((gla, do_ref[bb], no_ref[bb]))
    ys = [None] * nb
    for i in range(3):
        for bb in tiles:
            term = sg_ref[bb, :, i * D_MODEL:(i + 1) * D_MODEL].astype(F32) * _mm(branches[bb][i], wb_ref[0, i])
            ys[bb] = term if i == 0 else ys[bb] + term
    ys = [_mm(y.astype(BF16), wo_ref[0]) for y in ys]
    h2s = []
    for bb in tiles:
        x1 = _layernorm(DEEPNORM_ALPHA * x_ref[bb] + mods[bb][2:3] * ys[bb], lng_ref[...], lnb_ref[...])
        x1_ref[bb] = x1
        h2 = x1 * (1.0 + mods[bb][4:5]) + mods[bb][3:4]
        h2_ref[bb, :, :D_MODEL] = h2
        h2s.append(h2)
    logits = [lax.dot_general(wr_ref[...], h2, (((1,), (1,)), ((), ())),
                              precision=HIGHEST, preferred_element_type=F32) for h2 in h2s]
    routes = [_route(lg, br_ref[...]) for lg in logits]

    srows = lax.broadcasted_iota(jnp.int32, (SEG_ROWS, tm), 0).astype(F32)
    ii = lax.broadcasted_iota(jnp.int32, (tm, tm), 0)
    jj = lax.broadcasted_iota(jnp.int32, (tm, tm), 1)
    before = jnp.where(ii < jj, 1.0, 0.0).astype(BF16)
    onehots = [jnp.where(srows == seg, 1.0, 0.0) for seg, _, _ in routes]
    prefixes = [_mm(oh.astype(BF16), before) for oh in onehots]
    run = run_ref[...]
    for bb in tiles:
        seg, w_lo, w_hi = routes[bb]
        rank = jnp.sum(onehots[bb] * (prefixes[bb] + run[:, 0:1]), axis=0, keepdims=True)
        run = run + jnp.sum(onehots[bb], axis=1, keepdims=True)
        zrow = jnp.zeros_like(seg)
        route = jnp.concatenate([seg, rank, w_lo, w_hi, zrow, zrow, zrow, zrow], axis=0)
        route_ref[bb, 0] = route
        wide = jnp.concatenate([route, jnp.zeros((LANES - 8, tm), F32)], axis=0)
        h2_ref[bb, :, D_MODEL:] = wide.T
    run_ref[...] = run
    cnt_ref[...] = run


def _merge(xs, o_f, o_b, gsg, d_o, n_o, sg, mod2, gla_g, wb, wo, ln_g, ln_b, wr_t, br, layer):
    b, ntok, d = xs.shape
    nt = ntok // TOK_TILE
    nb = MERGE_BATCH if b % MERGE_BATCH == 0 else 1
    tok = lambda bi, ti: (bi, ti, 0)
    const2 = lambda bi, ti: (0, 0)

    def tokspec(wd):
        return pl.BlockSpec((nb, TOK_TILE, wd), tok)

    return pl.pallas_call(
        _merge_kernel,
        grid=(b // nb, nt),
        in_specs=[
            tokspec(d), tokspec(512), tokspec(512), tokspec(512), tokspec(512), tokspec(512),
            tokspec(3 * d),
            pl.BlockSpec((nb, 1, 8, d), lambda bi, ti: (bi, jnp.minimum(ti, 1), 0, 0)),
            pl.BlockSpec((1, GLA_DV), const2),
            pl.BlockSpec((1, 3, BRANCH_W, d), lambda bi, ti: (layer, 0, 0, 0)),
            pl.BlockSpec((1, d, d), lambda bi, ti: (layer, 0, 0)),
            pl.BlockSpec((1, d), const2),
            pl.BlockSpec((1, d), const2),
            pl.BlockSpec((N_EXPERTS, d), const2),
            pl.BlockSpec((N_EXPERTS, 1), const2),
        ],
        out_specs=[
            tokspec(d), tokspec(d + LANES),
            pl.BlockSpec((nb, 1, 8, TOK_TILE), lambda bi, ti: (bi, ti, 0, 0)),
            pl.BlockSpec((SEG_ROWS, LANES), const2),
        ],
        out_shape=[
            jax.ShapeDtypeStruct((b, ntok, d), F32),
            jax.ShapeDtypeStruct((b, ntok, d + LANES), F32),
            jax.ShapeDtypeStruct((b, nt, 8, TOK_TILE), F32),
            jax.ShapeDtypeStruct((SEG_ROWS, LANES), F32),
        ],
        scratch_shapes=[pltpu.VMEM((SEG_ROWS, LANES), F32)],
        compiler_params=_params("arbitrary", "arbitrary"),
        name="merge",
    )(xs, o_f, o_b, gsg, d_o, n_o, sg, mod2.reshape(b, 2, 8, d), gla_g.reshape(1, GLA_DV), wb, wo,
      ln_g.reshape(1, d), ln_b.reshape(1, d), wr_t, br.reshape(N_EXPERTS, 1))


def _moe_kernel(dest, meta, h_hbm, wg1, wu1, wd1, wg2, wu2, wd2, y_hbm,
                src, xbuf, ybuf, sem_in, sem_out):
    i = pl.program_id(0)
    ntile = pl.num_programs(0)
    n_used = meta[2 * ntile]
    slot = i % 2
    tm = MOE_TILE

    def n_real(tile):
        return meta[2 * ntile + 1 + tile]

    def gather(tile, sl):
        def body(j, carry):
            tok = src[tile * tm + j]
            pltpu.make_async_copy(h_hbm.at[pl.ds(tok, 1)], xbuf.at[sl, pl.ds(j, 1)], sem_in.at[sl]).start()
            return carry
        lax.fori_loop(0, tm, body, 0, unroll=8)

    def wait_in(sl):
        pltpu.make_async_copy(h_hbm.at[pl.ds(0, tm)], xbuf.at[sl], sem_in.at[sl]).wait()

    def row_out(tile, sl, j):
        tok = src[tile * tm + j]
        return pltpu.make_async_copy(ybuf.at[sl, pl.ds(j, 1)], y_hbm.at[pl.ds(tok, 1)], sem_out.at[sl])

    def scatter(tile, sl):
        nr = n_real(tile)

        def body(j, carry):
            row_out(tile, sl, j).start()
            return carry

        @pl.when(nr == tm)
        def _():
            for j in range(tm):
                row_out(tile, sl, j).start()

        @pl.when(nr < tm)
        def _():
            lax.fori_loop(0, nr, body, 0)

    def wait_out(tile, sl):
        nr = n_real(tile)

        @pl.when(nr == tm)
        def _():
            pltpu.make_async_copy(ybuf.at[sl], y_hbm.at[pl.ds(0, tm)], sem_out.at[sl]).wait()

        @pl.when(nr < tm)
        def _():
            def body(j, carry):
                row_out(tile, sl, j).wait()
                return carry
            lax.fori_loop(0, nr, body, 0)

    @pl.when(i == 0)
    def _():
        def clear(j, carry):
            src[j] = 0
            return carry
        for s in range(N_SEG):
            lax.fori_loop(meta[3 * ntile + 1 + 2 * s], meta[3 * ntile + 2 + 2 * s], clear, 0)
        lax.fori_loop(0, tm, lambda j, carry: clear(n_used * tm + j, carry), 0, unroll=8)

        def place(j, carry):
            src[dest[j]] = j
            return carry
        lax.fori_loop(0, dest.shape[0], place, 0, unroll=8)
        gather(0, 0)

    @pl.when(i < n_used)
    def _():
        wait_in(slot)

        @pl.when(i >= 2)
        def _():
            wait_out(i - 2, slot)

        for j in range(tm):
            tok = src[(i + 1) * tm + j]
            pltpu.make_async_copy(h_hbm.at[pl.ds(tok, 1)], xbuf.at[1 - slot, pl.ds(j, 1)],
                                  sem_in.at[1 - slot]).start()

        x = xbuf[slot, :, :D_MODEL].astype(BF16)
        w = xbuf[slot, :, D_MODEL:]

        a1, u1 = _mm(x, wg1[0]), _mm(x, wu1[0])
        a2, u2 = _mm(x, wg2[0]), _mm(x, wu2[0])
        y1 = _mm((a1 * _sigmoid(a1) * u1).astype(BF16), wd1[0])
        y2 = _mm((a2 * _sigmoid(a2) * u2).astype(BF16), wd2[0])
        ybuf[slot] = w[:, 2:3] * y1 + w[:, 3:4] * y2
        scatter(i, slot)

    @pl.when(i == ntile - 1)
    def _():
        wait_in(n_used % 2)
        wait_out(n_used - 1, (n_used - 1) % 2)

        @pl.when(n_used >= 2)
        def _():
            wait_out(n_used - 2, n_used % 2)


def _moe(h2, dest, meta, wg, wu, wd, layer):
    t, dw = h2.shape
    d = dw - LANES
    ntile = (meta.shape[0] - 1 - 2 * N_SEG) // 3
    lo = lambda i, s, m: (layer * N_EXPERTS + m[2 * i], 0, 0)
    hi = lambda i, s, m: (layer * N_EXPERTS + m[2 * i + 1], 0, 0)
    grid_spec = pltpu.PrefetchScalarGridSpec(
        num_scalar_prefetch=2,
        grid=(ntile,),
        in_specs=[
            pl.BlockSpec(memory_space=pl.ANY),
            pl.BlockSpec((1, d, D_EXPERT), lo), pl.BlockSpec((1, d, D_EXPERT), lo),
            pl.BlockSpec((1, D_EXPERT, d), lo),
            pl.BlockSpec((1, d, D_EXPERT), hi), pl.BlockSpec((1, d, D_EXPERT), hi),
            pl.BlockSpec((1, D_EXPERT, d), hi),
        ],
        out_specs=pl.BlockSpec(memory_space=pl.ANY),
        scratch_shapes=[pltpu.SMEM(((ntile + 1) * MOE_TILE,), jnp.int32),
                        pltpu.VMEM((2, MOE_TILE, dw), F32), pltpu.VMEM((2, MOE_TILE, d), F32),
                        pltpu.SemaphoreType.DMA((2,)), pltpu.SemaphoreType.DMA((2,))],
    )
    return pl.pallas_call(
        _moe_kernel,
        grid_spec=grid_spec,
        out_shape=jax.ShapeDtypeStruct((t, d), F32),
        compiler_params=_params("arbitrary"),
        name="moe",
    )(dest, meta, h2, wg, wu, wd, wg, wu, wd)


def _lookup(idx, table):
    n = table.shape[0]
    hit = idx[:, None] == jnp.arange(n, dtype=jnp.int32)[None, :]
    return jnp.sum(jnp.where(hit, table[None, :], 0), axis=1)


def _dispatch_plan(route, counts, t):
    ntile = (t + N_SEG * (MOE_TILE - 1)) // MOE_TILE
    seg = route[:, :, 0, :].reshape(t).astype(jnp.int32)
    rank = route[:, :, 1, :].reshape(t).astype(jnp.int32)
    cnt = counts[:N_SEG, 0].astype(jnp.int32)
    seg_tiles = (cnt + MOE_TILE - 1) // MOE_TILE
    upto = jnp.arange(N_SEG)[None, :] <= jnp.arange(N_SEG)[:, None]
    tile_end = jnp.sum(jnp.where(upto, seg_tiles[None, :], 0), axis=1)
    first_tile = tile_end - seg_tiles
    dest = _lookup(seg, first_tile * MOE_TILE) + rank
    n_used = tile_end[-1]
    tiles = jnp.arange(ntile, dtype=jnp.int32)
    tile_seg = jnp.sum((tile_end[None, :] <= jnp.minimum(tiles, n_used - 1)[:, None]).astype(jnp.int32), axis=1)
    tile_seg = jnp.minimum(tile_seg, N_SEG - 1)
    base = (tile_seg // N_PAIRS) * EXPERTS_PER_GROUP
    pair = tile_seg % N_PAIRS
    e_lo = base + _lookup(pair, jnp.asarray(PAIR_LO, jnp.int32))
    e_hi = base + _lookup(pair, jnp.asarray(PAIR_HI, jnp.int32))
    left = _lookup(tile_seg, cnt) - (tiles - _lookup(tile_seg, first_tile)) * MOE_TILE
    n_real = jnp.where(tiles < n_used, jnp.clip(left, 0, MOE_TILE), 0)
    pad_lo = first_tile * MOE_TILE + cnt
    pad_hi = tile_end * MOE_TILE
    meta = jnp.concatenate([jnp.stack([e_lo, e_hi], axis=1).reshape(-1), n_used[None], n_real,
                            jnp.stack([pad_lo, pad_hi], axis=1).reshape(-1)]).astype(jnp.int32)
    return dest, meta


def _final_kernel(x_ref, y_ref, mod_ref, g_ref, b_ref, o_ref):
    mod = mod_ref[0]
    o_ref[0] = _layernorm(DEEPNORM_ALPHA * x_ref[0] + mod[5:6] * y_ref[...], g_ref[...], b_ref[...])


def _final(x1, y, mod2, ln_g, ln_b, latents_only):
    b, ntok, d = x1.shape
    nt = ntok // TOK_TILE
    skip = CTX_LEN // TOK_TILE if latents_only else 0
    return pl.pallas_call(
        _final_kernel,
        grid=(b, nt - skip),
        in_specs=[
            pl.BlockSpec((1, TOK_TILE, d), lambda bi, ti: (bi, ti + skip, 0)),
            pl.BlockSpec((TOK_TILE, d), lambda bi, ti: (bi * nt + ti + skip, 0)),
            pl.BlockSpec((1, 8, d), lambda bi, ti: (2 * bi + jnp.minimum(ti + skip, 1), 0, 0)),
            pl.BlockSpec((1, d), lambda bi, ti: (0, 0)),
            pl.BlockSpec((1, d), lambda bi, ti: (0, 0)),
        ],
        out_specs=pl.BlockSpec((1, TOK_TILE, d), lambda bi, ti: (bi, ti, 0)),
        out_shape=jax.ShapeDtypeStruct((b, ntok - skip * TOK_TILE, d), F32),
        compiler_params=_params("arbitrary", "arbitrary"),
        name="final_ln",
    )(x1, y, mod2, ln_g.reshape(1, d), ln_b.reshape(1, d))


def _rotary_order(w):
    lead = w.shape[:-1]
    half = DIFF_DH // 2
    return w.reshape(lead + (DIFF_HEADS, 2, 2, half)).swapaxes(-2, -3).reshape(lead + (DIFF_HEADS * 2 * DIFF_DH,))


def _pack_w_in(w_in):
    splits = np.cumsum([256, 256, 512, 512, 32, 512, 512, 512, 512, 512, 512])
    gq, gk, gv, gg, gr, dq, dk, dv, nq, nk, nv, sg = jnp.split(w_in.astype(BF16), splits, axis=-1)
    gr = jnp.pad(gr, ((0, 0), (0, 0), (0, LANES - 2 * GLA_RANK)))
    return jnp.concatenate([gq, gk, gv, gg, gr, _rotary_order(dq), _rotary_order(dk), dv, nq, nk, nv, sg],
                           axis=-1)


def _rope_tables(n_lat):
    t = jnp.arange(n_lat)
    row = (t // GRID_W).astype(F32)
    col = (t % GRID_W).astype(F32)
    n_freq = DIFF_DH // 4
    inv = ROPE_BASE ** (-jnp.arange(n_freq, dtype=F32) / n_freq)
    ang = jnp.concatenate([row[:, None] * inv, col[:, None] * inv], -1)
    cos, sin = jnp.cos(ang), jnp.sin(ang)
    cos_t = jnp.concatenate([cos] * 4, axis=1)
    sin_t = jnp.concatenate([-sin, -sin, sin, sin], axis=1)
    cos_t = jnp.concatenate([jnp.ones((CTX_LEN, LANES), F32), cos_t], axis=0)
    sin_t = jnp.concatenate([jnp.zeros((CTX_LEN, LANES), F32), sin_t], axis=0)
    return cos_t, sin_t


def _pack_decay(w_decay, b_decay):
    depth = w_decay.shape[0]
    wd = jnp.zeros((depth, 2, LANES, GLA_HEADS * GLA_DK), F32)
    for d in range(2):
        wd = wd.at[:, d, d * GLA_RANK:(d + 1) * GLA_RANK].set(w_decay[:, d])
    return wd.astype(BF16), b_decay.reshape(depth, 2, 1, GLA_HEADS * GLA_DK)


def kernel(x, c, ctx, c_ctx, w_ada, b_ada, w_in, gla_w_decay, gla_b_decay, gla_norm_g, diff_lam_q,
           diff_lam_k, diff_norm_g, na_rpb, w_branch, w_o, ln_g, ln_b, w_router, b_router,
           w_exp_gate, w_exp_up, w_exp_down):
    b, l, d = x.shape
    lc = ctx.shape[1]
    ntok = lc + l
    t = b * ntok

    w_in_p = _pack_w_in(w_in)
    wd_p, bd_p = _pack_decay(gla_w_decay, gla_b_decay)
    cos_t, sin_t = _rope_tables(l)
    wb = w_branch.astype(BF16)
    wo = w_o.astype(BF16)
    wg = w_exp_gate.astype(BF16).reshape(DEPTH * N_EXPERTS, d, D_EXPERT)
    wu = w_exp_up.astype(BF16).reshape(DEPTH * N_EXPERTS, d, D_EXPERT)
    wdn = w_exp_down.astype(BF16).reshape(DEPTH * N_EXPERTS, D_EXPERT, d)
    wr_t = w_router.T

    cs = jnp.concatenate([c, c_ctx[None], jnp.zeros((16 - b - 1, d), F32)], axis=0)
    mods = _ada(cs, w_ada, b_ada).reshape(DEPTH, 16, 6, d)

    xs = jnp.concatenate([ctx, x], axis=1)
    prev = None
    for layer in range(DEPTH):
        lam_init = 0.8 - 0.6 * math.exp(-0.3 * layer)
        m_lat = mods[layer, :b]
        m_ctx = jnp.broadcast_to(mods[layer, b][None], (b, 6, d))
        mod2 = jnp.stack([m_ctx, m_lat], axis=1).reshape(2 * b, 6, d)
        mod2 = jnp.pad(mod2, ((0, 0), (0, 2), (0, 0)))

        xs, (gq, gk, gr, gv, gsg, dq, dk, dv, nq, nk, nv, sg) = _proj(
            xs, mod2, w_in_p, cos_t, sin_t, layer, prev)
        o_f, o_b = _gla(gq, gk, gv, gr, wd_p, bd_p, layer)
        d_o = _diff(dq, dk, dv, diff_lam_q[layer], diff_lam_k[layer], diff_norm_g[layer], lam_init)
        n_o = _na(nq, nk, nv, _na_bias(na_rpb[layer]))
        x1, h2, route, counts = _merge(xs, o_f, o_b, gsg, d_o, n_o, sg, mod2, gla_norm_g[layer],
                                       wb, wo, ln_g[layer, 0], ln_b[layer, 0], wr_t, b_router, layer)
        dest, meta = _dispatch_plan(route, counts, t)
        y = _moe(h2.reshape(t, d + LANES), dest, meta, wg, wu, wdn, layer)
        prev = (x1, y, mod2, ln_g[layer, 1], ln_b[layer, 1])
    return _final(*prev, latents_only=True)
```

```python
import functools
import itertools
import math

import jax
import jax.numpy as jnp
import numpy as np
from jax import lax
from jax.experimental import pallas as pl
from jax.experimental.pallas import tpu as pltpu

F32 = jnp.float32
BF16 = jnp.bfloat16
HIGHEST = lax.Precision.HIGHEST

D_MODEL = 1024
DEPTH = 4
GRID_W = 64
CTX_LEN = 256
BRANCH_W = D_MODEL // 2
GLA_HEADS = 4
GLA_DV = 128
GLA_DK = 64
GLA_RANK = 16
GLA_TAU = 16.0
GLA_CHUNK = 64
GLA_BATCH = 8
DIFF_HEADS = 4
DIFF_DV = 128
DIFF_DH = 64
DIFF_VROWS = DIFF_DV + 16
NA_HEADS = 8
NA_DH = 64
NA_WIN_H = 8
NA_WIN_W = 16
N_EXPERTS = 16
N_GROUPS = 4
EXPERTS_PER_GROUP = 4
D_EXPERT = D_MODEL // 2
ROPE_BASE = 10000.0
LN_EPS = 1e-5
RMS_EPS = 1e-6
NEG_BIG = -1e30
DEEPNORM_ALPHA = (2 * DEPTH) ** 0.25
LOG2E = 1.4426950408889634

LANES = 128
TOK_TILE = 256
NA_TILE = GRID_W
NA_BATCH = 4
MERGE_BATCH = 2
MOE_TILE = 256
N_PAIRS = 6
N_SEG = N_GROUPS * N_PAIRS
SEG_ROWS = 32
PAIR_LO = (0, 0, 0, 1, 1, 2)
PAIR_HI = (1, 2, 3, 2, 3, 3)
VMEM_LIMIT = 56 * 1024 * 1024

W_GQ, W_GK, W_GV, W_GG, W_GR = 0, 256, 512, 1024, 1536
W_DQ, W_DK, W_DV = 1664, 2176, 2688
W_NQ, W_NK, W_NV = 3200, 3712, 4224
W_SG, W_END = 4736, 7808

PF_GQ, PF_GK, PF_GR, PF_W = 0, 256, 512, 640
PB_SG, PB_GV, PB_GSG, PB_DQ, PB_DK, PB_NQ, PB_NK, PB_NV, PB_W = 0, 3072, 3584, 4096, 4608, 5120, 5632, 6144, 6656
HALF = BRANCH_W


def _nt(a, b):
    return lax.dot_general(a, b, (((1,), (1,)), ((), ())), preferred_element_type=F32)


def _tn(a, b):
    return lax.dot_general(a, b, (((0,), (0,)), ((), ())), preferred_element_type=F32)


def _mm(a, b):
    return jnp.dot(a, b, preferred_element_type=F32)


def _sigmoid(x):
    return 1.0 / (1.0 + jnp.exp(-x))


def _params(*sem):
    return pltpu.CompilerParams(dimension_semantics=sem, vmem_limit_bytes=VMEM_LIMIT)


def _ada_kernel(c_ref, w_ref, b_ref, o_ref):
    cs = c_ref[...]
    s = cs * _sigmoid(cs)
    o_ref[0] = jnp.dot(s, w_ref[0], precision=HIGHEST, preferred_element_type=F32) + b_ref[0]


def _ada(cs, w_ada, b_ada):
    depth, d, n = w_ada.shape
    bn = 1536
    return pl.pallas_call(
        _ada_kernel,
        grid=(depth, n // bn),
        in_specs=[
            pl.BlockSpec((cs.shape[0], d), lambda l, j: (0, 0)),
            pl.BlockSpec((1, d, bn), lambda l, j: (l, 0, j)),
            pl.BlockSpec((1, 1, bn), lambda l, j: (l, 0, j)),
        ],
        out_specs=pl.BlockSpec((1, cs.shape[0], bn), lambda l, j: (l, 0, j)),
        out_shape=jax.ShapeDtypeStruct((depth, cs.shape[0], n), F32),
        compiler_params=_params("arbitrary", "arbitrary"),
        name="ada",
    )(cs, w_ada, b_ada.reshape(depth, 1, n))


def _proj_kernel(after_moe, *refs):
    if after_moe:
        (x1_ref, y_ref, pmod_ref, lng_ref, lnb_ref, mod_ref, w_ref, cos_ref, sin_ref, xs_ref,
         pf, pb, dv) = refs
        x = _layernorm(DEEPNORM_ALPHA * x1_ref[0] + pmod_ref[0][5:6] * y_ref[...], lng_ref[...], lnb_ref[...])
        xs_ref[0] = x
    else:
        x_ref, mod_ref, w_ref, cos_ref, sin_ref, pf, pb, dv = refs
        x = x_ref[0]
    mod = mod_ref[0]
    h = (x * (1.0 + mod[1:2]) + mod[0:1]).astype(BF16)

    def mm(lo, hi):
        return _mm(h, w_ref[0, :, lo:hi])

    def put(ref, col, val):
        ref[0, :, col:col + val.shape[1]] = val

    put(pf, PF_GQ, mm(W_GQ, W_GK) * (GLA_DK ** -0.5))
    put(pf, PF_GK, mm(W_GK, W_GV))
    put(pb, PB_GV, mm(W_GV, W_GG).astype(BF16))
    g = mm(W_GG, W_GR)
    put(pb, PB_GSG, (g * _sigmoid(g)).astype(BF16))
    put(pf, PF_GR, mm(W_GR, W_DQ))

    cos = cos_ref[...]
    sin = sin_ref[...]

    def rope(y, scale):
        parts = []
        for i in range(DIFF_HEADS):
            p = y[:, i * LANES:(i + 1) * LANES]
            parts.append(((p * cos + pltpu.roll(p, LANES // 2, 1) * sin) * scale).astype(BF16))
        return jnp.concatenate(parts, axis=1)

    put(pb, PB_DQ, rope(mm(W_DQ, W_DK), (DIFF_DH ** -0.5) * LOG2E))
    put(pb, PB_DK, rope(mm(W_DK, W_DV), 1.0))
    v = mm(W_DV, W_NQ)
    sub = lax.broadcasted_iota(jnp.int32, (DIFF_VROWS - DIFF_DV, v.shape[0]), 0)
    ones_rows = jnp.where(sub == 0, 1.0, 0.0).astype(BF16)
    for i in range(DIFF_HEADS):
        dv[0, i * DIFF_VROWS:i * DIFF_VROWS + DIFF_DV, :] = v[:, i * LANES:(i + 1) * LANES].T.astype(BF16)
        dv[0, i * DIFF_VROWS + DIFF_DV:(i + 1) * DIFF_VROWS, :] = ones_rows
    put(pb, PB_NQ, (mm(W_NQ, W_NK) * ((NA_DH ** -0.5) * LOG2E)).astype(BF16))
    put(pb, PB_NK, mm(W_NK, W_NV).astype(BF16))
    put(pb, PB_NV, mm(W_NV, W_SG).astype(BF16))
    for i in range(3):
        lo = W_SG + i * D_MODEL
        put(pb, PB_SG + i * D_MODEL, _sigmoid(mm(lo, lo + D_MODEL)).astype(BF16))


def _proj(xs, mod2, w, cos_t, sin_t, layer, prev=None):
    b, ntok, d = (xs if prev is None else prev[0]).shape
    nt = ntok // TOK_TILE
    tok = lambda bi, ti: (bi, ti, 0)
    vrows = DIFF_HEADS * DIFF_VROWS

    mod_spec = pl.BlockSpec((1, 8, d), lambda bi, ti: (2 * bi + jnp.minimum(ti, 1), 0, 0))
    vec_spec = pl.BlockSpec((1, d), lambda bi, ti: (0, 0))
    common_specs = [
        mod_spec,
        pl.BlockSpec((1, d, W_END), lambda bi, ti: (layer, 0, 0), pipeline_mode=pl.Buffered(1)),
        pl.BlockSpec((TOK_TILE, LANES), lambda bi, ti: (ti, 0)),
        pl.BlockSpec((TOK_TILE, LANES), lambda bi, ti: (ti, 0)),
    ]
    out_specs = [pl.BlockSpec((1, TOK_TILE, PF_W), tok), pl.BlockSpec((1, TOK_TILE, PB_W), tok),
                 pl.BlockSpec((1, vrows, TOK_TILE), lambda bi, ti: (bi, 0, ti))]
    out_shapes = [jax.ShapeDtypeStruct((b, ntok, PF_W), F32), jax.ShapeDtypeStruct((b, ntok, PB_W), BF16),
                  jax.ShapeDtypeStruct((b, vrows, ntok), BF16)]
    if prev is None:
        in_specs = [pl.BlockSpec((1, TOK_TILE, d), tok)] + common_specs
        args = (xs, mod2, w, cos_t, sin_t)
    else:
        x1, y, pmod2, ln_g, ln_b = prev
        in_specs = [pl.BlockSpec((1, TOK_TILE, d), tok),
                    pl.BlockSpec((TOK_TILE, d), lambda bi, ti: (bi * nt + ti, 0)),
                    mod_spec, vec_spec, vec_spec] + common_specs
        out_specs = [pl.BlockSpec((1, TOK_TILE, d), tok)] + out_specs
        out_shapes = [jax.ShapeDtypeStruct((b, ntok, d), F32)] + out_shapes
        args = (x1, y, pmod2, ln_g.reshape(1, d), ln_b.reshape(1, d), mod2, w, cos_t, sin_t)
    outs = pl.pallas_call(
        functools.partial(_proj_kernel, prev is not None),
        grid=(b, nt),
        in_specs=in_specs,
        out_specs=out_specs,
        out_shape=out_shapes,
        compiler_params=_params("arbitrary", "arbitrary"),
        name="proj",
    )(*args)
    return (xs, outs) if prev is None else (outs[0], outs[1:])


def _log_sigmoid(z):
    return -(jnp.maximum(-z, 0.0) + jnp.log(1.0 + jnp.exp(-jnp.abs(z))))


def _gla_kernel(qf, kf, vf, rf, qb, kb, vb, rb, wd_ref, bd_ref, of_ref, ob_ref, *st_refs):
    n = pl.program_id(1)

    @pl.when(n == 0)
    def _():
        for st_ref in st_refs:
            st_ref[...] = jnp.zeros_like(st_ref)

    c = GLA_CHUNK
    row = lax.broadcasted_iota(jnp.int32, (c, c), 0)
    col = lax.broadcasted_iota(jnp.int32, (c, c), 1)
    lane = lax.broadcasted_iota(jnp.int32, (1, LANES), 1)
    head_mask = (lane < GLA_DK, lane >= GLA_DK)

    dirs = ((qf, kf, vf, rf, of_ref), (qb, kb, vb, rb, ob_ref))
    chains = list(itertools.product(range(qf.shape[0]), range(2)))
    causal = [(row >= col), (row <= col)]
    tri = [jnp.where(cz, 1.0, 0.0).astype(F32) for cz in causal]
    pairs = range(GLA_HEADS // 2)
    psl = [slice(p * LANES, (p + 1) * LANES) for p in pairs]

    z = [_mm(dirs[d][3][bb].astype(BF16), wd_ref[0, d]) + bd_ref[0, d] for bb, d in chains]
    cum = [jnp.dot(tri[d], _log_sigmoid(zc) * (1.0 / GLA_TAU), precision=HIGHEST,
                   preferred_element_type=F32) for (bb, d), zc in zip(chains, z)]
    q_in, k_in, k_st, dec = [], [], [], []
    for (bb, d), cm in zip(chains, cum):
        last = cm[c - 1:c] if d == 0 else cm[0:1]
        k = dirs[d][1][bb]
        q_in.append(dirs[d][0][bb] * jnp.exp(cm))
        k_in.append((k * jnp.exp(-cm)).astype(BF16))
        k_st.append(k * jnp.exp(last - cm))
        dec.append(jnp.exp(last))
    att = []
    for ci in range(len(chains)):
        for p in pairs:
            q2 = q_in[ci][:, psl[p]]
            lhs = jnp.concatenate([jnp.where(head_mask[0], q2, 0.0),
                                   jnp.where(head_mask[1], q2, 0.0)], axis=0).astype(BF16)
            att.append(_nt(lhs, k_in[ci][:, psl[p]]))
    for ci, (bb, d) in enumerate(chains):
        v = dirs[d][2][bb]
        for p in pairs:
            q2b = q_in[ci][:, psl[p]].astype(BF16)
            for hh in range(2):
                h = 2 * p + hh
                a = jnp.where(causal[d], att[ci * len(pairs) + p][hh * c:(hh + 1) * c], 0.0).astype(BF16)
                st = st_refs[(bb * 2 + d) * GLA_HEADS + h][...]
                dirs[d][4][bb, :, h * GLA_DV:(h + 1) * GLA_DV] = (
                    _mm(a, v[:, h * GLA_DV:(h + 1) * GLA_DV]) + _nt(q2b, st.astype(BF16)))
    for ci, (bb, d) in enumerate(chains):
        v = dirs[d][2][bb]
        for p in pairs:
            for hh in range(2):
                h = 2 * p + hh
                st_ref = st_refs[(bb * 2 + d) * GLA_HEADS + h]
                kh = jnp.where(head_mask[hh], k_st[ci][:, psl[p]], 0.0).astype(BF16)
                st_ref[...] = st_ref[...] * dec[ci][:, psl[p]] + _tn(v[:, h * GLA_DV:(h + 1) * GLA_DV], kh)


def _gla(pf, pb, wd, bd, layer):
    b, ntok, _ = pf.shape
    nch = ntok // GLA_CHUNK
    nctx = CTX_LEN // GLA_CHUNK
    nb = GLA_BATCH if b % GLA_BATCH == 0 else 1

    def fwd(bi, n):
        return (bi, n, 0)

    def bwd(bi, n):
        return (bi, jnp.where(n < nctx, nctx - 1 - n, nch + nctx - 1 - n), 0)

    def col(im, block):
        return lambda bi, n: im(bi, n)[:2] + (block,)

    def specs(im):
        return [pl.BlockSpec((nb, GLA_CHUNK, 256), col(im, PF_GQ // 256)),
                pl.BlockSpec((nb, GLA_CHUNK, 256), col(im, PF_GK // 256)),
                pl.BlockSpec((nb, GLA_CHUNK, HALF), col(im, PB_GV // HALF)),
                pl.BlockSpec((nb, GLA_CHUNK, LANES), col(im, PF_GR // LANES))]

    return pl.pallas_call(
        _gla_kernel,
        grid=(b // nb, nch),
        in_specs=specs(fwd) + specs(bwd) + [
            pl.BlockSpec((1, 2, LANES, 256), lambda bi, n: (layer, 0, 0, 0)),
            pl.BlockSpec((1, 2, 1, 256), lambda bi, n: (layer, 0, 0, 0)),
        ],
        out_specs=[pl.BlockSpec((nb, GLA_CHUNK, 512), fwd), pl.BlockSpec((nb, GLA_CHUNK, 512), bwd)],
        out_shape=[jax.ShapeDtypeStruct((b, ntok, 512), F32)] * 2,
        scratch_shapes=[pltpu.VMEM((GLA_DV, LANES), F32)] * (nb * 2 * GLA_HEADS),
        compiler_params=_params("arbitrary", "arbitrary"),
        name="gla",
    )(pf, pf, pb, pf, pf, pf, pb, pf, wd, bd)


DIFF_KC = 256
DIFF_BATCH = 2


def _diff_kernel(lam_init, q_ref, k_ref, v_ref, lq_ref, lk_ref, g_ref, o_ref, s_even, s_odd):
    t = pl.program_id(1)
    tq = q_ref.shape[1]
    nkc = k_ref.shape[1] // DIFF_KC
    lane = lax.broadcasted_iota(jnp.int32, (1, LANES), 1)
    comp0 = (lane // (DIFF_DH // 2)) % 2 == 0
    lql = lq_ref[...] * lk_ref[...]
    lam = (jnp.exp(jnp.sum(lql[0:1], axis=1, keepdims=True))
           - jnp.exp(jnp.sum(lql[1:2], axis=1, keepdims=True)) + lam_init)
    s_bufs = (s_even, s_odd)

    units = list(itertools.product(range(q_ref.shape[0]), range(DIFF_HEADS)))

    def stacked_q(u):
        bb, h = units[u]
        q = q_ref[bb, :, h * LANES:(h + 1) * LANES]
        zero = jnp.zeros_like(q)
        return jnp.concatenate([jnp.where(comp0, q, zero), jnp.where(comp0, zero, q)], axis=0)

    def score_chunk(u, qq, ci, m8):
        bb, h = units[u]
        s = _nt(k_ref[bb, ci * DIFF_KC:(ci + 1) * DIFF_KC, h * LANES:(h + 1) * LANES], qq)
        s_bufs[u % 2][ci] = s
        mc = jnp.max(s.reshape(DIFF_KC // 8, 8, 2 * tq), axis=0)
        return mc if m8 is None else jnp.maximum(m8, mc)

    def value_chunk(u, mrow, ci, acc):
        bb, h = units[u]
        p = jnp.exp2(s_bufs[u % 2][ci] - mrow).astype(BF16)
        part = _mm(v_ref[bb, h * DIFF_VROWS:(h + 1) * DIFF_VROWS, ci * DIFF_KC:(ci + 1) * DIFF_KC], p)
        return part if acc is None else acc + part

    def finish(u, acc):
        bb, h = units[u]
        o = acc[:DIFF_DV] / acc[DIFF_DV:DIFF_DV + 1]
        o = o[:, :tq] - lam * o[:, tq:]
        o = o * lax.rsqrt(jnp.mean(o * o, axis=0, keepdims=True) + RMS_EPS) * g_ref[...] * (1.0 - lam_init)
        o_ref[bb, :, h * DIFF_DV:(h + 1) * DIFF_DV] = o.T.astype(BF16)

    def attend(nch):
        qq = stacked_q(0)
        m8 = None
        for ci in range(nch):
            m8 = score_chunk(0, qq, ci, m8)
        for u in range(1, len(units) + 1):
            mrow = jnp.max(m8, axis=0, keepdims=True)
            if u < len(units):
                qq = stacked_q(u)
            acc = m8 = None
            for ci in range(nch):
                if u < len(units):
                    m8 = score_chunk(u, qq, ci, m8)
                acc = value_chunk(u - 1, mrow, ci, acc)
            finish(u - 1, acc)

    @pl.when(t == 0)
    def _():
        attend(1)

    @pl.when(t > 0)
    def _():
        attend(nkc)


def _diff(pb, dv, lam_q, lam_k, norm_g, lam_init):
    b, ntok, _ = pb.shape
    w = HALF
    nt = ntok // TOK_TILE
    nb = DIFF_BATCH if b % DIFF_BATCH == 0 else 1
    scores = pltpu.VMEM((ntok // DIFF_KC, DIFF_KC, 2 * TOK_TILE), F32)
    return pl.pallas_call(
        functools.partial(_diff_kernel, lam_init),
        grid=(b // nb, nt),
        in_specs=[
            pl.BlockSpec((nb, TOK_TILE, w), lambda bi, t: (bi, t, PB_DQ // HALF)),
            pl.BlockSpec((nb, ntok, w), lambda bi, t: (bi, 0, PB_DK // HALF), pipeline_mode=pl.Buffered(1)),
            pl.BlockSpec((nb, DIFF_HEADS * DIFF_VROWS, ntok), lambda bi, t: (bi, 0, 0),
                         pipeline_mode=pl.Buffered(1)),
            pl.BlockSpec((2, DIFF_DH), lambda bi, t: (0, 0)),
            pl.BlockSpec((2, DIFF_DH), lambda bi, t: (0, 0)),
            pl.BlockSpec((DIFF_DV, 1), lambda bi, t: (0, 0)),
        ],
        out_specs=pl.BlockSpec((nb, TOK_TILE, w), lambda bi, t: (bi, t, 0)),
        out_shape=jax.ShapeDtypeStruct((b, ntok, w), BF16),
        scratch_shapes=[scores, scores],
        compiler_params=_params("arbitrary", "arbitrary"),
        name="diff",
    )(pb, pb, dv, lam_q, lam_k, norm_g.reshape(DIFF_DV, 1))


def _na_kernel(rows, q_ref, k_ref, v_ref, b_ref, o_ref):
    t = pl.program_id(1)
    nctx = CTX_LEN // NA_TILE
    r = t - nctx
    rs = jnp.clip(r - NA_WIN_H // 2, 0, rows - NA_WIN_H)
    start = pl.multiple_of(CTX_LEN + rs * GRID_W, GRID_W)
    nloc = NA_WIN_H * GRID_W
    lane = lax.broadcasted_iota(jnp.int32, (1, LANES), 1)
    first = lane < NA_DH
    chains = list(itertools.product(range(q_ref.shape[0]), range(NA_HEADS // 2)))
    psl = [slice(p * LANES, (p + 1) * LANES) for p in range(NA_HEADS // 2)]
    scores = []
    for bb, p in chains:
        q2 = q_ref[bb, :, psl[p]]
        zero = jnp.zeros_like(q2)
        lhs = jnp.concatenate([jnp.where(first, q2, zero), jnp.where(first, zero, q2)], axis=0)
        s_loc = _nt(lhs, k_ref[bb, pl.ds(start, nloc), psl[p]]) + b_ref[p]
        s_ctx = _nt(lhs, k_ref[bb, 0:CTX_LEN, psl[p]])
        scores.append((s_loc, s_ctx))
    probs = []
    for s_loc, s_ctx in scores:
        m = jnp.maximum(jnp.max(s_loc, axis=1, keepdims=True), jnp.max(s_ctx, axis=1, keepdims=True))
        p_loc = jnp.exp2(s_loc - m)
        p_ctx = jnp.exp2(s_ctx - m)
        l = jnp.sum(p_loc, axis=1, keepdims=True) + jnp.sum(p_ctx, axis=1, keepdims=True)
        probs.append((p_loc.astype(BF16), p_ctx.astype(BF16), l))
    for (bb, p), (p_loc, p_ctx, l) in zip(chains, probs):
        o = (_mm(p_loc, v_ref[bb, pl.ds(start, nloc), psl[p]]) + _mm(p_ctx, v_ref[bb, 0:CTX_LEN, psl[p]])) / l
        o_ref[bb, :, psl[p]] = jnp.where(first, o[:NA_TILE], o[NA_TILE:]).astype(BF16)


def _na(pb, bias):
    b, ntok, _ = pb.shape
    w = HALF
    nt = ntok // NA_TILE
    nctx = CTX_LEN // NA_TILE
    rows = (ntok - CTX_LEN) // GRID_W
    nb = NA_BATCH if b % NA_BATCH == 0 else 1

    def cfg(bi, t):
        r = t - nctx
        return (jnp.where(t < nctx, NA_WIN_H, r - jnp.clip(r - NA_WIN_H // 2, 0, rows - NA_WIN_H)), 0, 0)

    def kv_spec(block):
        return pl.BlockSpec((nb, ntok, w), lambda bi, t: (bi, 0, block), pipeline_mode=pl.Buffered(1))

    return pl.pallas_call(
        functools.partial(_na_kernel, rows),
        grid=(b // nb, nt),
        in_specs=[
            pl.BlockSpec((nb, NA_TILE, w), lambda bi, t: (bi, t, PB_NQ // HALF)),
            kv_spec(PB_NK // HALF), kv_spec(PB_NV // HALF),
            pl.BlockSpec((NA_HEADS // 2, 2 * NA_TILE, NA_WIN_H * GRID_W), cfg),
        ],
        out_specs=pl.BlockSpec((nb, NA_TILE, w), lambda bi, t: (bi, t, 0)),
        out_shape=jax.ShapeDtypeStruct((b, ntok, w), BF16),
        compiler_params=_params("arbitrary", "arbitrary"),
        name="na",
    )(pb, pb, pb, bias)


def _na_bias(rpb):
    qc = np.arange(GRID_W)[:, None]
    kc = np.arange(GRID_W)[None, :]
    col_idx = np.clip(kc - qc + NA_WIN_W - 1, 0, 2 * NA_WIN_W - 2)
    cstart = np.clip(qc - NA_WIN_W // 2, 0, GRID_W - NA_WIN_W)
    win = (kc >= cstart) & (kc < cstart + NA_WIN_W)
    toep = jnp.where(win[None, None], rpb.astype(F32)[..., col_idx] * LOG2E, NEG_BIG)
    toep = toep.transpose(0, 2, 1, 3)
    cfgs = [toep[:, :, NA_WIN_H - 1 - c:2 * NA_WIN_H - 1 - c] for c in range(NA_WIN_H)]
    cfgs.append(jnp.full(cfgs[0].shape, NEG_BIG, F32))
    return jnp.stack(cfgs, axis=0).reshape(-1, 2 * GRID_W, NA_WIN_H * GRID_W)


def _layernorm(x, g, b):
    mu = jnp.mean(x, axis=1, keepdims=True)
    xc = x - mu
    var = jnp.mean(xc * xc, axis=1, keepdims=True)
    return xc * lax.rsqrt(var + LN_EPS) * g + b


def _route(logits, bias):
    aff = _sigmoid(logits)
    sel = aff + bias
    srow = [sel[e:e + 1] for e in range(N_EXPERTS)]
    arow = [aff[e:e + 1] for e in range(N_EXPERTS)]
    gscore = []
    for g in range(N_GROUPS):
        a0, a1, a2, a3 = srow[4 * g:4 * g + 4]
        hi01, lo01 = jnp.maximum(a0, a1), jnp.minimum(a0, a1)
        hi23, lo23 = jnp.maximum(a2, a3), jnp.minimum(a2, a3)
        top1 = jnp.maximum(hi01, hi23)
        top2 = jnp.maximum(jnp.minimum(hi01, hi23), jnp.maximum(lo01, lo23))
        gscore.append(top1 + top2)
    best = jnp.zeros_like(gscore[0])
    bestv = gscore[0]
    for g in range(1, N_GROUPS):
        better = gscore[g] > bestv
        best = jnp.where(better, float(g), best)
        bestv = jnp.where(better, gscore[g], bestv)

    def pick(rows_, i):
        out = rows_[i]
        for g in range(1, N_GROUPS):
            out = jnp.where(best == float(g), rows_[4 * g + i], out)
        return out

    s4 = [pick(srow, i) for i in range(4)]
    f4 = [pick(arow, i) for i in range(4)]
    chosen = []
    for i in range(4):
        rank = jnp.zeros_like(best)
        for j in range(4):
            if j == i:
                continue
            ahead = (s4[j] > s4[i]) | ((s4[j] == s4[i]) & (j < i))
            rank = rank + jnp.where(ahead, 1.0, 0.0)
        chosen.append(rank < 2.0)
    c0, c1, c2, c3 = chosen
    pidx = jnp.where(c0, jnp.where(c1, 0.0, jnp.where(c2, 1.0, 2.0)),
                     jnp.where(c1, jnp.where(c2, 3.0, 4.0), 5.0))
    a_lo = jnp.where(c0, f4[0], jnp.where(c1, f4[1], f4[2]))
    a_hi = jnp.where(c3, f4[3], jnp.where(c2, f4[2], f4[1]))
    den = a_lo + a_hi
    return best * float(N_PAIRS) + pidx, a_lo / den, a_hi / den


def _merge_kernel(x_ref, of_ref, ob_ref, gsg_ref, do_ref, no_ref, sg_ref, mod_ref, gg_ref,
                  wb_ref, wo_ref, lng_ref, lnb_ref, wr_ref, br_ref,
                  x1_ref, h2_ref, route_ref, cnt_ref, run_ref):
    first_step = jnp.logical_and(pl.program_id(0) == 0, pl.program_id(1) == 0)

    @pl.when(first_step)
    def _():
        run_ref[...] = jnp.zeros_like(run_ref)

    nb, tm = x_ref.shape[0], x_ref.shape[1]
    tiles = range(nb)
    mods = [mod_ref[bb, 0] for bb in tiles]
    gg = gg_ref[...]

    branches = []
    for bb in tiles:
        o = of_ref[bb] + ob_ref[bb]
        parts = []
        for h in range(GLA_HEADS):
            oh = o[:, h * GLA_DV:(h + 1) * GLA_DV]
            parts.append(oh * lax.rsqrt(jnp.mean(oh * oh, axis=1, keepdims=True) + RMS_EPS) * gg)
        gla = (jnp.concatenate(parts, axis=1) * gsg_ref[bb].astype(F32)).astype(BF16)
        branches.append((gla, do_ref[bb], no_ref[bb]))
    ys = [None] * nb
    for i in range(3):
        for bb in tiles:
            term = sg_ref[bb, :, i * D_MODEL:(i + 1) * D_MODEL].astype(F32) * _mm(branches[bb][i], wb_ref[0, i])
            ys[bb] = term if i == 0 else ys[bb] + term
    ys = [_mm(y.astype(BF16), wo_ref[0]) for y in ys]
    h2s = []
    for bb in tiles:
        x1 = _layernorm(DEEPNORM_ALPHA * x_ref[bb] + mods[bb][2:3] * ys[bb], lng_ref[...], lnb_ref[...])
        x1_ref[bb] = x1
        h2 = x1 * (1.0 + mods[bb][4:5]) + mods[bb][3:4]
        h2_ref[bb, :, :D_MODEL] = h2
        h2s.append(h2)
    logits = [lax.dot_general(wr_ref[...], h2, (((1,), (1,)), ((), ())),
                              precision=HIGHEST, preferred_element_type=F32) for h2 in h2s]
    routes = [_route(lg, br_ref[...]) for lg in logits]

    srows = lax.broadcasted_iota(jnp.int32, (SEG_ROWS, tm), 0).astype(F32)
    ii = lax.broadcasted_iota(jnp.int32, (tm, tm), 0)
    jj = lax.broadcasted_iota(jnp.int32, (tm, tm), 1)
    before = jnp.where(ii < jj, 1.0, 0.0).astype(BF16)
    onehots = [jnp.where(srows == seg, 1.0, 0.0) for seg, _, _ in routes]
    prefixes = [_mm(oh.astype(BF16), before) for oh in onehots]
    run = run_ref[...]
    for bb in tiles:
        seg, w_lo, w_hi = routes[bb]
        rank = jnp.sum(onehots[bb] * (prefixes[bb] + run[:, 0:1]), axis=0, keepdims=True)
        run = run + jnp.sum(onehots[bb], axis=1, keepdims=True)
        zrow = jnp.zeros_like(seg)
        route = jnp.concatenate([seg, rank, w_lo, w_hi, zrow, zrow, zrow, zrow], axis=0)
        route_ref[bb, 0] = route
        wide = jnp.concatenate([route, jnp.zeros((LANES - 8, tm), F32)], axis=0)
        h2_ref[bb, :, D_MODEL:] = wide.T
    run_ref[...] = run
    cnt_ref[...] = run


def _merge(xs, o_f, o_b, pb, d_o, n_o, mod2, gla_g, wb, wo, ln_g, ln_b, wr_t, br, layer):
    b, ntok, d = xs.shape
    nt = ntok // TOK_TILE
    nb = MERGE_BATCH if b % MERGE_BATCH == 0 else 1
    tok = lambda bi, ti: (bi, ti, 0)
    const2 = lambda bi, ti: (0, 0)

    def tokspec(wd):
        return pl.BlockSpec((nb, TOK_TILE, wd), tok)

    return pl.pallas_call(
        _merge_kernel,
        grid=(b // nb, nt),
        in_specs=[
            tokspec(d), tokspec(HALF), tokspec(HALF),
            pl.BlockSpec((nb, TOK_TILE, HALF), lambda bi, ti: (bi, ti, PB_GSG // HALF)),
            tokspec(HALF), tokspec(HALF),
            pl.BlockSpec((nb, TOK_TILE, 3 * d), lambda bi, ti: (bi, ti, PB_SG // (3 * d))),
            pl.BlockSpec((nb, 1, 8, d), lambda bi, ti: (bi, jnp.minimum(ti, 1), 0, 0)),
            pl.BlockSpec((1, GLA_DV), const2),
            pl.BlockSpec((1, 3, BRANCH_W, d), lambda bi, ti: (layer, 0, 0, 0)),
            pl.BlockSpec((1, d, d), lambda bi, ti: (layer, 0, 0)),
            pl.BlockSpec((1, d), const2),
            pl.BlockSpec((1, d), const2),
            pl.BlockSpec((N_EXPERTS, d), const2),
            pl.BlockSpec((N_EXPERTS, 1), const2),
        ],
        out_specs=[
            tokspec(d), tokspec(d + LANES),
            pl.BlockSpec((nb, 1, 8, TOK_TILE), lambda bi, ti: (bi, ti, 0, 0)),
            pl.BlockSpec((SEG_ROWS, LANES), const2),
        ],
        out_shape=[
            jax.ShapeDtypeStruct((b, ntok, d), F32),
            jax.ShapeDtypeStruct((b, ntok, d + LANES), F32),
            jax.ShapeDtypeStruct((b, nt, 8, TOK_TILE), F32),
            jax.ShapeDtypeStruct((SEG_ROWS, LANES), F32),
        ],
        scratch_shapes=[pltpu.VMEM((SEG_ROWS, LANES), F32)],
        compiler_params=_params("arbitrary", "arbitrary"),
        name="merge",
    )(xs, o_f, o_b, pb, d_o, n_o, pb, mod2.reshape(b, 2, 8, d), gla_g.reshape(1, GLA_DV), wb, wo,
      ln_g.reshape(1, d), ln_b.reshape(1, d), wr_t, br.reshape(N_EXPERTS, 1))


def _moe_kernel(dest, meta, h_hbm, wg1, wu1, wd1, wg2, wu2, wd2, y_hbm,
                src, xbuf, ybuf, sem_in, sem_out):
    i = pl.program_id(0)
    ntile = pl.num_programs(0)
    n_used = meta[2 * ntile]
    slot = i % 2
    tm = MOE_TILE

    def n_real(tile):
        return meta[2 * ntile + 1 + tile]

    def gather(tile, sl):
        def body(j, carry):
            tok = src[tile * tm + j]
            pltpu.make_async_copy(h_hbm.at[pl.ds(tok, 1)], xbuf.at[sl, pl.ds(j, 1)], sem_in.at[sl]).start()
            return carry
        lax.fori_loop(0, tm, body, 0, unroll=8)

    def wait_in(sl):
        pltpu.make_async_copy(h_hbm.at[pl.ds(0, tm)], xbuf.at[sl], sem_in.at[sl]).wait()

    def row_out(tile, sl, j):
        tok = src[tile * tm + j]
        return pltpu.make_async_copy(ybuf.at[sl, pl.ds(j, 1)], y_hbm.at[pl.ds(tok, 1)], sem_out.at[sl])

    def scatter(tile, sl):
        nr = n_real(tile)

        def body(j, carry):
            row_out(tile, sl, j).start()
            return carry

        @pl.when(nr == tm)
        def _():
            for j in range(tm):
                row_out(tile, sl, j).start()

        @pl.when(nr < tm)
        def _():
            lax.fori_loop(0, nr, body, 0)

    def wait_out(tile, sl):
        nr = n_real(tile)

        @pl.when(nr == tm)
        def _():
            pltpu.make_async_copy(ybuf.at[sl], y_hbm.at[pl.ds(0, tm)], sem_out.at[sl]).wait()

        @pl.when(nr < tm)
        def _():
            def body(j, carry):
                row_out(tile, sl, j).wait()
                return carry
            lax.fori_loop(0, nr, body, 0)

    @pl.when(i == 0)
    def _():
        def clear(j, carry):
            src[j] = 0
            return carry
        for s in range(N_SEG):
            lax.fori_loop(meta[3 * ntile + 1 + 2 * s], meta[3 * ntile + 2 + 2 * s], clear, 0)
        lax.fori_loop(0, tm, lambda j, carry: clear(n_used * tm + j, carry), 0, unroll=8)

        def place(j, carry):
            src[dest[j]] = j
            return carry
        lax.fori_loop(0, dest.shape[0], place, 0, unroll=8)
        gather(0, 0)

    @pl.when(i < n_used)
    def _():
        wait_in(slot)

        @pl.when(i >= 2)
        def _():
            wait_out(i - 2, slot)

        for j in range(tm):
            tok = src[(i + 1) * tm + j]
            pltpu.make_async_copy(h_hbm.at[pl.ds(tok, 1)], xbuf.at[1 - slot, pl.ds(j, 1)],
                                  sem_in.at[1 - slot]).start()

        x = xbuf[slot, :, :D_MODEL].astype(BF16)
        w = xbuf[slot, :, D_MODEL:]

        a1, u1 = _mm(x, wg1[0]), _mm(x, wu1[0])
        a2, u2 = _mm(x, wg2[0]), _mm(x, wu2[0])
        y1 = _mm((a1 * _sigmoid(a1) * u1).astype(BF16), wd1[0])
        y2 = _mm((a2 * _sigmoid(a2) * u2).astype(BF16), wd2[0])
        ybuf[slot] = w[:, 2:3] * y1 + w[:, 3:4] * y2
        scatter(i, slot)

    @pl.when(i == ntile - 1)
    def _():
        wait_in(n_used % 2)
        wait_out(n_used - 1, (n_used - 1) % 2)

        @pl.when(n_used >= 2)
        def _():
            wait_out(n_used - 2, n_used % 2)


def _moe(h2, dest, meta, wg, wu, wd, layer):
    t, dw = h2.shape
    d = dw - LANES
    ntile = (meta.shape[0] - 1 - 2 * N_SEG) // 3
    lo = lambda i, s, m: (layer * N_EXPERTS + m[2 * i], 0, 0)
    hi = lambda i, s, m: (layer * N_EXPERTS + m[2 * i + 1], 0, 0)
    grid_spec = pltpu.PrefetchScalarGridSpec(
        num_scalar_prefetch=2,
        grid=(ntile,),
        in_specs=[
            pl.BlockSpec(memory_space=pl.ANY),
            pl.BlockSpec((1, d, D_EXPERT), lo), pl.BlockSpec((1, d, D_EXPERT), lo),
            pl.BlockSpec((1, D_EXPERT, d), lo),
            pl.BlockSpec((1, d, D_EXPERT), hi), pl.BlockSpec((1, d, D_EXPERT), hi),
            pl.BlockSpec((1, D_EXPERT, d), hi),
        ],
        out_specs=pl.BlockSpec(memory_space=pl.ANY),
        scratch_shapes=[pltpu.SMEM(((ntile + 1) * MOE_TILE,), jnp.int32),
                        pltpu.VMEM((2, MOE_TILE, dw), F32), pltpu.VMEM((2, MOE_TILE, d), F32),
                        pltpu.SemaphoreType.DMA((2,)), pltpu.SemaphoreType.DMA((2,))],
    )
    return pl.pallas_call(
        _moe_kernel,
        grid_spec=grid_spec,
        out_shape=jax.ShapeDtypeStruct((t, d), F32),
        compiler_params=_params("arbitrary"),
        name="moe",
    )(dest, meta, h2, wg, wu, wd, wg, wu, wd)


def _lookup(idx, table):
    n = table.shape[0]
    hit = idx[:, None] == jnp.arange(n, dtype=jnp.int32)[None, :]
    return jnp.sum(jnp.where(hit, table[None, :], 0), axis=1)


def _dispatch_plan(route, counts, t):
    ntile = (t + N_SEG * (MOE_TILE - 1)) // MOE_TILE
    seg = route[:, :, 0, :].reshape(t).astype(jnp.int32)
    rank = route[:, :, 1, :].reshape(t).astype(jnp.int32)
    cnt = counts[:N_SEG, 0].astype(jnp.int32)
    seg_tiles = (cnt + MOE_TILE - 1) // MOE_TILE
    upto = jnp.arange(N_SEG)[None, :] <= jnp.arange(N_SEG)[:, None]
    tile_end = jnp.sum(jnp.where(upto, seg_tiles[None, :], 0), axis=1)
    first_tile = tile_end - seg_tiles
    dest = _lookup(seg, first_tile * MOE_TILE) + rank
    n_used = tile_end[-1]
    tiles = jnp.arange(ntile, dtype=jnp.int32)
    tile_seg = jnp.sum((tile_end[None, :] <= jnp.minimum(tiles, n_used - 1)[:, None]).astype(jnp.int32), axis=1)
    tile_seg = jnp.minimum(tile_seg, N_SEG - 1)
    base = (tile_seg // N_PAIRS) * EXPERTS_PER_GROUP
    pair = tile_seg % N_PAIRS
    e_lo = base + _lookup(pair, jnp.asarray(PAIR_LO, jnp.int32))
    e_hi = base + _lookup(pair, jnp.asarray(PAIR_HI, jnp.int32))
    left = _lookup(tile_seg, cnt) - (tiles - _lookup(tile_seg, first_tile)) * MOE_TILE
    n_real = jnp.where(tiles < n_used, jnp.clip(left, 0, MOE_TILE), 0)
    pad_lo = first_tile * MOE_TILE + cnt
    pad_hi = tile_end * MOE_TILE
    meta = jnp.concatenate([jnp.stack([e_lo, e_hi], axis=1).reshape(-1), n_used[None], n_real,
                            jnp.stack([pad_lo, pad_hi], axis=1).reshape(-1)]).astype(jnp.int32)
    return dest, meta


def _final_kernel(x_ref, y_ref, mod_ref, g_ref, b_ref, o_ref):
    mod = mod_ref[0]
    o_ref[0] = _layernorm(DEEPNORM_ALPHA * x_ref[0] + mod[5:6] * y_ref[...], g_ref[...], b_ref[...])


def _final(x1, y, mod2, ln_g, ln_b, latents_only):
    b, ntok, d = x1.shape
    nt = ntok // TOK_TILE
    skip = CTX_LEN // TOK_TILE if latents_only else 0
    return pl.pallas_call(
        _final_kernel,
        grid=(b, nt - skip),
        in_specs=[
            pl.BlockSpec((1, TOK_TILE, d), lambda bi, ti: (bi, ti + skip, 0)),
            pl.BlockSpec((TOK_TILE, d), lambda bi, ti: (bi * nt + ti + skip, 0)),
            pl.BlockSpec((1, 8, d), lambda bi, ti: (2 * bi + jnp.minimum(ti + skip, 1), 0, 0)),
            pl.BlockSpec((1, d), lambda bi, ti: (0, 0)),
            pl.BlockSpec((1, d), lambda bi, ti: (0, 0)),
        ],
        out_specs=pl.BlockSpec((1, TOK_TILE, d), lambda bi, ti: (bi, ti, 0)),
        out_shape=jax.ShapeDtypeStruct((b, ntok - skip * TOK_TILE, d), F32),
        compiler_params=_params("arbitrary", "arbitrary"),
        name="final_ln",
    )(x1, y, mod2, ln_g.reshape(1, d), ln_b.reshape(1, d))


def _rotary_order(w):
    lead = w.shape[:-1]
    half = DIFF_DH // 2
    return w.reshape(lead + (DIFF_HEADS, 2, 2, half)).swapaxes(-2, -3).reshape(lead + (DIFF_HEADS * 2 * DIFF_DH,))


def _pack_w_in(w_in):
    splits = np.cumsum([256, 256, 512, 512, 32, 512, 512, 512, 512, 512, 512])
    gq, gk, gv, gg, gr, dq, dk, dv, nq, nk, nv, sg = jnp.split(w_in.astype(BF16), splits, axis=-1)
    gr = jnp.pad(gr, ((0, 0), (0, 0), (0, LANES - 2 * GLA_RANK)))
    return jnp.concatenate([gq, gk, gv, gg, gr, _rotary_order(dq), _rotary_order(dk), dv, nq, nk, nv, sg],
                           axis=-1)


def _rope_tables(n_lat):
    t = jnp.arange(n_lat)
    row = (t // GRID_W).astype(F32)
    col = (t % GRID_W).astype(F32)
    n_freq = DIFF_DH // 4
    inv = ROPE_BASE ** (-jnp.arange(n_freq, dtype=F32) / n_freq)
    ang = jnp.concatenate([row[:, None] * inv, col[:, None] * inv], -1)
    cos, sin = jnp.cos(ang), jnp.sin(ang)
    cos_t = jnp.concatenate([cos] * 4, axis=1)
    sin_t = jnp.concatenate([-sin, -sin, sin, sin], axis=1)
    cos_t = jnp.concatenate([jnp.ones((CTX_LEN, LANES), F32), cos_t], axis=0)
    sin_t = jnp.concatenate([jnp.zeros((CTX_LEN, LANES), F32), sin_t], axis=0)
    return cos_t, sin_t


def _pack_decay(w_decay, b_decay):
    depth = w_decay.shape[0]
    wd = jnp.zeros((depth, 2, LANES, GLA_HEADS * GLA_DK), F32)
    for d in range(2):
        wd = wd.at[:, d, d * GLA_RANK:(d + 1) * GLA_RANK].set(w_decay[:, d])
    return wd.astype(BF16), b_decay.reshape(depth, 2, 1, GLA_HEADS * GLA_DK)


def kernel(x, c, ctx, c_ctx, w_ada, b_ada, w_in, gla_w_decay, gla_b_decay, gla_norm_g, diff_lam_q,
           diff_lam_k, diff_norm_g, na_rpb, w_branch, w_o, ln_g, ln_b, w_router, b_router,
           w_exp_gate, w_exp_up, w_exp_down):
    b, l, d = x.shape
    lc = ctx.shape[1]
    ntok = lc + l
    t = b * ntok

    w_in_p = _pack_w_in(w_in)
    wd_p, bd_p = _pack_decay(gla_w_decay, gla_b_decay)
    cos_t, sin_t = _rope_tables(l)
    wb = w_branch.astype(BF16)
    wo = w_o.astype(BF16)
    wg = w_exp_gate.astype(BF16).reshape(DEPTH * N_EXPERTS, d, D_EXPERT)
    wu = w_exp_up.astype(BF16).reshape(DEPTH * N_EXPERTS, d, D_EXPERT)
    wdn = w_exp_down.astype(BF16).reshape(DEPTH * N_EXPERTS, D_EXPERT, d)
    wr_t = w_router.T

    cs = jnp.concatenate([c, c_ctx[None], jnp.zeros((16 - b - 1, d), F32)], axis=0)
    mods = _ada(cs, w_ada, b_ada).reshape(DEPTH, 16, 6, d)

    xs = jnp.concatenate([ctx, x], axis=1)
    prev = None
    for layer in range(DEPTH):
        lam_init = 0.8 - 0.6 * math.exp(-0.3 * layer)
        m_lat = mods[layer, :b]
        m_ctx = jnp.broadcast_to(mods[layer, b][None], (b, 6, d))
        mod2 = jnp.stack([m_ctx, m_lat], axis=1).reshape(2 * b, 6, d)
        mod2 = jnp.pad(mod2, ((0, 0), (0, 2), (0, 0)))

        xs, (pf, pb, dv) = _proj(xs, mod2, w_in_p, cos_t, sin_t, layer, prev)
        o_f, o_b = _gla(pf, pb, wd_p, bd_p, layer)
        d_o = _diff(pb, dv, diff_lam_q[layer], diff_lam_k[layer], diff_norm_g[layer], lam_init)
        n_o = _na(pb, _na_bias(na_rpb[layer]))
        x1, h2, route, counts = _merge(xs, o_f, o_b, pb, d_o, n_o, mod2, gla_norm_g[layer],
                                       wb, wo, ln_g[layer, 0], ln_b[layer, 0], wr_t, b_router, layer)
        dest, meta = _dispatch_plan(route, counts, t)
        y = _moe(h2.reshape(t, d + LANES), dest, meta, wg, wu, wdn, layer)
        prev = (x1, y, mod2, ln_g[layer, 1], ln_b[layer, 1])
    return _final(*prev, latents_only=True)
```

```python
import functools
import itertools
import math

import jax
import jax.numpy as jnp
import numpy as np
from jax import lax
from jax.experimental import pallas as pl
from jax.experimental.pallas import tpu as pltpu

F32 = jnp.float32
BF16 = jnp.bfloat16
HIGHEST = lax.Precision.HIGHEST

D_MODEL = 1024
DEPTH = 4
GRID_W = 64
CTX_LEN = 256
BRANCH_W = D_MODEL // 2
GLA_HEADS = 4
GLA_DV = 128
GLA_DK = 64
GLA_RANK = 16
GLA_TAU = 16.0
GLA_CHUNK = 64
GLA_BATCH = 8
DIFF_HEADS = 4
DIFF_DV = 128
DIFF_DH = 64
DIFF_VROWS = DIFF_DV + 16
NA_HEADS = 8
NA_DH = 64
NA_WIN_H = 8
NA_WIN_W = 16
N_EXPERTS = 16
N_GROUPS = 4
EXPERTS_PER_GROUP = 4
D_EXPERT = D_MODEL // 2
ROPE_BASE = 10000.0
LN_EPS = 1e-5
RMS_EPS = 1e-6
NEG_BIG = -1e30
DEEPNORM_ALPHA = (2 * DEPTH) ** 0.25
LOG2E = 1.4426950408889634

LANES = 128
TOK_TILE = 256
NA_TILE = GRID_W
NA_BATCH = 4
MERGE_BATCH = 2
MOE_TILE = 256
N_PAIRS = 6
N_SEG = N_GROUPS * N_PAIRS
SEG_ROWS = 32
PAIR_LO = (0, 0, 0, 1, 1, 2)
PAIR_HI = (1, 2, 3, 2, 3, 3)
VMEM_LIMIT = 56 * 1024 * 1024

W_GQ, W_GK, W_GV, W_GG, W_GR = 0, 256, 512, 1024, 1536
W_DQ, W_DK, W_DV = 1664, 2176, 2688
W_NQ, W_NK, W_NV = 3200, 3712, 4224
W_SG, W_END = 4736, 7808

PF_GQ, PF_GK, PF_GR, PF_W = 0, 256, 512, 640
PB_SG, PB_GV, PB_GSG, PB_DQ, PB_DK, PB_NQ, PB_NK, PB_NV, PB_W = 0, 3072, 3584, 4096, 4608, 5120, 5632, 6144, 6656
HALF = BRANCH_W


def _nt(a, b):
    return lax.dot_general(a, b, (((1,), (1,)), ((), ())), preferred_element_type=F32)


def _tn(a, b):
    return lax.dot_general(a, b, (((0,), (0,)), ((), ())), preferred_element_type=F32)


def _mm(a, b):
    return jnp.dot(a, b, preferred_element_type=F32)


def _sigmoid(x):
    return 0.5 * jnp.tanh(0.5 * x) + 0.5


def _params(*sem):
    return pltpu.CompilerParams(dimension_semantics=sem, vmem_limit_bytes=VMEM_LIMIT)


def _ada_kernel(c_ref, w_ref, b_ref, o_ref):
    cs = c_ref[...]
    s = cs * _sigmoid(cs)
    o_ref[0] = jnp.dot(s, w_ref[0], precision=HIGHEST, preferred_element_type=F32) + b_ref[0]


def _ada(cs, w_ada, b_ada):
    depth, d, n = w_ada.shape
    bn = 1536
    return pl.pallas_call(
        _ada_kernel,
        grid=(depth, n // bn),
        in_specs=[
            pl.BlockSpec((cs.shape[0], d), lambda l, j: (0, 0)),
            pl.BlockSpec((1, d, bn), lambda l, j: (l, 0, j)),
            pl.BlockSpec((1, 1, bn), lambda l, j: (l, 0, j)),
        ],
        out_specs=pl.BlockSpec((1, cs.shape[0], bn), lambda l, j: (l, 0, j)),
        out_shape=jax.ShapeDtypeStruct((depth, cs.shape[0], n), F32),
        compiler_params=_params("arbitrary", "arbitrary"),
        name="ada",
    )(cs, w_ada, b_ada.reshape(depth, 1, n))


def _proj_kernel(after_moe, *refs):
    if after_moe:
        (x1_ref, y_ref, pmod_ref, lng_ref, lnb_ref, mod_ref, w_ref, cos_ref, sin_ref, xs_ref,
         pf, pb, dv) = refs
        x = _layernorm(DEEPNORM_ALPHA * x1_ref[0] + pmod_ref[0][5:6] * y_ref[...], lng_ref[...], lnb_ref[...])
        xs_ref[0] = x
    else:
        x_ref, mod_ref, w_ref, cos_ref, sin_ref, pf, pb, dv = refs
        x = x_ref[0]
    mod = mod_ref[0]
    h = (x * (1.0 + mod[1:2]) + mod[0:1]).astype(BF16)

    def mm(lo, hi):
        return _mm(h, w_ref[0, :, lo:hi])

    def put(ref, col, val):
        ref[0, :, col:col + val.shape[1]] = val

    put(pf, PF_GQ, mm(W_GQ, W_GK) * (GLA_DK ** -0.5))
    put(pf, PF_GK, mm(W_GK, W_GV))
    put(pb, PB_GV, mm(W_GV, W_GG).astype(BF16))
    g = mm(W_GG, W_GR)
    put(pb, PB_GSG, (g * _sigmoid(g)).astype(BF16))
    put(pf, PF_GR, mm(W_GR, W_DQ))

    cos = cos_ref[...]
    sin = sin_ref[...]

    def rope(y, scale):
        parts = []
        for i in range(DIFF_HEADS):
            p = y[:, i * LANES:(i + 1) * LANES]
            parts.append(((p * cos + pltpu.roll(p, LANES // 2, 1) * sin) * scale).astype(BF16))
        return jnp.concatenate(parts, axis=1)

    put(pb, PB_DQ, rope(mm(W_DQ, W_DK), (DIFF_DH ** -0.5) * LOG2E))
    put(pb, PB_DK, rope(mm(W_DK, W_DV), 1.0))
    v = mm(W_DV, W_NQ)
    sub = lax.broadcasted_iota(jnp.int32, (DIFF_VROWS - DIFF_DV, v.shape[0]), 0)
    ones_rows = jnp.where(sub == 0, 1.0, 0.0).astype(BF16)
    for i in range(DIFF_HEADS):
        dv[0, i * DIFF_VROWS:i * DIFF_VROWS + DIFF_DV, :] = v[:, i * LANES:(i + 1) * LANES].T.astype(BF16)
        dv[0, i * DIFF_VROWS + DIFF_DV:(i + 1) * DIFF_VROWS, :] = ones_rows
    put(pb, PB_NQ, (mm(W_NQ, W_NK) * ((NA_DH ** -0.5) * LOG2E)).astype(BF16))
    put(pb, PB_NK, mm(W_NK, W_NV).astype(BF16))
    put(pb, PB_NV, mm(W_NV, W_SG).astype(BF16))
    for i in range(3):
        lo = W_SG + i * D_MODEL
        put(pb, PB_SG + i * D_MODEL, _sigmoid(mm(lo, lo + D_MODEL)).astype(BF16))


def _proj(xs, mod2, w, cos_t, sin_t, layer, prev=None):
    b, ntok, d = (xs if prev is None else prev[0]).shape
    nt = ntok // TOK_TILE
    tok = lambda bi, ti: (bi, ti, 0)
    vrows = DIFF_HEADS * DIFF_VROWS

    mod_spec = pl.BlockSpec((1, 8, d), lambda bi, ti: (2 * bi + jnp.minimum(ti, 1), 0, 0))
    vec_spec = pl.BlockSpec((1, d), lambda bi, ti: (0, 0))
    common_specs = [
        mod_spec,
        pl.BlockSpec((1, d, W_END), lambda bi, ti: (layer, 0, 0), pipeline_mode=pl.Buffered(1)),
        pl.BlockSpec((TOK_TILE, LANES), lambda bi, ti: (ti, 0)),
        pl.BlockSpec((TOK_TILE, LANES), lambda bi, ti: (ti, 0)),
    ]
    out_specs = [pl.BlockSpec((1, TOK_TILE, PF_W), tok), pl.BlockSpec((1, TOK_TILE, PB_W), tok),
                 pl.BlockSpec((1, vrows, TOK_TILE), lambda bi, ti: (bi, 0, ti))]
    out_shapes = [jax.ShapeDtypeStruct((b, ntok, PF_W), F32), jax.ShapeDtypeStruct((b, ntok, PB_W), BF16),
                  jax.ShapeDtypeStruct((b, vrows, ntok), BF16)]
    if prev is None:
        in_specs = [pl.BlockSpec((1, TOK_TILE, d), tok)] + common_specs
        args = (xs, mod2, w, cos_t, sin_t)
    else:
        x1, y, pmod2, ln_g, ln_b = prev
        in_specs = [pl.BlockSpec((1, TOK_TILE, d), tok),
                    pl.BlockSpec((TOK_TILE, d), lambda bi, ti: (bi * nt + ti, 0)),
                    mod_spec, vec_spec, vec_spec] + common_specs
        out_specs = [pl.BlockSpec((1, TOK_TILE, d), tok)] + out_specs
        out_shapes = [jax.ShapeDtypeStruct((b, ntok, d), F32)] + out_shapes
        args = (x1, y, pmod2, ln_g.reshape(1, d), ln_b.reshape(1, d), mod2, w, cos_t, sin_t)
    outs = pl.pallas_call(
        functools.partial(_proj_kernel, prev is not None),
        grid=(b, nt),
        in_specs=in_specs,
        out_specs=out_specs,
        out_shape=out_shapes,
        compiler_params=_params("arbitrary", "arbitrary"),
        name="proj",
    )(*args)
    return (xs, outs) if prev is None else (outs[0], outs[1:])


def _log_sigmoid(z):
    return -(jnp.maximum(-z, 0.0) + jnp.log(1.0 + jnp.exp(-jnp.abs(z))))


def _gla_kernel(qf, kf, vf, rf, qb, kb, vb, rb, wd_ref, bd_ref, of_ref, ob_ref, *st_refs):
    n = pl.program_id(1)

    @pl.when(n == 0)
    def _():
        for st_ref in st_refs:
            st_ref[...] = jnp.zeros_like(st_ref)

    c = GLA_CHUNK
    row = lax.broadcasted_iota(jnp.int32, (c, c), 0)
    col = lax.broadcasted_iota(jnp.int32, (c, c), 1)
    lane = lax.broadcasted_iota(jnp.int32, (1, LANES), 1)
    head_mask = (lane < GLA_DK, lane >= GLA_DK)

    dirs = ((qf, kf, vf, rf, of_ref), (qb, kb, vb, rb, ob_ref))
    chains = list(itertools.product(range(qf.shape[0]), range(2)))
    causal = [(row >= col), (row <= col)]
    tri = [jnp.where(cz, 1.0, 0.0).astype(F32) for cz in causal]
    pairs = range(GLA_HEADS // 2)
    psl = [slice(p * LANES, (p + 1) * LANES) for p in pairs]

    z = [_mm(dirs[d][3][bb].astype(BF16), wd_ref[0, d]) + bd_ref[0, d] for bb, d in chains]
    cum = [jnp.dot(tri[d], _log_sigmoid(zc) * (1.0 / GLA_TAU), precision=HIGHEST,
                   preferred_element_type=F32) for (bb, d), zc in zip(chains, z)]
    q_in, k_in, k_st, dec = [], [], [], []
    for (bb, d), cm in zip(chains, cum):
        last = cm[c - 1:c] if d == 0 else cm[0:1]
        k = dirs[d][1][bb]
        q_in.append(dirs[d][0][bb] * jnp.exp(cm))
        k_in.append((k * jnp.exp(-cm)).astype(BF16))
        k_st.append(k * jnp.exp(last - cm))
        dec.append(jnp.exp(last))
    att = []
    for ci in range(len(chains)):
        for p in pairs:
            q2 = q_in[ci][:, psl[p]]
            lhs = jnp.concatenate([jnp.where(head_mask[0], q2, 0.0),
                                   jnp.where(head_mask[1], q2, 0.0)], axis=0).astype(BF16)
            att.append(_nt(lhs, k_in[ci][:, psl[p]]))
    for ci, (bb, d) in enumerate(chains):
        v = dirs[d][2][bb]
        for p in pairs:
            q2b = q_in[ci][:, psl[p]].astype(BF16)
            for hh in range(2):
                h = 2 * p + hh
                a = jnp.where(causal[d], att[ci * len(pairs) + p][hh * c:(hh + 1) * c], 0.0).astype(BF16)
                st = st_refs[(bb * 2 + d) * GLA_HEADS + h][...]
                dirs[d][4][bb, :, h * GLA_DV:(h + 1) * GLA_DV] = (
                    _mm(a, v[:, h * GLA_DV:(h + 1) * GLA_DV]) + _nt(q2b, st.astype(BF16)))
    for ci, (bb, d) in enumerate(chains):
        v = dirs[d][2][bb]
        for p in pairs:
            for hh in range(2):
                h = 2 * p + hh
                st_ref = st_refs[(bb * 2 + d) * GLA_HEADS + h]
                kh = jnp.where(head_mask[hh], k_st[ci][:, psl[p]], 0.0).astype(BF16)
                st_ref[...] = st_ref[...] * dec[ci][:, psl[p]] + _tn(v[:, h * GLA_DV:(h + 1) * GLA_DV], kh)


def _gla(pf, pb, wd, bd, layer):
    b, ntok, _ = pf.shape
    nch = ntok // GLA_CHUNK
    nctx = CTX_LEN // GLA_CHUNK
    nb = GLA_BATCH if b % GLA_BATCH == 0 else 1

    def fwd(bi, n):
        return (bi, n, 0)

    def bwd(bi, n):
        return (bi, jnp.where(n < nctx, nctx - 1 - n, nch + nctx - 1 - n), 0)

    def col(im, block):
        return lambda bi, n: im(bi, n)[:2] + (block,)

    def specs(im):
        return [pl.BlockSpec((nb, GLA_CHUNK, 256), col(im, PF_GQ // 256)),
                pl.BlockSpec((nb, GLA_CHUNK, 256), col(im, PF_GK // 256)),
                pl.BlockSpec((nb, GLA_CHUNK, HALF), col(im, PB_GV // HALF)),
                pl.BlockSpec((nb, GLA_CHUNK, LANES), col(im, PF_GR // LANES))]

    return pl.pallas_call(
        _gla_kernel,
        grid=(b // nb, nch),
        in_specs=specs(fwd) + specs(bwd) + [
            pl.BlockSpec((1, 2, LANES, 256), lambda bi, n: (layer, 0, 0, 0)),
            pl.BlockSpec((1, 2, 1, 256), lambda bi, n: (layer, 0, 0, 0)),
        ],
        out_specs=[pl.BlockSpec((nb, GLA_CHUNK, 512), fwd), pl.BlockSpec((nb, GLA_CHUNK, 512), bwd)],
        out_shape=[jax.ShapeDtypeStruct((b, ntok, 512), F32)] * 2,
        scratch_shapes=[pltpu.VMEM((GLA_DV, LANES), F32)] * (nb * 2 * GLA_HEADS),
        compiler_params=_params("arbitrary", "arbitrary"),
        name="gla",
    )(pf, pf, pb, pf, pf, pf, pb, pf, wd, bd)


DIFF_KC = 256
DIFF_BATCH = 2


def _diff_kernel(lam_init, q_ref, k_ref, v_ref, lq_ref, lk_ref, g_ref, o_ref, s_even, s_odd):
    t = pl.program_id(1)
    tq = q_ref.shape[1]
    nkc = k_ref.shape[1] // DIFF_KC
    lane = lax.broadcasted_iota(jnp.int32, (1, LANES), 1)
    comp0 = (lane // (DIFF_DH // 2)) % 2 == 0
    lql = lq_ref[...] * lk_ref[...]
    lam = (jnp.exp(jnp.sum(lql[0:1], axis=1, keepdims=True))
           - jnp.exp(jnp.sum(lql[1:2], axis=1, keepdims=True)) + lam_init)
    s_bufs = (s_even, s_odd)

    units = list(itertools.product(range(q_ref.shape[0]), range(DIFF_HEADS)))

    def stacked_q(u):
        bb, h = units[u]
        q = q_ref[bb, :, h * LANES:(h + 1) * LANES]
        zero = jnp.zeros_like(q)
        return jnp.concatenate([jnp.where(comp0, q, zero), jnp.where(comp0, zero, q)], axis=0)

    def score_chunk(u, qq, ci, m8):
        bb, h = units[u]
        s = _nt(k_ref[bb, ci * DIFF_KC:(ci + 1) * DIFF_KC, h * LANES:(h + 1) * LANES], qq)
        s_bufs[u % 2][ci] = s
        mc = jnp.max(s.reshape(DIFF_KC // 8, 8, 2 * tq), axis=0)
        return mc if m8 is None else jnp.maximum(m8, mc)

    def value_chunk(u, mrow, ci, acc):
        bb, h = units[u]
        p = jnp.exp2(s_bufs[u % 2][ci] - mrow).astype(BF16)
        part = _mm(v_ref[bb, h * DIFF_VROWS:(h + 1) * DIFF_VROWS, ci * DIFF_KC:(ci + 1) * DIFF_KC], p)
        return part if acc is None else acc + part

    def finish(u, acc):
        bb, h = units[u]
        o = acc[:DIFF_DV] / acc[DIFF_DV:DIFF_DV + 1]
        o = o[:, :tq] - lam * o[:, tq:]
        o = o * lax.rsqrt(jnp.mean(o * o, axis=0, keepdims=True) + RMS_EPS) * g_ref[...] * (1.0 - lam_init)
        o_ref[bb, :, h * DIFF_DV:(h + 1) * DIFF_DV] = o.T.astype(BF16)

    def attend(nch):
        qq = stacked_q(0)
        m8 = None
        for ci in range(nch):
            m8 = score_chunk(0, qq, ci, m8)
        for u in range(1, len(units) + 1):
            mrow = jnp.max(m8, axis=0, keepdims=True)
            if u < len(units):
                qq = stacked_q(u)
            acc = m8 = None
            for ci in range(nch):
                if u < len(units):
                    m8 = score_chunk(u, qq, ci, m8)
                acc = value_chunk(u - 1, mrow, ci, acc)
            finish(u - 1, acc)

    @pl.when(t == 0)
    def _():
        attend(1)

    @pl.when(t > 0)
    def _():
        attend(nkc)


def _diff(pb, dv, lam_q, lam_k, norm_g, lam_init):
    b, ntok, _ = pb.shape
    w = HALF
    nt = ntok // TOK_TILE
    nb = DIFF_BATCH if b % DIFF_BATCH == 0 else 1
    scores = pltpu.VMEM((ntok // DIFF_KC, DIFF_KC, 2 * TOK_TILE), F32)
    return pl.pallas_call(
        functools.partial(_diff_kernel, lam_init),
        grid=(b // nb, nt),
        in_specs=[
            pl.BlockSpec((nb, TOK_TILE, w), lambda bi, t: (bi, t, PB_DQ // HALF)),
            pl.BlockSpec((nb, ntok, w), lambda bi, t: (bi, 0, PB_DK // HALF), pipeline_mode=pl.Buffered(1)),
            pl.BlockSpec((nb, DIFF_HEADS * DIFF_VROWS, ntok), lambda bi, t: (bi, 0, 0),
                         pipeline_mode=pl.Buffered(1)),
            pl.BlockSpec((2, DIFF_DH), lambda bi, t: (0, 0)),
            pl.BlockSpec((2, DIFF_DH), lambda bi, t: (0, 0)),
            pl.BlockSpec((DIFF_DV, 1), lambda bi, t: (0, 0)),
        ],
        out_specs=pl.BlockSpec((nb, TOK_TILE, w), lambda bi, t: (bi, t, 0)),
        out_shape=jax.ShapeDtypeStruct((b, ntok, w), BF16),
        scratch_shapes=[scores, scores],
        compiler_params=_params("arbitrary", "arbitrary"),
        name="diff",
    )(pb, pb, dv, lam_q, lam_k, norm_g.reshape(DIFF_DV, 1))


def _na_kernel(rows, q_ref, k_ref, v_ref, b_ref, o_ref):
    t = pl.program_id(1)
    nctx = CTX_LEN // NA_TILE
    r = t - nctx
    rs = jnp.clip(r - NA_WIN_H // 2, 0, rows - NA_WIN_H)
    start = pl.multiple_of(CTX_LEN + rs * GRID_W, GRID_W)
    nloc = NA_WIN_H * GRID_W
    lane = lax.broadcasted_iota(jnp.int32, (1, LANES), 1)
    first = lane < NA_DH
    chains = list(itertools.product(range(q_ref.shape[0]), range(NA_HEADS // 2)))
    psl = [slice(p * LANES, (p + 1) * LANES) for p in range(NA_HEADS // 2)]
    scores = []
    for bb, p in chains:
        q2 = q_ref[bb, :, psl[p]]
        zero = jnp.zeros_like(q2)
        lhs = jnp.concatenate([jnp.where(first, q2, zero), jnp.where(first, zero, q2)], axis=0)
        s_loc = _nt(lhs, k_ref[bb, pl.ds(start, nloc), psl[p]]) + b_ref[p]
        s_ctx = _nt(lhs, k_ref[bb, 0:CTX_LEN, psl[p]])
        scores.append((s_loc, s_ctx))
    probs = []
    for s_loc, s_ctx in scores:
        m = jnp.maximum(jnp.max(s_loc, axis=1, keepdims=True), jnp.max(s_ctx, axis=1, keepdims=True))
        p_loc = jnp.exp2(s_loc - m)
        p_ctx = jnp.exp2(s_ctx - m)
        l = jnp.sum(p_loc, axis=1, keepdims=True) + jnp.sum(p_ctx, axis=1, keepdims=True)
        probs.append((p_loc.astype(BF16), p_ctx.astype(BF16), l))
    for (bb, p), (p_loc, p_ctx, l) in zip(chains, probs):
        o = (_mm(p_loc, v_ref[bb, pl.ds(start, nloc), psl[p]]) + _mm(p_ctx, v_ref[bb, 0:CTX_LEN, psl[p]])) / l
        o_ref[bb, :, psl[p]] = jnp.where(first, o[:NA_TILE], o[NA_TILE:]).astype(BF16)


def _na(pb, bias):
    b, ntok, _ = pb.shape
    w = HALF
    nt = ntok // NA_TILE
    nctx = CTX_LEN // NA_TILE
    rows = (ntok - CTX_LEN) // GRID_W
    nb = NA_BATCH if b % NA_BATCH == 0 else 1

    def cfg(bi, t):
        r = t - nctx
        return (jnp.where(t < nctx, NA_WIN_H, r - jnp.clip(r - NA_WIN_H // 2, 0, rows - NA_WIN_H)), 0, 0)

    def kv_spec(block):
        return pl.BlockSpec((nb, ntok, w), lambda bi, t: (bi, 0, block), pipeline_mode=pl.Buffered(1))

    return pl.pallas_call(
        functools.partial(_na_kernel, rows),
        grid=(b // nb, nt),
        in_specs=[
            pl.BlockSpec((nb, NA_TILE, w), lambda bi, t: (bi, t, PB_NQ // HALF)),
            kv_spec(PB_NK // HALF), kv_spec(PB_NV // HALF),
            pl.BlockSpec((NA_HEADS // 2, 2 * NA_TILE, NA_WIN_H * GRID_W), cfg),
        ],
        out_specs=pl.BlockSpec((nb, NA_TILE, w), lambda bi, t: (bi, t, 0)),
        out_shape=jax.ShapeDtypeStruct((b, ntok, w), BF16),
        compiler_params=_params("arbitrary", "arbitrary"),
        name="na",
    )(pb, pb, pb, bias)


def _na_bias(rpb):
    qc = np.arange(GRID_W)[:, None]
    kc = np.arange(GRID_W)[None, :]
    col_idx = np.clip(kc - qc + NA_WIN_W - 1, 0, 2 * NA_WIN_W - 2)
    cstart = np.clip(qc - NA_WIN_W // 2, 0, GRID_W - NA_WIN_W)
    win = (kc >= cstart) & (kc < cstart + NA_WIN_W)
    toep = jnp.where(win[None, None], rpb.astype(F32)[..., col_idx] * LOG2E, NEG_BIG)
    toep = toep.transpose(0, 2, 1, 3)
    cfgs = [toep[:, :, NA_WIN_H - 1 - c:2 * NA_WIN_H - 1 - c] for c in range(NA_WIN_H)]
    cfgs.append(jnp.full(cfgs[0].shape, NEG_BIG, F32))
    return jnp.stack(cfgs, axis=0).reshape(-1, 2 * GRID_W, NA_WIN_H * GRID_W)


def _layernorm(x, g, b):
    mu = jnp.mean(x, axis=1, keepdims=True)
    xc = x - mu
    var = jnp.mean(xc * xc, axis=1, keepdims=True)
    return xc * lax.rsqrt(var + LN_EPS) * g + b


def _route(logits, bias):
    aff = _sigmoid(logits)
    sel = aff + bias
    srow = [sel[e:e + 1] for e in range(N_EXPERTS)]
    arow = [aff[e:e + 1] for e in range(N_EXPERTS)]
    gscore = []
    for g in range(N_GROUPS):
        a0, a1, a2, a3 = srow[4 * g:4 * g + 4]
        hi01, lo01 = jnp.maximum(a0, a1), jnp.minimum(a0, a1)
        hi23, lo23 = jnp.maximum(a2, a3), jnp.minimum(a2, a3)
        top1 = jnp.maximum(hi01, hi23)
        top2 = jnp.maximum(jnp.minimum(hi01, hi23), jnp.maximum(lo01, lo23))
        gscore.append(top1 + top2)
    best = jnp.zeros_like(gscore[0])
    bestv = gscore[0]
    for g in range(1, N_GROUPS):
        better = gscore[g] > bestv
        best = jnp.where(better, float(g), best)
        bestv = jnp.where(better, gscore[g], bestv)

    def pick(rows_, i):
        out = rows_[i]
        for g in range(1, N_GROUPS):
            out = jnp.where(best == float(g), rows_[4 * g + i], out)
        return out

    s4 = [pick(srow, i) for i in range(4)]
    f4 = [pick(arow, i) for i in range(4)]
    chosen = []
    for i in range(4):
        rank = jnp.zeros_like(best)
        for j in range(4):
            if j == i:
                continue
            ahead = (s4[j] > s4[i]) | ((s4[j] == s4[i]) & (j < i))
            rank = rank + jnp.where(ahead, 1.0, 0.0)
        chosen.append(rank < 2.0)
    c0, c1, c2, c3 = chosen
    pidx = jnp.where(c0, jnp.where(c1, 0.0, jnp.where(c2, 1.0, 2.0)),
                     jnp.where(c1, jnp.where(c2, 3.0, 4.0), 5.0))
    a_lo = jnp.where(c0, f4[0], jnp.where(c1, f4[1], f4[2]))
    a_hi = jnp.where(c3, f4[3], jnp.where(c2, f4[2], f4[1]))
    den = a_lo + a_hi
    return best * float(N_PAIRS) + pidx, a_lo / den, a_hi / den


def _merge_kernel(x_ref, of_ref, ob_ref, gsg_ref, do_ref, no_ref, sg_ref, mod_ref, gg_ref,
                  wb_ref, wo_ref, lng_ref, lnb_ref, wr_ref, br_ref,
                  x1_ref, h2_ref, route_ref, cnt_ref, run_ref):
    first_step = jnp.logical_and(pl.program_id(0) == 0, pl.program_id(1) == 0)

    @pl.when(first_step)
    def _():
        run_ref[...] = jnp.zeros_like(run_ref)

    nb, tm = x_ref.shape[0], x_ref.shape[1]
    tiles = range(nb)
    mods = [mod_ref[bb, 0] for bb in tiles]
    gg = gg_ref[...]

    branches = []
    for bb in tiles:
        o = of_ref[bb] + ob_ref[bb]
        parts = []
        for h in range(GLA_HEADS):
            oh = o[:, h * GLA_DV:(h + 1) * GLA_DV]
            parts.append(oh * lax.rsqrt(jnp.mean(oh * oh, axis=1, keepdims=True) + RMS_EPS) * gg)
        gla = (jnp.concatenate(parts, axis=1) * gsg_ref[bb].astype(F32)).astype(BF16)
        branches.append((gla, do_ref[bb], no_ref[bb]))
    ys = [None] * nb
    for i in range(3):
        for bb in tiles:
            term = sg_ref[bb, :, i * D_MODEL:(i + 1) * D_MODEL].astype(F32) * _mm(branches[bb][i], wb_ref[0, i])
            ys[bb] = term if i == 0 else ys[bb] + term
    ys = [_mm(y.astype(BF16), wo_ref[0]) for y in ys]
    h2s = []
    for bb in tiles:
        x1 = _layernorm(DEEPNORM_ALPHA * x_ref[bb] + mods[bb][2:3] * ys[bb], lng_ref[...], lnb_ref[...])
        x1_ref[bb] = x1
        h2 = x1 * (1.0 + mods[bb][4:5]) + mods[bb][3:4]
        h2_ref[bb, :, :D_MODEL] = h2
        h2s.append(h2)
    logits = [lax.dot_general(wr_ref[...], h2, (((1,), (1,)), ((), ())),
                              precision=HIGHEST, preferred_element_type=F32) for h2 in h2s]
    routes = [_route(lg, br_ref[...]) for lg in logits]

    srows = lax.broadcasted_iota(jnp.int32, (SEG_ROWS, tm), 0).astype(F32)
    ii = lax.broadcasted_iota(jnp.int32, (tm, tm), 0)
    jj = lax.broadcasted_iota(jnp.int32, (tm, tm), 1)
    before = jnp.where(ii < jj, 1.0, 0.0).astype(BF16)
    onehots = [jnp.where(srows == seg, 1.0, 0.0) for seg, _, _ in routes]
    prefixes = [_mm(oh.astype(BF16), before) for oh in onehots]
    run = run_ref[...]
    for bb in tiles:
        seg, w_lo, w_hi = routes[bb]
        rank = jnp.sum(onehots[bb] * (prefixes[bb] + run[:, 0:1]), axis=0, keepdims=True)
        run = run + jnp.sum(onehots[bb], axis=1, keepdims=True)
        zrow = jnp.zeros_like(seg)
        route = jnp.concatenate([seg, rank, w_lo, w_hi, zrow, zrow, zrow, zrow], axis=0)
        route_ref[bb, 0] = route
        wide = jnp.concatenate([route, jnp.zeros((LANES - 8, tm), F32)], axis=0)
        h2_ref[bb, :, D_MODEL:] = wide.T
    run_ref[...] = run
    cnt_ref[...] = run


def _merge(xs, o_f, o_b, pb, d_o, n_o, mod2, gla_g, wb, wo, ln_g, ln_b, wr_t, br, layer):
    b, ntok, d = xs.shape
    nt = ntok // TOK_TILE
    nb = MERGE_BATCH if b % MERGE_BATCH == 0 else 1
    tok = lambda bi, ti: (bi, ti, 0)
    const2 = lambda bi, ti: (0, 0)

    def tokspec(wd):
        return pl.BlockSpec((nb, TOK_TILE, wd), tok)

    return pl.pallas_call(
        _merge_kernel,
        grid=(b // nb, nt),
        in_specs=[
            tokspec(d), tokspec(HALF), tokspec(HALF),
            pl.BlockSpec((nb, TOK_TILE, HALF), lambda bi, ti: (bi, ti, PB_GSG // HALF)),
            tokspec(HALF), tokspec(HALF),
            pl.BlockSpec((nb, TOK_TILE, 3 * d), lambda bi, ti: (bi, ti, PB_SG // (3 * d))),
            pl.BlockSpec((nb, 1, 8, d), lambda bi, ti: (bi, jnp.minimum(ti, 1), 0, 0)),
            pl.BlockSpec((1, GLA_DV), const2),
            pl.BlockSpec((1, 3, BRANCH_W, d), lambda bi, ti: (layer, 0, 0, 0)),
            pl.BlockSpec((1, d, d), lambda bi, ti: (layer, 0, 0)),
            pl.BlockSpec((1, d), const2),
            pl.BlockSpec((1, d), const2),
            pl.BlockSpec((N_EXPERTS, d), const2),
            pl.BlockSpec((N_EXPERTS, 1), const2),
        ],
        out_specs=[
            tokspec(d), tokspec(d + LANES),
            pl.BlockSpec((nb, 1, 8, TOK_TILE), lambda bi, ti: (bi, ti, 0, 0)),
            pl.BlockSpec((SEG_ROWS, LANES), const2),
        ],
        out_shape=[
            jax.ShapeDtypeStruct((b, ntok, d), F32),
            jax.ShapeDtypeStruct((b, ntok, d + LANES), F32),
            jax.ShapeDtypeStruct((b, nt, 8, TOK_TILE), F32),
            jax.ShapeDtypeStruct((SEG_ROWS, LANES), F32),
        ],
        scratch_shapes=[pltpu.VMEM((SEG_ROWS, LANES), F32)],
        compiler_params=_params("arbitrary", "arbitrary"),
        name="merge",
    )(xs, o_f, o_b, pb, d_o, n_o, pb, mod2.reshape(b, 2, 8, d), gla_g.reshape(1, GLA_DV), wb, wo,
      ln_g.reshape(1, d), ln_b.reshape(1, d), wr_t, br.reshape(N_EXPERTS, 1))


def _moe_kernel(dest, meta, h_hbm, wg1, wu1, wd1, wg2, wu2, wd2, y_hbm,
                src, xbuf, ybuf, sem_in, sem_out):
    i = pl.program_id(0)
    ntile = pl.num_programs(0)
    n_used = meta[2 * ntile]
    slot = i % 2
    tm = MOE_TILE

    def n_real(tile):
        return meta[2 * ntile + 1 + tile]

    def gather(tile, sl):
        def body(j, carry):
            tok = src[tile * tm + j]
            pltpu.make_async_copy(h_hbm.at[pl.ds(tok, 1)], xbuf.at[sl, pl.ds(j, 1)], sem_in.at[sl]).start()
            return carry
        lax.fori_loop(0, tm, body, 0, unroll=8)

    def wait_in(sl):
        pltpu.make_async_copy(h_hbm.at[pl.ds(0, tm)], xbuf.at[sl], sem_in.at[sl]).wait()

    def row_out(tile, sl, j):
        tok = src[tile * tm + j]
        return pltpu.make_async_copy(ybuf.at[sl, pl.ds(j, 1)], y_hbm.at[pl.ds(tok, 1)], sem_out.at[sl])

    def scatter(tile, sl):
        nr = n_real(tile)

        def body(j, carry):
            row_out(tile, sl, j).start()
            return carry

        @pl.when(nr == tm)
        def _():
            for j in range(tm):
                row_out(tile, sl, j).start()

        @pl.when(nr < tm)
        def _():
            lax.fori_loop(0, nr, body, 0)

    def wait_out(tile, sl):
        nr = n_real(tile)

        @pl.when(nr == tm)
        def _():
            pltpu.make_async_copy(ybuf.at[sl], y_hbm.at[pl.ds(0, tm)], sem_out.at[sl]).wait()

        @pl.when(nr < tm)
        def _():
            def body(j, carry):
                row_out(tile, sl, j).wait()
                return carry
            lax.fori_loop(0, nr, body, 0)

    @pl.when(i == 0)
    def _():
        def clear(j, carry):
            src[j] = 0
            return carry
        for s in range(N_SEG):
            lax.fori_loop(meta[3 * ntile + 1 + 2 * s], meta[3 * ntile + 2 + 2 * s], clear, 0)
        lax.fori_loop(0, tm, lambda j, carry: clear(n_used * tm + j, carry), 0, unroll=8)

        def place(j, carry):
            src[dest[j]] = j
            return carry
        lax.fori_loop(0, dest.shape[0], place, 0, unroll=8)
        gather(0, 0)

    @pl.when(i < n_used)
    def _():
        wait_in(slot)

        @pl.when(i >= 2)
        def _():
            wait_out(i - 2, slot)

        for j in range(tm):
            tok = src[(i + 1) * tm + j]
            pltpu.make_async_copy(h_hbm.at[pl.ds(tok, 1)], xbuf.at[1 - slot, pl.ds(j, 1)],
                                  sem_in.at[1 - slot]).start()

        x = xbuf[slot, :, :D_MODEL].astype(BF16)
        w = xbuf[slot, :, D_MODEL:]

        a1, u1 = _mm(x, wg1[0]), _mm(x, wu1[0])
        a2, u2 = _mm(x, wg2[0]), _mm(x, wu2[0])
        y1 = _mm((a1 * _sigmoid(a1) * u1).astype(BF16), wd1[0])
        y2 = _mm((a2 * _sigmoid(a2) * u2).astype(BF16), wd2[0])
        ybuf[slot] = w[:, 2:3] * y1 + w[:, 3:4] * y2
        scatter(i, slot)

    @pl.when(i == ntile - 1)
    def _():
        wait_in(n_used % 2)
        wait_out(n_used - 1, (n_used - 1) % 2)

        @pl.when(n_used >= 2)
        def _():
            wait_out(n_used - 2, n_used % 2)


def _moe(h2, dest, meta, wg, wu, wd, layer):
    t, dw = h2.shape
    d = dw - LANES
    ntile = (meta.shape[0] - 1 - 2 * N_SEG) // 3
    lo = lambda i, s, m: (layer * N_EXPERTS + m[2 * i], 0, 0)
    hi = lambda i, s, m: (layer * N_EXPERTS + m[2 * i + 1], 0, 0)
    grid_spec = pltpu.PrefetchScalarGridSpec(
        num_scalar_prefetch=2,
        grid=(ntile,),
        in_specs=[
            pl.BlockSpec(memory_space=pl.ANY),
            pl.BlockSpec((1, d, D_EXPERT), lo), pl.BlockSpec((1, d, D_EXPERT), lo),
            pl.BlockSpec((1, D_EXPERT, d), lo),
            pl.BlockSpec((1, d, D_EXPERT), hi), pl.BlockSpec((1, d, D_EXPERT), hi),
            pl.BlockSpec((1, D_EXPERT, d), hi),
        ],
        out_specs=pl.BlockSpec(memory_space=pl.ANY),
        scratch_shapes=[pltpu.SMEM(((ntile + 1) * MOE_TILE,), jnp.int32),
                        pltpu.VMEM((2, MOE_TILE, dw), F32), pltpu.VMEM((2, MOE_TILE, d), F32),
                        pltpu.SemaphoreType.DMA((2,)), pltpu.SemaphoreType.DMA((2,))],
    )
    return pl.pallas_call(
        _moe_kernel,
        grid_spec=grid_spec,
        out_shape=jax.ShapeDtypeStruct((t, d), F32),
        compiler_params=_params("arbitrary"),
        name="moe",
    )(dest, meta, h2, wg, wu, wd, wg, wu, wd)


def _lookup(idx, table):
    n = table.shape[0]
    hit = idx[:, None] == jnp.arange(n, dtype=jnp.int32)[None, :]
    return jnp.sum(jnp.where(hit, table[None, :], 0), axis=1)


def _dispatch_plan(route, counts, t):
    ntile = (t + N_SEG * (MOE_TILE - 1)) // MOE_TILE
    seg = route[:, :, 0, :].reshape(t).astype(jnp.int32)
    rank = route[:, :, 1, :].reshape(t).astype(jnp.int32)
    cnt = counts[:N_SEG, 0].astype(jnp.int32)
    seg_tiles = (cnt + MOE_TILE - 1) // MOE_TILE
    upto = jnp.arange(N_SEG)[None, :] <= jnp.arange(N_SEG)[:, None]
    tile_end = jnp.sum(jnp.where(upto, seg_tiles[None, :], 0), axis=1)
    first_tile = tile_end - seg_tiles
    dest = _lookup(seg, first_tile * MOE_TILE) + rank
    n_used = tile_end[-1]
    tiles = jnp.arange(ntile, dtype=jnp.int32)
    tile_seg = jnp.sum((tile_end[None, :] <= jnp.minimum(tiles, n_used - 1)[:, None]).astype(jnp.int32), axis=1)
    tile_seg = jnp.minimum(tile_seg, N_SEG - 1)
    base = (tile_seg // N_PAIRS) * EXPERTS_PER_GROUP
    pair = tile_seg % N_PAIRS
    e_lo = base + _lookup(pair, jnp.asarray(PAIR_LO, jnp.int32))
    e_hi = base + _lookup(pair, jnp.asarray(PAIR_HI, jnp.int32))
    left = _lookup(tile_seg, cnt) - (tiles - _lookup(tile_seg, first_tile)) * MOE_TILE
    n_real = jnp.where(tiles < n_used, jnp.clip(left, 0, MOE_TILE), 0)
    pad_lo = first_tile * MOE_TILE + cnt
    pad_hi = tile_end * MOE_TILE
    meta = jnp.concatenate([jnp.stack([e_lo, e_hi], axis=1).reshape(-1), n_used[None], n_real,
                            jnp.stack([pad_lo, pad_hi], axis=1).reshape(-1)]).astype(jnp.int32)
    return dest, meta


def _final_kernel(x_ref, y_ref, mod_ref, g_ref, b_ref, o_ref):
    mod = mod_ref[0]
    o_ref[0] = _layernorm(DEEPNORM_ALPHA * x_ref[0] + mod[5:6] * y_ref[...], g_ref[...], b_ref[...])


def _final(x1, y, mod2, ln_g, ln_b, latents_only):
    b, ntok, d = x1.shape
    nt = ntok // TOK_TILE
    skip = CTX_LEN // TOK_TILE if latents_only else 0
    return pl.pallas_call(
        _final_kernel,
        grid=(b, nt - skip),
        in_specs=[
            pl.BlockSpec((1, TOK_TILE, d), lambda bi, ti: (bi, ti + skip, 0)),
            pl.BlockSpec((TOK_TILE, d), lambda bi, ti: (bi * nt + ti + skip, 0)),
            pl.BlockSpec((1, 8, d), lambda bi, ti: (2 * bi + jnp.minimum(ti + skip, 1), 0, 0)),
            pl.BlockSpec((1, d), lambda bi, ti: (0, 0)),
            pl.BlockSpec((1, d), lambda bi, ti: (0, 0)),
        ],
        out_specs=pl.BlockSpec((1, TOK_TILE, d), lambda bi, ti: (bi, ti, 0)),
        out_shape=jax.ShapeDtypeStruct((b, ntok - skip * TOK_TILE, d), F32),
        compiler_params=_params("arbitrary", "arbitrary"),
        name="final_ln",
    )(x1, y, mod2, ln_g.reshape(1, d), ln_b.reshape(1, d))


def _rotary_order(w):
    lead = w.shape[:-1]
    half = DIFF_DH // 2
    return w.reshape(lead + (DIFF_HEADS, 2, 2, half)).swapaxes(-2, -3).reshape(lead + (DIFF_HEADS * 2 * DIFF_DH,))


def _pack_w_in(w_in):
    splits = np.cumsum([256, 256, 512, 512, 32, 512, 512, 512, 512, 512, 512])
    gq, gk, gv, gg, gr, dq, dk, dv, nq, nk, nv, sg = jnp.split(w_in.astype(BF16), splits, axis=-1)
    gr = jnp.pad(gr, ((0, 0), (0, 0), (0, LANES - 2 * GLA_RANK)))
    return jnp.concatenate([gq, gk, gv, gg, gr, _rotary_order(dq), _rotary_order(dk), dv, nq, nk, nv, sg],
                           axis=-1)


def _rope_tables(n_lat):
    t = jnp.arange(n_lat)
    row = (t // GRID_W).astype(F32)
    col = (t % GRID_W).astype(F32)
    n_freq = DIFF_DH // 4
    inv = ROPE_BASE ** (-jnp.arange(n_freq, dtype=F32) / n_freq)
    ang = jnp.concatenate([row[:, None] * inv, col[:, None] * inv], -1)
    cos, sin = jnp.cos(ang), jnp.sin(ang)
    cos_t = jnp.concatenate([cos] * 4, axis=1)
    sin_t = jnp.concatenate([-sin, -sin, sin, sin], axis=1)
    cos_t = jnp.concatenate([jnp.ones((CTX_LEN, LANES), F32), cos_t], axis=0)
    sin_t = jnp.concatenate([jnp.zeros((CTX_LEN, LANES), F32), sin_t], axis=0)
    return cos_t, sin_t


def _pack_decay(w_decay, b_decay):
    depth = w_decay.shape[0]
    wd = jnp.zeros((depth, 2, LANES, GLA_HEADS * GLA_DK), F32)
    for d in range(2):
        wd = wd.at[:, d, d * GLA_RANK:(d + 1) * GLA_RANK].set(w_decay[:, d])
    return wd.astype(BF16), b_decay.reshape(depth, 2, 1, GLA_HEADS * GLA_DK)


def kernel(x, c, ctx, c_ctx, w_ada, b_ada, w_in, gla_w_decay, gla_b_decay, gla_norm_g, diff_lam_q,
           diff_lam_k, diff_norm_g, na_rpb, w_branch, w_o, ln_g, ln_b, w_router, b_router,
           w_exp_gate, w_exp_up, w_exp_down):
    b, l, d = x.shape
    lc = ctx.shape[1]
    ntok = lc + l
    t = b * ntok

    w_in_p = _pack_w_in(w_in)
    wd_p, bd_p = _pack_decay(gla_w_decay, gla_b_decay)
    cos_t, sin_t = _rope_tables(l)
    wb = w_branch.astype(BF16)
    wo = w_o.astype(BF16)
    wg = w_exp_gate.astype(BF16).reshape(DEPTH * N_EXPERTS, d, D_EXPERT)
    wu = w_exp_up.astype(BF16).reshape(DEPTH * N_EXPERTS, d, D_EXPERT)
    wdn = w_exp_down.astype(BF16).reshape(DEPTH * N_EXPERTS, D_EXPERT, d)
    wr_t = w_router.T

    cs = jnp.concatenate([c, c_ctx[None], jnp.zeros((16 - b - 1, d), F32)], axis=0)
    mods = _ada(cs, w_ada, b_ada).reshape(DEPTH, 16, 6, d)

    xs = jnp.concatenate([ctx, x], axis=1)
    prev = None
    for layer in range(DEPTH):
        lam_init = 0.8 - 0.6 * math.exp(-0.3 * layer)
        m_lat = mods[layer, :b]
        m_ctx = jnp.broadcast_to(mods[layer, b][None], (b, 6, d))
        mod2 = jnp.stack([m_ctx, m_lat], axis=1).reshape(2 * b, 6, d)
        mod2 = jnp.pad(mod2, ((0, 0), (0, 2), (0, 0)))

        xs, (pf, pb, dv) = _proj(xs, mod2, w_in_p, cos_t, sin_t, layer, prev)
        o_f, o_b = _gla(pf, pb, wd_p, bd_p, layer)
        d_o = _diff(pb, dv, diff_lam_q[layer], diff_lam_k[layer], diff_norm_g[layer], lam_init)
        n_o = _na(pb, _na_bias(na_rpb[layer]))
        x1, h2, route, counts = _merge(xs, o_f, o_b, pb, d_o, n_o, mod2, gla_norm_g[layer],
                                       wb, wo, ln_g[layer, 0], ln_b[layer, 0], wr_t, b_router, layer)
        dest, meta = _dispatch_plan(route, counts, t)
        y = _moe(h2.reshape(t, d + LANES), dest, meta, wg, wu, wdn, layer)
        prev = (x1, y, mod2, ln_g[layer, 1], ln_b[layer, 1])
    return _final(*prev, latents_only=True)
```
